```python
import jax, jax.numpy as jnp
from jax import lax
import numpy as np

D_MODEL = 1024
BATCH = 2
SEQ = 8192
DEPTH = 2

CTX_LEN = 256
GRID_W = 64
D_BRANCH = D_MODEL // 2
N_EVEN = (DEPTH + 1) // 2
N_ODD = DEPTH // 2
POOL_WINDOWS = (2, 4, 8, 16)
POOL_GROUP = D_BRANCH // len(POOL_WINDOWS)
SHORT_CONV_W = 3
RWKV_HEAD = 64
RWKV_HEADS = D_BRANCH // RWKV_HEAD
D_DECAY_LORA = 32
D_AAA_LORA = 32
D_GATE_LORA = 96
CONF_CONV_W = 31
EVEN_IN = 6 * D_BRANCH
ODD_IN = 8 * D_BRANCH
NORM_EPS = 1e-6
LNX_EPS = 64e-5
CONF_LN_EPS = 1e-5

kernel_name = "hybrid_pool_shortconv_rwkv7_conformer_dit"


def rmsnorm(x, g):
    xf = x.astype(jnp.float32)
    y = xf * lax.rsqrt(jnp.mean(xf * xf, axis=-1, keepdims=True) + NORM_EPS)
    return (y * g.astype(jnp.float32)).astype(x.dtype)


def layernorm(x, g, b, eps):
    xf = x.astype(jnp.float32)
    mu = jnp.mean(xf, axis=-1, keepdims=True)
    var = jnp.mean(jnp.square(xf - mu), axis=-1, keepdims=True)
    y = (xf - mu) * lax.rsqrt(var + eps) * g.astype(jnp.float32) + b.astype(jnp.float32)
    return y.astype(x.dtype)


def adaln(cvec, w, b):
    m = jax.nn.silu(cvec) @ w + b
    return jnp.split(m[..., None, :], 3, axis=-1)


def modulate(x, g, shift, scale):
    return rmsnorm(x, g) * (1.0 + scale) + shift


def to_lines(h, layout):
    if layout == "seq":
        return h
    b, t, ch = h.shape
    rows = t // GRID_W
    g = h.reshape(b, rows, GRID_W, ch)
    if layout == "rows":
        return g.reshape(b * rows, GRID_W, ch)
    return g.transpose(0, 2, 1, 3).reshape(b * GRID_W, rows, ch)


def from_lines(y, layout, b):
    if layout == "seq":
        return y
    ch = y.shape[-1]
    if layout == "rows":
        return y.reshape(b, -1, ch)
    rows = y.shape[1]
    return y.reshape(b, GRID_W, rows, ch).transpose(0, 2, 1, 3).reshape(b, rows * GRID_W, ch)


def dwconv(u, w):
    k = w.shape[0]
    return lax.conv_general_dilated(
        u, w[:, None, :].astype(u.dtype), window_strides=(1,), padding=[(k // 2, k // 2)],
        dimension_numbers=("NWC", "WIO", "NWC"), feature_group_count=u.shape[-1])


def centred_mean(u, window):
    l = u.shape[1]
    cs = jnp.pad(jnp.cumsum(u.astype(jnp.float32), axis=1), ((0, 0), (1, 0), (0, 0)))
    t = jnp.arange(l)
    lo = jnp.clip(t - window // 2, 0, l)
    hi = jnp.clip(t - window // 2 + window, 0, l)
    s = jnp.take(cs, hi, axis=1) - jnp.take(cs, lo, axis=1)
    return (s / (hi - lo).astype(jnp.float32)[None, :, None]).astype(u.dtype)


def even_mix(h, w_in, w_out, pool_w, pool_scale, sconv_w, grid):
    b = h.shape[0]
    line = "rows" if grid else "seq"
    u_a, z_a, v_b, g_b, g_c, z_b = jnp.split(h @ w_in, 6, axis=-1)
    ua = to_lines(u_a, line)
    n, l, _ = ua.shape
    ug = ua.reshape(n, l, len(POOL_WINDOWS), POOL_GROUP)
    pooled = jnp.stack([centred_mean(ug[:, :, i], w) for i, w in enumerate(POOL_WINDOWS)], axis=2) - ug
    y_a = jnp.einsum("nlgc,gcd->nlgd", pooled, pool_w).reshape(n, l, D_BRANCH)
    y_a = from_lines(y_a, line, b) * pool_scale
    y_b = g_b * from_lines(dwconv(to_lines(g_c * v_b, line), sconv_w), line, b)
    y = jnp.concatenate([y_a * jax.nn.silu(z_a), y_b * jax.nn.silu(z_b)], axis=-1)
    return y @ w_out


def token_shift_mix(z, mu):
    zp = jnp.pad(z, ((0, 0), (1, 1), (0, 0)))
    nb = 0.5 * (zp[:, :-2] + zp[:, 2:])
    return z + (nb - z) * mu


def wkv_scan(r, w, k, v, a, bb, s0, reverse):
    def step(s, inp):
        r_t, w_t, k_t, v_t, a_t, b_t = inp
        sa = jnp.einsum("bhvk,bhk->bhv", s, a_t)
        s = s * w_t[:, :, None, :] + sa[..., None] * b_t[:, :, None, :] + v_t[..., None] * k_t[:, :, None, :]
        return s, jnp.einsum("bhvk,bhk->bhv", s, r_t)
    xs = tuple(jnp.moveaxis(z.astype(jnp.float32), 1, 0) for z in (r, w, k, v, a, bb))
    s_fin, o = lax.scan(step, s0, xs, reverse=reverse)
    return s_fin, jnp.moveaxis(o, 0, 1)


def rwkv_branch(u, r, k, v, mu, w0, w1, w2, a0, a1, a2, g1, g2, k_k, k_a, r_k, lnx_g, lnx_b, s0):
    b, l, _ = u.shape
    heads = lambda z: z.reshape(b, l, RWKV_HEADS, RWKV_HEAD)
    r = token_shift_mix(r, mu[0])
    k = token_shift_mix(k, mu[1])
    v = token_shift_mix(v, mu[2])
    uw = token_shift_mix(u, mu[3])
    ua = token_shift_mix(u, mu[4])
    ugt = token_shift_mix(u, mu[5])
    g = jax.nn.sigmoid(ugt @ g1) @ g2
    kk = heads(k * k_k).astype(jnp.float32)
    kk = kk / jnp.maximum(jnp.sqrt(jnp.sum(kk * kk, axis=-1, keepdims=True)), 1e-12)
    r_h = heads(r).astype(jnp.float32)
    v_h = heads(v).astype(jnp.float32)
    outs, finals, k_dirs = [], [], []
    for d in range(2):
        logw = -jax.nn.softplus(-(w0[d] + jnp.tanh(uw @ w1[d]) @ w2[d])) - 0.5
        decay = jnp.exp(-jnp.exp(logw.astype(jnp.float32)))
        a = jax.nn.sigmoid(a0[d] + (ua @ a1[d]) @ a2[d])
        k_d = heads(k * (1.0 + (a - 1.0) * k_a)).astype(jnp.float32)
        a_h = heads(a).astype(jnp.float32)
        init = jnp.zeros((b, RWKV_HEADS, RWKV_HEAD, RWKV_HEAD), jnp.float32) if s0 is None else s0[d]
        s_fin, o = wkv_scan(r_h, heads(decay), k_d, v_h, -kk, kk * a_h, init, reverse=(d == 1))
        outs.append(o)
        finals.append(s_fin)
        k_dirs.append(k_d)
    o = outs[0] + outs[1]
    mu_o = jnp.mean(o, axis=-1, keepdims=True)
    var_o = jnp.mean(jnp.square(o - mu_o), axis=-1, keepdims=True)
    o = ((o - mu_o) * lax.rsqrt(var_o + LNX_EPS)).reshape(b, l, D_BRANCH)
    o = o * lnx_g.astype(jnp.float32) + lnx_b.astype(jnp.float32)
    bonus = jnp.sum(r_h * (k_dirs[0] + k_dirs[1]) * r_k.astype(jnp.float32), axis=-1, keepdims=True) * v_h
    y = (o + bonus.reshape(b, l, D_BRANCH)) * g.astype(jnp.float32)
    return y.astype(u.dtype), (finals[0], finals[1])


def conformer_branch(p1, p2, dw_w, dw_b, ln_g, ln_b, line):
    b = p1.shape[0]
    h = p1 * jax.nn.sigmoid(p2)
    h = from_lines(dwconv(to_lines(h, line), dw_w), line, b) + dw_b
    return jax.nn.silu(layernorm(h, ln_g, ln_b, CONF_LN_EPS))


def odd_merge(y_c, z_c, p1, p2, z_d, w_out, dw_w, dw_b, ln_g, ln_b, grid):
    y_d = conformer_branch(p1, p2, dw_w, dw_b, ln_g, ln_b, "cols" if grid else "seq")
    y = jnp.concatenate([y_c * jax.nn.silu(z_c), y_d * jax.nn.silu(z_d)], axis=-1)
    return y @ w_out


def setup_inputs(seed: int = 0) -> dict:
    key = jax.random.key(seed)
    ks = iter(jax.random.split(key, 48))
    nrm = lambda shape, s: jax.random.normal(next(ks), shape, jnp.float32) * s
    uni = lambda shape, lo, hi: jax.random.uniform(next(ks), shape, jnp.float32, lo, hi)
    D, DB, ne, no = D_MODEL, D_BRANCH, N_EVEN, N_ODD
    return {
        "x": nrm((BATCH, SEQ, D), 1.0),
        "c": nrm((BATCH, D), 1.0),
        "ctx": nrm((BATCH, CTX_LEN, D), 1.0),
        "c_ctx": nrm((D,), 1.0),
        "ada_w_e": nrm((ne, D, 3 * D), 0.5 * D ** -0.5),
        "ada_b_e": nrm((ne, 3 * D), 0.01),
        "norm_e": 1.0 + nrm((ne, D), 0.02),
        "in_e": nrm((ne, D, EVEN_IN), D ** -0.5),
        "out_e": nrm((ne, 2 * DB, D), (2 * DB) ** -0.5),
        "pool_w": nrm((ne, len(POOL_WINDOWS), POOL_GROUP, POOL_GROUP), POOL_GROUP ** -0.5),
        "pool_scale": 1.0 + nrm((ne, DB), 0.02),
        "sconv_w": nrm((ne, SHORT_CONV_W, DB), SHORT_CONV_W ** -0.5),
        "ada_w_o": nrm((no, D, 3 * D), 0.5 * D ** -0.5),
        "ada_b_o": nrm((no, 3 * D), 0.01),
        "norm_o": 1.0 + nrm((no, D), 0.02),
        "in_o": nrm((no, D, ODD_IN), D ** -0.5),
        "out_o": nrm((no, 2 * DB, D), (2 * DB) ** -0.5),
        "rwkv_mu": uni((no, 6, DB), 0.0, 1.0),
        "w0": uni((no, 2, DB), -6.5, -1.5),
        "w1": nrm((no, 2, DB, D_DECAY_LORA), DB ** -0.5),
        "w2": nrm((no, 2, D_DECAY_LORA, DB), 0.1 * D_DECAY_LORA ** -0.5),
        "a0": nrm((no, 2, DB), 0.1),
        "a1": nrm((no, 2, DB, D_AAA_LORA), DB ** -0.5),
        "a2": nrm((no, 2, D_AAA_LORA, DB), 0.1 * D_AAA_LORA ** -0.5),
        "g1": nrm((no, DB, D_GATE_LORA), DB ** -0.5),
        "g2": nrm((no, D_GATE_LORA, DB), D_GATE_LORA ** -0.5),
        "k_k": 0.85 + nrm((no, DB), 0.05),
        "k_a": 1.0 + nrm((no, DB), 0.05),
        "r_k": nrm((no, RWKV_HEADS, RWKV_HEAD), 0.1),
        "lnx_g": 1.0 + nrm((no, DB), 0.02),
        "lnx_b": nrm((no, DB), 0.01),
        "conf_dw_w": nrm((no, CONF_CONV_W, DB), CONF_CONV_W ** -0.5),
        "conf_dw_b": nrm((no, DB), 0.01),
        "conf_ln_g": 1.0 + nrm((no, DB), 0.02),
        "conf_ln_b": nrm((no, DB), 0.01),
        "final_g": 1.0 + nrm((D,), 0.02),
    }


def reference(x, c, ctx, c_ctx,
              ada_w_e, ada_b_e, norm_e, in_e, out_e, pool_w, pool_scale, sconv_w,
              ada_w_o, ada_b_o, norm_o, in_o, out_o, rwkv_mu, w0, w1, w2, a0, a1, a2, g1, g2,
              k_k, k_a, r_k, lnx_g, lnx_b, conf_dw_w, conf_dw_b, conf_ln_g, conf_ln_b,
              final_g):
    xc = ctx
    for i in range(DEPTH):
        j = i // 2
        need_ctx = i < DEPTH - 1
        if i % 2 == 0:
            sh, sc, gt = adaln(c, ada_w_e[j], ada_b_e[j])
            h = modulate(x, norm_e[j], sh, sc)
            y = even_mix(h, in_e[j], out_e[j], pool_w[j], pool_scale[j], sconv_w[j], grid=True)
            if need_ctx:
                shc, scc, gtc = adaln(c_ctx, ada_w_e[j], ada_b_e[j])
                hc = modulate(xc, norm_e[j], shc, scc)
                xc = xc + gtc * even_mix(hc, in_e[j], out_e[j], pool_w[j], pool_scale[j], sconv_w[j], grid=False)
            x = x + gt * y
        else:
            rp = (rwkv_mu[j], w0[j], w1[j], w2[j], a0[j], a1[j], a2[j], g1[j], g2[j],
                  k_k[j], k_a[j], r_k[j], lnx_g[j], lnx_b[j])
            cp = (out_o[j], conf_dw_w[j], conf_dw_b[j], conf_ln_g[j], conf_ln_b[j])
            shc, scc, gtc = adaln(c_ctx, ada_w_o[j], ada_b_o[j])
            hc = modulate(xc, norm_o[j], shc, scc)
            uc, rc, kc, vc, zcc, p1c, p2c, zdc = jnp.split(hc @ in_o[j], 8, axis=-1)
            yc_rwkv, ctx_states = rwkv_branch(uc, rc, kc, vc, *rp, None)
            sh, sc, gt = adaln(c, ada_w_o[j], ada_b_o[j])
            h = modulate(x, norm_o[j], sh, sc)
            u, r, k, v, zc, p1, p2, zd = jnp.split(h @ in_o[j], 8, axis=-1)
            y_rwkv, _ = rwkv_branch(u, r, k, v, *rp, ctx_states)
            y = odd_merge(y_rwkv, zc, p1, p2, zd, *cp, grid=True)
            if need_ctx:
                xc = xc + gtc * odd_merge(yc_rwkv, zcc, p1c, p2c, zdc, *cp, grid=False)
            x = x + gt * y
    return rmsnorm(x, final_g)
```

```python
import functools

import numpy as np
import jax
import jax.numpy as jnp
from jax import lax
from jax.experimental import pallas as pl
from jax.experimental.pallas import tpu as pltpu

F32 = jnp.float32
BF16 = jnp.bfloat16

GRID_W = 64
POOL_WINDOWS = (2, 4, 8, 16)
SHORT_CONV_W = 3
CONF_CONV_W = 31
RWKV_HEAD = 64
NORM_EPS = 1e-6
LNX_EPS = 64e-5
CONF_LN_EPS = 1e-5

CHUNK = 64
HEADS_PER_GROUP = 4
GROUP = HEADS_PER_GROUP * RWKV_HEAD
LORA_PAD = 128
SUBLANES = 8
VMEM_LIMIT = 52 * 1024 * 1024

TILE_TOKENS = 256
PREP_TOKENS = 128
CONV_ROWS = 128


def _sigmoid(x):
    return 1.0 / (1.0 + jnp.exp(-x))


def _silu(x):
    return x * _sigmoid(x)


def _split(x, n):
    if x.dtype == BF16:
        return [x]
    parts, r = [], x
    for i in range(n):
        p = r.astype(BF16)
        parts.append(p)
        if i + 1 < n:
            r = r - p.astype(F32)
    return parts


def _dot(a, b, nt):
    dims = (((1,), (1,)), ((), ())) if nt else (((1,), (0,)), ((), ()))
    return lax.dot_general(a, b, dims, preferred_element_type=F32)


def _mmp(ap, bp, nt=False):
    order = max(len(ap), len(bp))
    acc = None
    for i in reversed(range(len(ap))):
        for j in reversed(range(len(bp))):
            if i + j < order:
                t = _dot(ap[i], bp[j], nt)
                acc = t if acc is None else acc + t
    return acc


def _mm(a, b, na=1, nb=1, nt=False):
    return _mmp(_split(a, na), _split(b, nb), nt)


def _rmsnorm(x, g):
    ms = jnp.mean(x * x, axis=-1, keepdims=True)
    return x * lax.rsqrt(ms + NORM_EPS) * g


def _modulated(x, g, mod_row, d):
    shift, scale = mod_row[:, 0:d], mod_row[:, d:2 * d]
    return _rmsnorm(x, g) * (1.0 + scale) + shift


def _mod_row(mod_ref, ctx):
    row = 2 if ctx else pl.program_id(0)
    return mod_ref[pl.ds(row, 1), :]


def _adaln_kernel(c_ref, w_ref, b_ref, o_ref):
    o_ref[...] = _mm(_silu(c_ref[...]), w_ref[...], 2, 2) + b_ref[...]


def _adaln(cvec, w, b):
    d, n = w.shape
    tn = 768
    return pl.pallas_call(
        _adaln_kernel,
        grid=(n // tn,),
        in_specs=[pl.BlockSpec((SUBLANES, d), lambda j: (0, 0)),
                  pl.BlockSpec((d, tn), lambda j: (0, j)),
                  pl.BlockSpec((1, tn), lambda j: (0, j))],
        out_specs=pl.BlockSpec((SUBLANES, tn), lambda j: (0, j)),
        out_shape=jax.ShapeDtypeStruct((SUBLANES, n), F32),
        compiler_params=pltpu.CompilerParams(dimension_semantics=("parallel",),
                                             vmem_limit_bytes=VMEM_LIMIT),
        name="adaln",
    )(cvec, w, b.reshape(1, n))


def _even_kernel(x_ref, mod_ref, g_ref, win_ref, wout_ref, poolw_ref, pscale_ref, sconv_ref,
                 band_ref, invc_ref, o_ref, *, ctx, line):
    tm, d = x_ref.shape
    db = d // 2
    grp = db // len(POOL_WINDOWS)
    x = x_ref[...]
    mrow = _mod_row(mod_ref, ctx)
    h = _modulated(x, g_ref[...], mrow, d)
    proj = _mm(h, win_ref[...])
    u_a, z_a = proj[:, 0:db], proj[:, db:2 * db]
    v_b, g_b = proj[:, 2 * db:3 * db], proj[:, 3 * db:4 * db]
    g_c, z_b = proj[:, 4 * db:5 * db], proj[:, 5 * db:6 * db]

    ya = []
    for i in range(len(POOL_WINDOWS)):
        ug = u_a[:, i * grp:(i + 1) * grp]
        sums = _mm(band_ref[i], ug, 1, 2)
        pooled = sums * invc_ref[i] - ug
        ya.append(_mm(pooled, poolw_ref[i]))
    y_a = jnp.concatenate(ya, axis=-1) * pscale_ref[...] * _silu(z_a)

    q = g_c * v_b
    pos = lax.broadcasted_iota(jnp.int32, q.shape, 0) % line
    q_prev = jnp.where(pos >= 1, pltpu.roll(q, 1, 0), 0.0)
    q_next = jnp.where(pos <= line - 2, pltpu.roll(q, tm - 1, 0), 0.0)
    w = sconv_ref[...]
    conv = q_prev * w[0:1, :] + q * w[1:2, :] + q_next * w[2:3, :]
    y_b = g_b * conv * _silu(z_b)

    y = _mm(jnp.concatenate([y_a, y_b], axis=-1), wout_ref[...])
    o_ref[...] = x + mrow[:, 2 * d:3 * d] * y


def _pool_constants(tm, line):
    pos = np.arange(tm) % line
    lid = np.arange(tm) // line
    bands, invs = [], []
    for w in POOL_WINDOWS:
        lo = np.clip(pos - w // 2, 0, line)
        hi = np.clip(pos - w // 2 + w, 0, line)
        same = lid[:, None] == lid[None, :]
        band = same & (pos[None, :] >= lo[:, None]) & (pos[None, :] < hi[:, None])
        bands.append(band.astype(np.float32))
        invs.append(np.broadcast_to((1.0 / (hi - lo))[:, None], (tm, 128)).astype(np.float32))
    return jnp.asarray(np.stack(bands), BF16), jnp.asarray(np.stack(invs), F32)


def _even_layer(x, mod, norm_g, w_in, w_out, pool_w, pool_scale, sconv_w, *, ctx, line):
    b, t, d = x.shape
    tm = min(TILE_TOKENS, t)
    assert t % tm == 0 and tm % line == 0
    db = d // 2
    band, invc = _pool_constants(tm, line)
    full = lambda a: pl.BlockSpec(a.shape, lambda bi, i: (0,) * a.ndim)
    args = (mod, norm_g.reshape(1, d), w_in.astype(BF16), w_out.astype(BF16), pool_w.astype(BF16),
            pool_scale.reshape(1, db), sconv_w, band, invc)
    return pl.pallas_call(
        functools.partial(_even_kernel, ctx=ctx, line=line),
        grid=(b, t // tm),
        in_specs=[pl.BlockSpec((None, tm, d), lambda bi, i: (bi, i, 0))] + [full(a) for a in args],
        out_specs=pl.BlockSpec((None, tm, d), lambda bi, i: (bi, i, 0)),
        out_shape=jax.ShapeDtypeStruct((b, t, d), F32),
        compiler_params=pltpu.CompilerParams(dimension_semantics=("parallel", "parallel"),
                                             vmem_limit_bytes=VMEM_LIMIT),
        name="even_ctx" if ctx else "even_latent",
    )(x, *args)


def _inproj_kernel(x_ref, mod_ref, g_ref, w_ref, o_ref, *, ctx):
    d = x_ref.shape[-1]
    h = _modulated(x_ref[...], g_ref[...], _mod_row(mod_ref, ctx), d)
    o_ref[...] = _mm(h, w_ref[...])


def _inproj(x, mod, norm_g, w_in, *, ctx):
    b, t, d = x.shape
    n = w_in.shape[1]
    tm = min(TILE_TOKENS, t)
    assert t % tm == 0
    return pl.pallas_call(
        functools.partial(_inproj_kernel, ctx=ctx),
        grid=(b, t // tm),
        in_specs=[pl.BlockSpec((None, tm, d), lambda bi, i: (bi, i, 0)),
                  pl.BlockSpec(mod.shape, lambda bi, i: (0, 0)),
                  pl.BlockSpec((1, d), lambda bi, i: (0, 0)),
                  pl.BlockSpec((d, n), lambda bi, i: (0, 0))],
        out_specs=pl.BlockSpec((None, tm, n), lambda bi, i: (bi, i, 0)),
        out_shape=jax.ShapeDtypeStruct((b, t, n), F32),
        compiler_params=pltpu.CompilerParams(dimension_semantics=("parallel", "parallel"),
                                             vmem_limit_bytes=VMEM_LIMIT),
        name="inproj_ctx" if ctx else "inproj_latent",
    )(x, mod, norm_g.reshape(1, d), w_in.astype(BF16))


def _prep_kernel(pm_ref, pp_ref, pn_ref, mu_ref, w0_ref, a0_ref, w1_ref, w2_ref, a1_ref, a2_ref,
                 g1_ref, g2_ref, kk_ref, ka_ref, rk_ref, seg_ref, tri_ref, blk_ref,
                 v_ref, g_ref, bonus_ref, at_ref, rt_ref, bt_ref, kt_ref, gc_ref):
    tm = pm_ref.shape[0]
    db = v_ref.shape[-1]
    i, n = pl.program_id(1), pl.num_programs(1)
    z = pm_ref[...]
    rows = lax.broadcasted_iota(jnp.int32, z.shape, 0)
    prev_row = pp_ref[SUBLANES - 1:SUBLANES, :] * jnp.where(i > 0, 1.0, 0.0)
    next_row = pn_ref[0:1, :] * jnp.where(i < n - 1, 1.0, 0.0)
    z_prev = jnp.where(rows == 0, prev_row, pltpu.roll(z, 1, 0))
    z_next = jnp.where(rows == tm - 1, next_row, pltpu.roll(z, tm - 1, 0))
    dz = 0.5 * (z_prev + z_next) - z
    mu = mu_ref[...]
    u, du = z[:, 0:db], dz[:, 0:db]
    r = z[:, db:2 * db] + dz[:, db:2 * db] * mu[0:1, :]
    k = z[:, 2 * db:3 * db] + dz[:, 2 * db:3 * db] * mu[1:2, :]
    v = z[:, 3 * db:4 * db] + dz[:, 3 * db:4 * db] * mu[2:3, :]
    uw = u + du * mu[3:4, :]
    ua = u + du * mu[4:5, :]
    ug = u + du * mu[5:6, :]

    g_ref[...] = _mm(_sigmoid(_mm(ug, g1_ref[...])), g2_ref[...])
    v_ref[...] = v
    lw_pre = w0_ref[...] + _mm(jnp.tanh(_mm(uw, w1_ref[...])), w2_ref[...])
    a_pre = a0_ref[...] + _mm(_mm(ua, a1_ref[...]), a2_ref[...])

    kk = k * kk_ref[...]
    norm = jnp.sqrt(_mm(kk * kk, seg_ref[...], 3, 1))
    kk = kk / jnp.maximum(norm, 1e-12)

    ksum = None
    for d in range(2):
        x = -lw_pre[:, d * db:(d + 1) * db]
        softplus = jnp.maximum(x, 0.0) + jnp.log(1.0 + jnp.exp(-jnp.abs(x)))
        lw = -jnp.exp(-softplus - 0.5)
        a = _sigmoid(a_pre[:, d * db:(d + 1) * db])
        k_d = k * (1.0 + (a - 1.0) * ka_ref[...])
        ksum = k_d if ksum is None else ksum + k_d
        lw_parts = _split(lw, 3)
        cl = _mmp([tri_ref[d]], lw_parts)
        tot = _mmp([blk_ref[...]], lw_parts)
        at_ref[d] = -kk * jnp.exp(cl - lw)
        rt_ref[d] = r * jnp.exp(cl)
        ginv = jnp.exp(-cl)
        bt_ref[d] = kk * a * ginv
        kt_ref[d] = k_d * ginv
        gam = jnp.exp(tot)
        for c in range(tm // CHUNK):
            gc_ref[d, c] = gam[c * CHUNK:c * CHUNK + SUBLANES, :]
    bonus_ref[...] = _mm(r * ksum * rk_ref[...], seg_ref[...], 3, 1) * v


def _prep_constants(tm):
    t = np.arange(tm)
    same = (t[:, None] // CHUNK) == (t[None, :] // CHUNK)
    lower = same & (t[None, :] <= t[:, None])
    upper = same & (t[None, :] >= t[:, None])
    tri = np.stack([lower, upper]).astype(np.float32)
    return jnp.asarray(tri, BF16), jnp.asarray(same.astype(np.float32), BF16)


def _pad_to(a, axis, size):
    pad = [(0, 0)] * a.ndim
    pad[axis] = (0, size - a.shape[axis])
    return jnp.pad(a, pad)


def _rwkv_prep(proj, p):
    b, t, _ = proj.shape
    db = p["k_k"].shape[-1]
    tm = min(PREP_TOKENS, t)
    assert t % tm == 0 and tm % CHUNK == 0
    nc = t // CHUNK
    nb = t // SUBLANES
    per = tm // SUBLANES
    tri, blk = _prep_constants(tm)
    heads = db // RWKV_HEAD
    seg = jnp.asarray(np.kron(np.eye(heads), np.ones((RWKV_HEAD, RWKV_HEAD))), BF16)

    def lora_pair(w_a, w_b):
        rank = w_a.shape[-1]
        down = _pad_to(jnp.concatenate([w_a[0], w_a[1]], axis=-1), 1, LORA_PAD)
        up = jnp.zeros((LORA_PAD, 2 * db), F32)
        up = up.at[0:rank, 0:db].set(w_b[0]).at[rank:2 * rank, db:2 * db].set(w_b[1])
        return down.astype(BF16), up.astype(BF16)

    w1c, w2c = lora_pair(p["w1"], p["w2"])
    a1c, a2c = lora_pair(p["a1"], p["a2"])
    consts = (_pad_to(p["rwkv_mu"], 0, SUBLANES), p["w0"].reshape(1, 2 * db), p["a0"].reshape(1, 2 * db),
              w1c, w2c, a1c, a2c,
              _pad_to(p["g1"], 1, LORA_PAD).astype(BF16), _pad_to(p["g2"], 0, LORA_PAD).astype(BF16),
              p["k_k"].reshape(1, db), p["k_a"].reshape(1, db), p["r_k"].reshape(1, db), seg, tri, blk)
    full = lambda a: pl.BlockSpec(a.shape, lambda bi, i: (0,) * a.ndim)
    tok = pl.BlockSpec((None, tm, db), lambda bi, i: (bi, i, 0))
    dirtok = pl.BlockSpec((None, 2, tm, db), lambda bi, i: (bi, 0, i, 0))
    tok_shape = jax.ShapeDtypeStruct((b, t, db), F32)
    dir_shape = jax.ShapeDtypeStruct((b, 2, t, db), F32)
    return pl.pallas_call(
        _prep_kernel,
        grid=(b, t // tm),
        in_specs=[pl.BlockSpec((None, tm, 4 * db), lambda bi, i: (bi, i, 0)),
                  pl.BlockSpec((None, SUBLANES, 4 * db), lambda bi, i: (bi, jnp.maximum(i * per - 1, 0), 0)),
                  pl.BlockSpec((None, SUBLANES, 4 * db),
                               lambda bi, i: (bi, jnp.minimum((i + 1) * per, nb - 1), 0))]
        + [full(a) for a in consts],
        out_specs=[tok, tok, tok, dirtok, dirtok, dirtok, dirtok,
                   pl.BlockSpec((None, 2, tm // CHUNK, SUBLANES, db), lambda bi, i: (bi, 0, i, 0, 0))],
        out_shape=[tok_shape, tok_shape, tok_shape, dir_shape, dir_shape, dir_shape, dir_shape,
                   jax.ShapeDtypeStruct((b, 2, nc, SUBLANES, db), F32)],
        compiler_params=pltpu.CompilerParams(dimension_semantics=("parallel", "parallel"),
                                             vmem_limit_bytes=VMEM_LIMIT),
        name="rwkv_prep",
    )(proj, proj, proj, *consts)


SCAN_PARTS = 2


def _scan_kernel(at0, rt0, bt0, kt0, v0, gc0, at1, rt1, bt1, kt1, v1, gc1, s0_ref,
                 hm_ref, ms_ref, mi_ref, idw_ref, bdm_ref, eye_ref,
                 of_ref, or_ref, st_ref):
    i = pl.program_id(1)

    @pl.when(i == 0)
    def _():
        st_ref[...] = s0_ref[...]

    npart = SCAN_PARTS
    bdm = bdm_ref[...]
    eye_b = eye_ref[...].astype(BF16)

    def bd(x):
        return [jnp.concatenate([part * hm_ref[h] for h in range(HEADS_PER_GROUP)], axis=0)
                for part in _split(x, npart)]

    def cat_parts(plists):
        return [jnp.concatenate(ps, axis=-1) for ps in zip(*plists)]

    dirs = ((at0, rt0, bt0, kt0, v0, gc0, of_ref), (at1, rt1, bt1, kt1, v1, gc1, or_ref))
    for d, (at_r, rt_r, bt_r, kt_r, v_r, gc_r, o_r) in enumerate(dirs):
        ms, mi, idw = ms_ref[d], mi_ref[d], idw_ref[...]
        for g in range(at_r.shape[-1] // GROUP):
            sl = slice(g * GROUP, (g + 1) * GROUP)
            at, rt, bt, kt, v = at_r[:, sl], rt_r[:, sl], bt_r[:, sl], kt_r[:, sl], v_r[:, sl]
            gc = gc_r[0:1, sl]

            bk = [jnp.concatenate(ps, axis=0) for ps in zip(bd(bt), bd(kt))]
            gmat = _mmp(_split(jnp.concatenate([at, rt], axis=0), npart), bk, nt=True)
            lab = gmat[0:CHUNK, 0:GROUP] * ms
            lak = gmat[0:CHUNK, GROUP:] * ms
            mrb = gmat[CHUNK:, 0:GROUP] * mi
            mrk = gmat[CHUNK:, GROUP:] * mi

            pw = lab
            tinv = idw + lab
            steps = int(np.log2(CHUNK)) - 1
            for s in range(steps + 1):
                wts = bd(pw)
                if s == 0:
                    pw = _mmp(_split(pw, npart), wts)
                elif s < steps:
                    res = _mmp(_split(jnp.concatenate([pw, tinv], axis=0), npart), wts)
                    pw, tinv = res[0:CHUNK], tinv + res[CHUNK:]
                else:
                    tinv = tinv + _mmp(_split(tinv, npart), wts)

            vbd = bd(v)
            lv_mv = _mmp(_split(jnp.concatenate([lak, mrk], axis=0), npart), vbd)
            lv, mv = lv_mv[0:CHUNK], lv_mv[CHUNK:]
            aw = _mmp(_split(tinv, npart), cat_parts([bd(at), bd(lv)]))
            ahat, wv = aw[:, 0:GROUP], aw[:, GROUP:]
            ro = _mmp(_split(mrb, npart), cat_parts([bd(ahat), bd(wv)]))
            rhat = rt + ro[:, 0:GROUP]
            ointra = ro[:, GROUP:] + mv

            bkh = jnp.concatenate([bt * gc, kt * gc], axis=0)
            bkt = _mmp([eye_b], _split(bkh, npart), nt=True)
            rhs = jnp.concatenate([jnp.concatenate([ahat, wv], axis=-1),
                                   jnp.concatenate([jnp.zeros_like(v), v], axis=-1)], axis=0)
            pq = _mmp(_split(bkt, npart), _split(rhs, npart))
            pm = pq[:, 0:GROUP] * bdm + eye_ref[...] * gc
            qm = pq[:, GROUP:] * bdm

            st = st_ref[d, g]
            rs = _mmp(_split(jnp.concatenate([rhat, pm], axis=0), npart), _split(st, npart))
            o_r[:, sl] = rs[0:CHUNK] + ointra
            st_ref[d, g] = rs[CHUNK:] + qm


def _scan_constants():
    t = np.arange(CHUNK)[:, None]
    lane = np.arange(GROUP)[None, :]
    s = lane % CHUNK
    strict = np.stack([s < t, s > t]).astype(np.float32)
    incl = np.stack([s <= t, s >= t]).astype(np.float32)
    idw = (s == t).astype(np.float32)
    hm = np.stack([np.broadcast_to((lane // RWKV_HEAD) == h, (CHUNK, GROUP)) for h in range(HEADS_PER_GROUP)])
    bdm = np.kron(np.eye(HEADS_PER_GROUP), np.ones((RWKV_HEAD, RWKV_HEAD)))
    return (jnp.asarray(hm.astype(np.float32), BF16), jnp.asarray(strict), jnp.asarray(incl), jnp.asarray(idw),
            jnp.asarray(bdm, F32), jnp.eye(GROUP, dtype=F32))


def _rwkv_scan(at, rt, bt, kt, v, gc, s0):
    b, _, t, db = at.shape
    nc = t // CHUNK
    ng = db // GROUP
    consts = _scan_constants()
    full = lambda a: pl.BlockSpec(a.shape, lambda bi, i: (0,) * a.ndim)
    fwd = lambda bi, i: (bi, 0, i, 0)
    rev = lambda bi, i: (bi, 1, nc - 1 - i, 0)
    dspec = lambda im: pl.BlockSpec((None, None, CHUNK, db), im)
    vspec = lambda im: pl.BlockSpec((None, CHUNK, db), im)
    gspec = lambda im: pl.BlockSpec((None, None, None, SUBLANES, db), im)
    sspec = pl.BlockSpec((None, 2, ng, GROUP, GROUP), lambda bi, i: (bi, 0, 0, 0, 0))
    in_specs = ([dspec(fwd)] * 4 + [vspec(lambda bi, i: (bi, i, 0)), gspec(lambda bi, i: (bi, 0, i, 0, 0))]
                + [dspec(rev)] * 4 + [vspec(lambda bi, i: (bi, nc - 1 - i, 0)),
                                      gspec(lambda bi, i: (bi, 1, nc - 1 - i, 0, 0))]
                + [sspec] + [full(a) for a in consts])
    return pl.pallas_call(
        _scan_kernel,
        grid=(b, nc),
        in_specs=in_specs,
        out_specs=[vspec(lambda bi, i: (bi, i, 0)), vspec(lambda bi, i: (bi, nc - 1 - i, 0)), sspec],
        out_shape=[jax.ShapeDtypeStruct((b, t, db), F32), jax.ShapeDtypeStruct((b, t, db), F32),
                   jax.ShapeDtypeStruct(s0.shape, F32)],
        compiler_params=pltpu.CompilerParams(dimension_semantics=("parallel", "arbitrary"),
                                             vmem_limit_bytes=VMEM_LIMIT),
        name="rwkv_scan",
    )(at, rt, bt, kt, v, gc, at, rt, bt, kt, v, gc, s0, *consts)


def _conv_kernel(p1_ref, p2_ref, w_ref, b_ref, o_ref, h_ref, *, stride):
    t, c = p1_ref.shape
    halo = (CONF_CONV_W // 2) * stride
    nsteps = t // CONV_ROWS
    zeros = jnp.zeros((halo, c), F32)
    h_ref[0:halo, :] = zeros
    h_ref[halo + t:halo + t + halo, :] = zeros

    def glu(j, carry):
        r0 = pl.multiple_of(j * CONV_ROWS, CONV_ROWS)
        h_ref[pl.ds(halo + r0, CONV_ROWS), :] = (p1_ref[pl.ds(r0, CONV_ROWS), :]
                                                  * _sigmoid(p2_ref[pl.ds(r0, CONV_ROWS), :]))
        return carry

    lax.fori_loop(0, nsteps, glu, 0)
    w = w_ref[...]
    bias = b_ref[...]

    def conv(j, carry):
        r0 = pl.multiple_of(j * CONV_ROWS, CONV_ROWS)
        acc = jnp.broadcast_to(bias, (CONV_ROWS, c))
        for tap in range(CONF_CONV_W):
            acc = acc + h_ref[pl.ds(r0 + tap * stride, CONV_ROWS), :] * w[tap:tap + 1, :]
        o_ref[pl.ds(r0, CONV_ROWS), :] = acc
        return carry

    lax.fori_loop(0, nsteps, conv, 0)


def _conformer_conv(proj, dw_w, dw_b, *, stride):
    b, t, n = proj.shape
    db = dw_w.shape[-1]
    lanes = 128
    assert t % CONV_ROWS == 0 and stride % SUBLANES == 0
    halo = (CONF_CONV_W // 2) * stride
    c1, c2 = 5 * db // lanes, 6 * db // lanes
    return pl.pallas_call(
        functools.partial(_conv_kernel, stride=stride),
        grid=(b, db // lanes),
        in_specs=[pl.BlockSpec((None, t, lanes), lambda bi, j: (bi, 0, c1 + j)),
                  pl.BlockSpec((None, t, lanes), lambda bi, j: (bi, 0, c2 + j)),
                  pl.BlockSpec((_round_up(CONF_CONV_W, SUBLANES), lanes), lambda bi, j: (0, j)),
                  pl.BlockSpec((1, lanes), lambda bi, j: (0, j))],
        out_specs=pl.BlockSpec((None, t, lanes), lambda bi, j: (bi, 0, j)),
        out_shape=jax.ShapeDtypeStruct((b, t, db), F32),
        scratch_shapes=[pltpu.VMEM((t + 2 * halo, lanes), F32)],
        compiler_params=pltpu.CompilerParams(dimension_semantics=("parallel", "parallel"),
                                             vmem_limit_bytes=VMEM_LIMIT),
        name="conformer_conv",
    )(proj, proj, _pad_to(dw_w, 0, _round_up(CONF_CONV_W, SUBLANES)), dw_b.reshape(1, db))


def _round_up(x, m):
    return (x + m - 1) // m * m


def _merge_kernel(x_ref, of_ref, or_ref, g_ref, bonus_ref, conv_ref, zc_ref, zd_ref, mod_ref,
                  lnxg_ref, lnxb_ref, clng_ref, clnb_ref, wout_ref, fg_ref, seg_ref, o_ref):
    d = x_ref.shape[-1]
    x = x_ref[...]
    mrow = _mod_row(mod_ref, False)
    o = of_ref[...] + or_ref[...]
    seg = seg_ref[...]
    dev = o - _mm(o, seg, 3, 1)
    var = _mm(dev * dev, seg, 3, 1)
    on = dev * lax.rsqrt(var + LNX_EPS) * lnxg_ref[...] + lnxb_ref[...]
    y_c = (on + bonus_ref[...]) * g_ref[...] * _silu(zc_ref[...])

    c = conv_ref[...]
    mean = jnp.mean(c, axis=-1, keepdims=True)
    cdev = c - mean
    cvar = jnp.mean(cdev * cdev, axis=-1, keepdims=True)
    ln = cdev * lax.rsqrt(cvar + CONF_LN_EPS) * clng_ref[...] + clnb_ref[...]
    y_d = _silu(ln) * _silu(zd_ref[...])

    y = _mm(jnp.concatenate([y_c, y_d], axis=-1), wout_ref[...])
    o_ref[...] = _rmsnorm(x + mrow[:, 2 * d:3 * d] * y, fg_ref[...])


def _merge(x, o_f, o_r, g, bonus, conv, proj, mod, p, final_g):
    b, t, d = x.shape
    db = d // 2
    tm = min(TILE_TOKENS, t)
    heads = db // RWKV_HEAD
    seg = jnp.asarray(np.kron(np.eye(heads), np.full((RWKV_HEAD, RWKV_HEAD), 1.0 / RWKV_HEAD)), BF16)
    consts = (mod, p["lnx_g"].reshape(1, db), p["lnx_b"].reshape(1, db), p["conf_ln_g"].reshape(1, db),
              p["conf_ln_b"].reshape(1, db), p["out_o"].astype(BF16), final_g.reshape(1, d), seg)
    full = lambda a: pl.BlockSpec(a.shape, lambda bi, i: (0,) * a.ndim)
    tok = pl.BlockSpec((None, tm, db), lambda bi, i: (bi, i, 0))
    col = lambda cblk: pl.BlockSpec((None, tm, db), lambda bi, i: (bi, i, cblk))
    return pl.pallas_call(
        _merge_kernel,
        grid=(b, t // tm),
        in_specs=[pl.BlockSpec((None, tm, d), lambda bi, i: (bi, i, 0)), tok, tok, tok, tok, tok,
                  col(4), col(7)] + [full(a) for a in consts],
        out_specs=pl.BlockSpec((None, tm, d), lambda bi, i: (bi, i, 0)),
        out_shape=jax.ShapeDtypeStruct((b, t, d), F32),
        compiler_params=pltpu.CompilerParams(dimension_semantics=("parallel", "parallel"),
                                             vmem_limit_bytes=VMEM_LIMIT),
        name="merge",
    )(x, o_f, o_r, g, bonus, conv, proj, proj, *consts)


def kernel(x, c, ctx, c_ctx, ada_w_e, ada_b_e, norm_e, in_e, out_e, pool_w, pool_scale, sconv_w,
           ada_w_o, ada_b_o, norm_o, in_o, out_o, rwkv_mu, w0, w1, w2, a0, a1, a2, g1, g2,
           k_k, k_a, r_k, lnx_g, lnx_b, conf_dw_w, conf_dw_b, conf_ln_g, conf_ln_b, final_g):
    b, t, d = x.shape
    db = d // 2
    assert b == 2
    cvec = _pad_to(jnp.concatenate([c, c_ctx[None, :]], axis=0), 0, SUBLANES)
    mod_e = _adaln(cvec, ada_w_e[0], ada_b_e[0])
    mod_o = _adaln(cvec, ada_w_o[0], ada_b_o[0])

    even = dict(norm_g=norm_e[0], w_in=in_e[0], w_out=out_e[0], pool_w=pool_w[0],
                pool_scale=pool_scale[0], sconv_w=sconv_w[0])
    x1 = _even_layer(x, mod_e, ctx=False, line=GRID_W, **even)
    xc1 = _even_layer(ctx, mod_e, ctx=True, line=ctx.shape[1], **even)

    p = dict(rwkv_mu=rwkv_mu[0], w0=w0[0], w1=w1[0], w2=w2[0], a0=a0[0], a1=a1[0], a2=a2[0],
             g1=g1[0], g2=g2[0], k_k=k_k[0], k_a=k_a[0], r_k=r_k[0], lnx_g=lnx_g[0], lnx_b=lnx_b[0],
             conf_ln_g=conf_ln_g[0], conf_ln_b=conf_ln_b[0], out_o=out_o[0])

    proj_c = _inproj(xc1, mod_o, norm_o[0], in_o[0][:, 0:4 * db], ctx=True)
    v_c, _, _, at_c, rt_c, bt_c, kt_c, gc_c = _rwkv_prep(proj_c, p)
    zero_state = jnp.zeros((b, 2, db // GROUP, GROUP, GROUP), F32)
    _, _, ctx_state = _rwkv_scan(at_c, rt_c, bt_c, kt_c, v_c, gc_c, zero_state)

    proj = _inproj(x1, mod_o, norm_o[0], in_o[0], ctx=False)
    v_l, g_l, bonus, at_l, rt_l, bt_l, kt_l, gc_l = _rwkv_prep(proj, p)
    o_f, o_r, _ = _rwkv_scan(at_l, rt_l, bt_l, kt_l, v_l, gc_l, ctx_state)
    conv = _conformer_conv(proj, conf_dw_w[0], conf_dw_b[0], stride=GRID_W)
    return _merge(x1, o_f, o_r, g_l, bonus, conv, proj, mod_o, p, final_g)
```

```python
import functools

import numpy as np
import jax
import jax.numpy as jnp
from jax import lax
from jax.experimental import pallas as pl
from jax.experimental.pallas import tpu as pltpu

F32 = jnp.float32
BF16 = jnp.bfloat16

GRID_W = 64
POOL_WINDOWS = (2, 4, 8, 16)
SHORT_CONV_W = 3
CONF_CONV_W = 31
RWKV_HEAD = 64
NORM_EPS = 1e-6
LNX_EPS = 64e-5
CONF_LN_EPS = 1e-5

CHUNK = 64
HEADS_PER_GROUP = 4
GROUP = HEADS_PER_GROUP * RWKV_HEAD
LORA_PAD = 128
SUBLANES = 8
VMEM_LIMIT = 52 * 1024 * 1024

TILE_TOKENS = 256
PREP_TOKENS = 128
CONV_ROWS = 128


def _sigmoid(x):
    return 1.0 / (1.0 + jnp.exp(-x))


def _silu(x):
    return x * _sigmoid(x)


def _split(x, n):
    if x.dtype == BF16:
        return [x]
    parts, r = [], x
    for i in range(n):
        p = r.astype(BF16)
        parts.append(p)
        if i + 1 < n:
            r = r - p.astype(F32)
    return parts


def _dot(a, b, nt):
    dims = (((1,), (1,)), ((), ())) if nt else (((1,), (0,)), ((), ()))
    return lax.dot_general(a, b, dims, preferred_element_type=F32)


def _mmp(ap, bp, nt=False):
    order = max(len(ap), len(bp))
    acc = None
    for i in reversed(range(len(ap))):
        for j in reversed(range(len(bp))):
            if i + j < order:
                t = _dot(ap[i], bp[j], nt)
                acc = t if acc is None else acc + t
    return acc


def _mm(a, b, na=1, nb=1, nt=False):
    return _mmp(_split(a, na), _split(b, nb), nt)


def _rmsnorm(x, g):
    ms = jnp.mean(x * x, axis=-1, keepdims=True)
    return x * lax.rsqrt(ms + NORM_EPS) * g


def _modulated(x, g, mod_row, d):
    shift, scale = mod_row[:, 0:d], mod_row[:, d:2 * d]
    return _rmsnorm(x, g) * (1.0 + scale) + shift


def _mod_row(mod_ref, ctx):
    row = 2 if ctx else pl.program_id(0)
    return mod_ref[pl.ds(row, 1), :]


def _adaln_kernel(c_ref, w_ref, b_ref, o_ref):
    o_ref[...] = _mm(_silu(c_ref[...]), w_ref[...], 2, 2) + b_ref[...]


def _adaln(cvec, w, b):
    d, n = w.shape
    tn = 768
    return pl.pallas_call(
        _adaln_kernel,
        grid=(n // tn,),
        in_specs=[pl.BlockSpec((SUBLANES, d), lambda j: (0, 0)),
                  pl.BlockSpec((d, tn), lambda j: (0, j)),
                  pl.BlockSpec((1, tn), lambda j: (0, j))],
        out_specs=pl.BlockSpec((SUBLANES, tn), lambda j: (0, j)),
        out_shape=jax.ShapeDtypeStruct((SUBLANES, n), F32),
        compiler_params=pltpu.CompilerParams(dimension_semantics=("parallel",),
                                             vmem_limit_bytes=VMEM_LIMIT),
        name="adaln",
    )(cvec, w, b.reshape(1, n))


def _even_kernel(x_ref, mod_ref, g_ref, win_ref, wout_ref, poolw_ref, pscale_ref, sconv_ref,
                 band_ref, invc_ref, o_ref, *, ctx, line):
    tm, d = x_ref.shape
    db = d // 2
    grp = db // len(POOL_WINDOWS)
    x = x_ref[...]
    mrow = _mod_row(mod_ref, ctx)
    h = _modulated(x, g_ref[...], mrow, d)
    proj = _mm(h, win_ref[...])
    u_a, z_a = proj[:, 0:db], proj[:, db:2 * db]
    v_b, g_b = proj[:, 2 * db:3 * db], proj[:, 3 * db:4 * db]
    g_c, z_b = proj[:, 4 * db:5 * db], proj[:, 5 * db:6 * db]

    ya = []
    for i in range(len(POOL_WINDOWS)):
        ug = u_a[:, i * grp:(i + 1) * grp]
        sums = _mm(band_ref[i], ug, 1, 2)
        pooled = sums * invc_ref[i] - ug
        ya.append(_mm(pooled, poolw_ref[i]))
    y_a = jnp.concatenate(ya, axis=-1) * pscale_ref[...] * _silu(z_a)

    q = g_c * v_b
    pos = lax.broadcasted_iota(jnp.int32, q.shape, 0) % line
    q_prev = jnp.where(pos >= 1, pltpu.roll(q, 1, 0), 0.0)
    q_next = jnp.where(pos <= line - 2, pltpu.roll(q, tm - 1, 0), 0.0)
    w = sconv_ref[...]
    conv = q_prev * w[0:1, :] + q * w[1:2, :] + q_next * w[2:3, :]
    y_b = g_b * conv * _silu(z_b)

    y = _mm(jnp.concatenate([y_a, y_b], axis=-1), wout_ref[...])
    o_ref[...] = x + mrow[:, 2 * d:3 * d] * y


def _pool_constants(tm, line):
    pos = np.arange(tm) % line
    lid = np.arange(tm) // line
    bands, invs = [], []
    for w in POOL_WINDOWS:
        lo = np.clip(pos - w // 2, 0, line)
        hi = np.clip(pos - w // 2 + w, 0, line)
        same = lid[:, None] == lid[None, :]
        band = same & (pos[None, :] >= lo[:, None]) & (pos[None, :] < hi[:, None])
        bands.append(band.astype(np.float32))
        invs.append(np.broadcast_to((1.0 / (hi - lo))[:, None], (tm, 128)).astype(np.float32))
    return jnp.asarray(np.stack(bands), BF16), jnp.asarray(np.stack(invs), F32)


def _even_layer(x, mod, norm_g, w_in, w_out, pool_w, pool_scale, sconv_w, *, ctx, line):
    b, t, d = x.shape
    tm = min(TILE_TOKENS, t)
    assert t % tm == 0 and tm % line == 0
    db = d // 2
    band, invc = _pool_constants(tm, line)
    full = lambda a: pl.BlockSpec(a.shape, lambda bi, i: (0,) * a.ndim)
    args = (mod, norm_g.reshape(1, d), w_in.astype(BF16), w_out.astype(BF16), pool_w.astype(BF16),
            pool_scale.reshape(1, db), sconv_w, band, invc)
    return pl.pallas_call(
        functools.partial(_even_kernel, ctx=ctx, line=line),
        grid=(b, t // tm),
        in_specs=[pl.BlockSpec((None, tm, d), lambda bi, i: (bi, i, 0))] + [full(a) for a in args],
        out_specs=pl.BlockSpec((None, tm, d), lambda bi, i: (bi, i, 0)),
        out_shape=jax.ShapeDtypeStruct((b, t, d), F32),
        compiler_params=pltpu.CompilerParams(dimension_semantics=("parallel", "parallel"),
                                             vmem_limit_bytes=VMEM_LIMIT),
        name="even_ctx" if ctx else "even_latent",
    )(x, *args)


def _inproj_kernel(x_ref, mod_ref, g_ref, w_ref, o_ref, *, ctx):
    d = x_ref.shape[-1]
    h = _modulated(x_ref[...], g_ref[...], _mod_row(mod_ref, ctx), d)
    o_ref[...] = _mm(h, w_ref[...])


def _inproj(x, mod, norm_g, w_in, *, ctx):
    b, t, d = x.shape
    n = w_in.shape[1]
    tm = min(TILE_TOKENS, t)
    assert t % tm == 0
    return pl.pallas_call(
        functools.partial(_inproj_kernel, ctx=ctx),
        grid=(b, t // tm),
        in_specs=[pl.BlockSpec((None, tm, d), lambda bi, i: (bi, i, 0)),
                  pl.BlockSpec(mod.shape, lambda bi, i: (0, 0)),
                  pl.BlockSpec((1, d), lambda bi, i: (0, 0)),
                  pl.BlockSpec((d, n), lambda bi, i: (0, 0))],
        out_specs=pl.BlockSpec((None, tm, n), lambda bi, i: (bi, i, 0)),
        out_shape=jax.ShapeDtypeStruct((b, t, n), F32),
        compiler_params=pltpu.CompilerParams(dimension_semantics=("parallel", "parallel"),
                                             vmem_limit_bytes=VMEM_LIMIT),
        name="inproj_ctx" if ctx else "inproj_latent",
    )(x, mod, norm_g.reshape(1, d), w_in.astype(BF16))


def _prep_kernel(pm_ref, pp_ref, pn_ref, mu_ref, w0_ref, a0_ref, w1_ref, w2_ref, a1_ref, a2_ref,
                 g1_ref, g2_ref, kk_ref, ka_ref, rk_ref, seg_ref, tri_ref, blk_ref,
                 v_ref, g_ref, bonus_ref, at_ref, rt_ref, bt_ref, kt_ref, gc_ref):
    tm = pm_ref.shape[0]
    db = v_ref.shape[-1]
    i, n = pl.program_id(1), pl.num_programs(1)
    z = pm_ref[...]
    rows = lax.broadcasted_iota(jnp.int32, z.shape, 0)
    prev_row = pp_ref[SUBLANES - 1:SUBLANES, :] * jnp.where(i > 0, 1.0, 0.0)
    next_row = pn_ref[0:1, :] * jnp.where(i < n - 1, 1.0, 0.0)
    z_prev = jnp.where(rows == 0, prev_row, pltpu.roll(z, 1, 0))
    z_next = jnp.where(rows == tm - 1, next_row, pltpu.roll(z, tm - 1, 0))
    dz = 0.5 * (z_prev + z_next) - z
    mu = mu_ref[...]
    u, du = z[:, 0:db], dz[:, 0:db]
    r = z[:, db:2 * db] + dz[:, db:2 * db] * mu[0:1, :]
    k = z[:, 2 * db:3 * db] + dz[:, 2 * db:3 * db] * mu[1:2, :]
    v = z[:, 3 * db:4 * db] + dz[:, 3 * db:4 * db] * mu[2:3, :]
    uw = u + du * mu[3:4, :]
    ua = u + du * mu[4:5, :]
    ug = u + du * mu[5:6, :]

    g_ref[...] = _mm(_sigmoid(_mm(ug, g1_ref[...])), g2_ref[...])
    v_ref[...] = v
    lw_pre = w0_ref[...] + _mm(jnp.tanh(_mm(uw, w1_ref[...])), w2_ref[...])
    a_pre = a0_ref[...] + _mm(_mm(ua, a1_ref[...]), a2_ref[...])

    kk = k * kk_ref[...]
    norm = jnp.sqrt(_mm(kk * kk, seg_ref[...], 3, 1))
    kk = kk / jnp.maximum(norm, 1e-12)

    ksum = None
    for d in range(2):
        x = -lw_pre[:, d * db:(d + 1) * db]
        softplus = jnp.maximum(x, 0.0) + jnp.log(1.0 + jnp.exp(-jnp.abs(x)))
        lw = -jnp.exp(-softplus - 0.5)
        a = _sigmoid(a_pre[:, d * db:(d + 1) * db])
        k_d = k * (1.0 + (a - 1.0) * ka_ref[...])
        ksum = k_d if ksum is None else ksum + k_d
        lw_parts = _split(lw, 3)
        cl = _mmp([tri_ref[d]], lw_parts)
        tot = _mmp([blk_ref[...]], lw_parts)
        at_ref[d] = -kk * jnp.exp(cl - lw)
        rt_ref[d] = r * jnp.exp(cl)
        ginv = jnp.exp(-cl)
        bt_ref[d] = kk * a * ginv
        kt_ref[d] = k_d * ginv
        gam = jnp.exp(tot)
        for c in range(tm // CHUNK):
            gc_ref[d, c] = gam[c * CHUNK:c * CHUNK + SUBLANES, :]
    bonus_ref[...] = _mm(r * ksum * rk_ref[...], seg_ref[...], 3, 1) * v


def _prep_constants(tm):
    t = np.arange(tm)
    same = (t[:, None] // CHUNK) == (t[None, :] // CHUNK)
    lower = same & (t[None, :] <= t[:, None])
    upper = same & (t[None, :] >= t[:, None])
    tri = np.stack([lower, upper]).astype(np.float32)
    return jnp.asarray(tri, BF16), jnp.asarray(same.astype(np.float32), BF16)


def _pad_to(a, axis, size):
    pad = [(0, 0)] * a.ndim
    pad[axis] = (0, size - a.shape[axis])
    return jnp.pad(a, pad)


def _rwkv_prep(proj, p):
    b, t, _ = proj.shape
    db = p["k_k"].shape[-1]
    tm = min(PREP_TOKENS, t)
    assert t % tm == 0 and tm % CHUNK == 0
    nc = t // CHUNK
    nb = t // SUBLANES
    per = tm // SUBLANES
    tri, blk = _prep_constants(tm)
    heads = db // RWKV_HEAD
    seg = jnp.asarray(np.kron(np.eye(heads), np.ones((RWKV_HEAD, RWKV_HEAD))), BF16)

    def lora_pair(w_a, w_b):
        rank = w_a.shape[-1]
        down = _pad_to(jnp.concatenate([w_a[0], w_a[1]], axis=-1), 1, LORA_PAD)
        up = jnp.zeros((LORA_PAD, 2 * db), F32)
        up = up.at[0:rank, 0:db].set(w_b[0]).at[rank:2 * rank, db:2 * db].set(w_b[1])
        return down.astype(BF16), up.astype(BF16)

    w1c, w2c = lora_pair(p["w1"], p["w2"])
    a1c, a2c = lora_pair(p["a1"], p["a2"])
    consts = (_pad_to(p["rwkv_mu"], 0, SUBLANES), p["w0"].reshape(1, 2 * db), p["a0"].reshape(1, 2 * db),
              w1c, w2c, a1c, a2c,
              _pad_to(p["g1"], 1, LORA_PAD).astype(BF16), _pad_to(p["g2"], 0, LORA_PAD).astype(BF16),
              p["k_k"].reshape(1, db), p["k_a"].reshape(1, db), p["r_k"].reshape(1, db), seg, tri, blk)
    full = lambda a: pl.BlockSpec(a.shape, lambda bi, i: (0,) * a.ndim)
    tok = pl.BlockSpec((None, tm, db), lambda bi, i: (bi, i, 0))
    dirtok = pl.BlockSpec((None, 2, tm, db), lambda bi, i: (bi, 0, i, 0))
    tok_shape = jax.ShapeDtypeStruct((b, t, db), F32)
    dir_shape = jax.ShapeDtypeStruct((b, 2, t, db), F32)
    return pl.pallas_call(
        _prep_kernel,
        grid=(b, t // tm),
        in_specs=[pl.BlockSpec((None, tm, 4 * db), lambda bi, i: (bi, i, 0)),
                  pl.BlockSpec((None, SUBLANES, 4 * db), lambda bi, i: (bi, jnp.maximum(i * per - 1, 0), 0)),
                  pl.BlockSpec((None, SUBLANES, 4 * db),
                               lambda bi, i: (bi, jnp.minimum((i + 1) * per, nb - 1), 0))]
        + [full(a) for a in consts],
        out_specs=[tok, tok, tok, dirtok, dirtok, dirtok, dirtok,
                   pl.BlockSpec((None, 2, tm // CHUNK, SUBLANES, db), lambda bi, i: (bi, 0, i, 0, 0))],
        out_shape=[tok_shape, tok_shape, tok_shape, dir_shape, dir_shape, dir_shape, dir_shape,
                   jax.ShapeDtypeStruct((b, 2, nc, SUBLANES, db), F32)],
        compiler_params=pltpu.CompilerParams(dimension_semantics=("parallel", "parallel"),
                                             vmem_limit_bytes=VMEM_LIMIT),
        name="rwkv_prep",
    )(proj, proj, proj, *consts)


SCAN_PARTS = 1
STATE_PARTS = 1


def _scan_kernel(at0, rt0, bt0, kt0, v0, gc0, at1, rt1, bt1, kt1, v1, gc1, s0_ref,
                 hm_ref, ms_ref, mi_ref, idw_ref, bdm_ref, eye_ref,
                 of_ref, or_ref, st_ref):
    i = pl.program_id(1)

    @pl.when(i == 0)
    def _():
        st_ref[...] = s0_ref[...]

    npart = SCAN_PARTS
    bdm = bdm_ref[...]
    eye_b = eye_ref[...].astype(BF16)

    def bd(x):
        return [jnp.concatenate([part * hm_ref[h] for h in range(HEADS_PER_GROUP)], axis=0)
                for part in _split(x, npart)]

    def cat_parts(plists):
        return [jnp.concatenate(ps, axis=-1) for ps in zip(*plists)]

    steps = int(np.log2(CHUNK)) - 1

    def load(c):
        at_r, rt_r, bt_r, kt_r, v_r, gc_r = c["refs"]
        sl = c["sl"]
        c.update(at=at_r[:, sl], rt=rt_r[:, sl], bt=bt_r[:, sl], kt=kt_r[:, sl], v=v_r[:, sl],
                 gc=gc_r[0:1, sl], st=st_ref[c["d"], c["g"]])

    def interactions(c):
        ms, mi = ms_ref[c["d"]], mi_ref[c["d"]]
        bk = [jnp.concatenate(ps, axis=0) for ps in zip(bd(c["bt"]), bd(c["kt"]))]
        gmat = _mmp(_split(jnp.concatenate([c["at"], c["rt"]], axis=0), npart), bk, nt=True)
        lab = gmat[0:CHUNK, 0:GROUP] * ms
        c.update(lak=gmat[0:CHUNK, GROUP:] * ms, mrb=gmat[CHUNK:, 0:GROUP] * mi,
                 mrk=gmat[CHUNK:, GROUP:] * mi, pw=lab, tinv=idw_ref[...] + lab)

    def invert(s):
        def stage(c):
            pw, tinv = c["pw"], c["tinv"]
            wts = bd(pw)
            if s == 0:
                pw = _mmp(_split(pw, npart), wts)
            elif s < steps:
                res = _mmp(_split(jnp.concatenate([pw, tinv], axis=0), npart), wts)
                pw, tinv = res[0:CHUNK], tinv + res[CHUNK:]
            else:
                tinv = tinv + _mmp(_split(tinv, npart), wts)
            c.update(pw=pw, tinv=tinv)
        return stage

    def values(c):
        lv_mv = _mmp(_split(jnp.concatenate([c["lak"], c["mrk"]], axis=0), npart), bd(c["v"]))
        c.update(lv=lv_mv[0:CHUNK], mv=lv_mv[CHUNK:])

    def solve(c):
        aw = _mmp(_split(c["tinv"], npart), cat_parts([bd(c["at"]), bd(c["lv"])]))
        c.update(ahat=aw[:, 0:GROUP], wv=aw[:, GROUP:])

    def outputs(c):
        ro = _mmp(_split(c["mrb"], npart), cat_parts([bd(c["ahat"]), bd(c["wv"])]))
        c.update(rhat=c["rt"] + ro[:, 0:GROUP], ointra=ro[:, GROUP:] + c["mv"])

    def transposed(c):
        bkh = jnp.concatenate([c["bt"] * c["gc"], c["kt"] * c["gc"]], axis=0)
        c.update(bkt=_mmp([eye_b], _split(bkh, npart), nt=True))

    def transition(c):
        v = c["v"]
        rhs = jnp.concatenate([jnp.concatenate([c["ahat"], c["wv"]], axis=-1),
                               jnp.concatenate([jnp.zeros_like(v), v], axis=-1)], axis=0)
        pq = _mmp(_split(c["bkt"], npart), _split(rhs, npart))
        c.update(pm=pq[:, 0:GROUP] * bdm + eye_ref[...] * c["gc"], qm=pq[:, GROUP:] * bdm)

    def advance(c):
        rs = _mmp(_split(jnp.concatenate([c["rhat"], c["pm"]], axis=0), STATE_PARTS),
                  _split(c["st"], STATE_PARTS))
        c.update(o=rs[0:CHUNK] + c["ointra"], st_new=rs[CHUNK:] + c["qm"])

    def store(c):
        c["out"][:, c["sl"]] = c["o"]
        st_ref[c["d"], c["g"]] = c["st_new"]

    chains = []
    dirs = ((at0, rt0, bt0, kt0, v0, gc0, of_ref), (at1, rt1, bt1, kt1, v1, gc1, or_ref))
    for d, refs in enumerate(dirs):
        for g in range(at0.shape[-1] // GROUP):
            chains.append(dict(d=d, g=g, sl=slice(g * GROUP, (g + 1) * GROUP), refs=refs[:6], out=refs[6]))
    stages = ([load, interactions] + [invert(s) for s in range(steps + 1)]
              + [values, solve, outputs, transposed, transition, advance, store])
    for stage in stages:
        for c in chains:
            stage(c)


def _scan_constants():
    t = np.arange(CHUNK)[:, None]
    lane = np.arange(GROUP)[None, :]
    s = lane % CHUNK
    strict = np.stack([s < t, s > t]).astype(np.float32)
    incl = np.stack([s <= t, s >= t]).astype(np.float32)
    idw = (s == t).astype(np.float32)
    hm = np.stack([np.broadcast_to((lane // RWKV_HEAD) == h, (CHUNK, GROUP)) for h in range(HEADS_PER_GROUP)])
    bdm = np.kron(np.eye(HEADS_PER_GROUP), np.ones((RWKV_HEAD, RWKV_HEAD)))
    return (jnp.asarray(hm.astype(np.float32), BF16), jnp.asarray(strict), jnp.asarray(incl), jnp.asarray(idw),
            jnp.asarray(bdm, F32), jnp.eye(GROUP, dtype=F32))


def _rwkv_scan(at, rt, bt, kt, v, gc, s0):
    b, _, t, db = at.shape
    nc = t // CHUNK
    ng = db // GROUP
    consts = _scan_constants()
    full = lambda a: pl.BlockSpec(a.shape, lambda bi, i: (0,) * a.ndim)
    fwd = lambda bi, i: (bi, 0, i, 0)
    rev = lambda bi, i: (bi, 1, nc - 1 - i, 0)
    dspec = lambda im: pl.BlockSpec((None, None, CHUNK, db), im)
    vspec = lambda im: pl.BlockSpec((None, CHUNK, db), im)
    gspec = lambda im: pl.BlockSpec((None, None, None, SUBLANES, db), im)
    sspec = pl.BlockSpec((None, 2, ng, GROUP, GROUP), lambda bi, i: (bi, 0, 0, 0, 0))
    in_specs = ([dspec(fwd)] * 4 + [vspec(lambda bi, i: (bi, i, 0)), gspec(lambda bi, i: (bi, 0, i, 0, 0))]
                + [dspec(rev)] * 4 + [vspec(lambda bi, i: (bi, nc - 1 - i, 0)),
                                      gspec(lambda bi, i: (bi, 1, nc - 1 - i, 0, 0))]
                + [sspec] + [full(a) for a in consts])
    return pl.pallas_call(
        _scan_kernel,
        grid=(b, nc),
        in_specs=in_specs,
        out_specs=[vspec(lambda bi, i: (bi, i, 0)), vspec(lambda bi, i: (bi, nc - 1 - i, 0)), sspec],
        out_shape=[jax.ShapeDtypeStruct((b, t, db), F32), jax.ShapeDtypeStruct((b, t, db), F32),
                   jax.ShapeDtypeStruct(s0.shape, F32)],
        compiler_params=pltpu.CompilerParams(dimension_semantics=("parallel", "arbitrary"),
                                             vmem_limit_bytes=VMEM_LIMIT),
        name="rwkv_scan",
    )(at, rt, bt, kt, v, gc, at, rt, bt, kt, v, gc, s0, *consts)


def _conv_kernel(p1_ref, p2_ref, w_ref, b_ref, o_ref, h_ref, *, stride):
    t, c = p1_ref.shape
    halo = (CONF_CONV_W // 2) * stride
    nsteps = t // CONV_ROWS
    zeros = jnp.zeros((halo, c), F32)
    h_ref[0:halo, :] = zeros
    h_ref[halo + t:halo + t + halo, :] = zeros

    def glu(j, carry):
        r0 = pl.multiple_of(j * CONV_ROWS, CONV_ROWS)
        h_ref[pl.ds(halo + r0, CONV_ROWS), :] = (p1_ref[pl.ds(r0, CONV_ROWS), :]
                                                  * _sigmoid(p2_ref[pl.ds(r0, CONV_ROWS), :]))
        return carry

    lax.fori_loop(0, nsteps, glu, 0)
    w = w_ref[...]
    bias = b_ref[...]

    def conv(j, carry):
        r0 = pl.multiple_of(j * CONV_ROWS, CONV_ROWS)
        acc = jnp.broadcast_to(bias, (CONV_ROWS, c))
        for tap in range(CONF_CONV_W):
            acc = acc + h_ref[pl.ds(r0 + tap * stride, CONV_ROWS), :] * w[tap:tap + 1, :]
        o_ref[pl.ds(r0, CONV_ROWS), :] = acc
        return carry

    lax.fori_loop(0, nsteps, conv, 0)


def _conformer_conv(proj, dw_w, dw_b, *, stride):
    b, t, n = proj.shape
    db = dw_w.shape[-1]
    lanes = 128
    assert t % CONV_ROWS == 0 and stride % SUBLANES == 0
    halo = (CONF_CONV_W // 2) * stride
    c1, c2 = 5 * db // lanes, 6 * db // lanes
    return pl.pallas_call(
        functools.partial(_conv_kernel, stride=stride),
        grid=(b, db // lanes),
        in_specs=[pl.BlockSpec((None, t, lanes), lambda bi, j: (bi, 0, c1 + j)),
                  pl.BlockSpec((None, t, lanes), lambda bi, j: (bi, 0, c2 + j)),
                  pl.BlockSpec((_round_up(CONF_CONV_W, SUBLANES), lanes), lambda bi, j: (0, j)),
                  pl.BlockSpec((1, lanes), lambda bi, j: (0, j))],
        out_specs=pl.BlockSpec((None, t, lanes), lambda bi, j: (bi, 0, j)),
        out_shape=jax.ShapeDtypeStruct((b, t, db), F32),
        scratch_shapes=[pltpu.VMEM((t + 2 * halo, lanes), F32)],
        compiler_params=pltpu.CompilerParams(dimension_semantics=("parallel", "parallel"),
                                             vmem_limit_bytes=VMEM_LIMIT),
        name="conformer_conv",
    )(proj, proj, _pad_to(dw_w, 0, _round_up(CONF_CONV_W, SUBLANES)), dw_b.reshape(1, db))


def _round_up(x, m):
    return (x + m - 1) // m * m


def _merge_kernel(x_ref, of_ref, or_ref, g_ref, bonus_ref, conv_ref, zc_ref, zd_ref, mod_ref,
                  lnxg_ref, lnxb_ref, clng_ref, clnb_ref, wout_ref, fg_ref, seg_ref, o_ref):
    d = x_ref.shape[-1]
    x = x_ref[...]
    mrow = _mod_row(mod_ref, False)
    o = of_ref[...] + or_ref[...]
    seg = seg_ref[...]
    dev = o - _mm(o, seg, 3, 1)
    var = _mm(dev * dev, seg, 3, 1)
    on = dev * lax.rsqrt(var + LNX_EPS) * lnxg_ref[...] + lnxb_ref[...]
    y_c = (on + bonus_ref[...]) * g_ref[...] * _silu(zc_ref[...])

    c = conv_ref[...]
    mean = jnp.mean(c, axis=-1, keepdims=True)
    cdev = c - mean
    cvar = jnp.mean(cdev * cdev, axis=-1, keepdims=True)
    ln = cdev * lax.rsqrt(cvar + CONF_LN_EPS) * clng_ref[...] + clnb_ref[...]
    y_d = _silu(ln) * _silu(zd_ref[...])

    y = _mm(jnp.concatenate([y_c, y_d], axis=-1), wout_ref[...])
    o_ref[...] = _rmsnorm(x + mrow[:, 2 * d:3 * d] * y, fg_ref[...])


def _merge(x, o_f, o_r, g, bonus, conv, proj, mod, p, final_g):
    b, t, d = x.shape
    db = d // 2
    tm = min(TILE_TOKENS, t)
    heads = db // RWKV_HEAD
    seg = jnp.asarray(np.kron(np.eye(heads), np.full((RWKV_HEAD, RWKV_HEAD), 1.0 / RWKV_HEAD)), BF16)
    consts = (mod, p["lnx_g"].reshape(1, db), p["lnx_b"].reshape(1, db), p["conf_ln_g"].reshape(1, db),
              p["conf_ln_b"].reshape(1, db), p["out_o"].astype(BF16), final_g.reshape(1, d), seg)
    full = lambda a: pl.BlockSpec(a.shape, lambda bi, i: (0,) * a.ndim)
    tok = pl.BlockSpec((None, tm, db), lambda bi, i: (bi, i, 0))
    col = lambda cblk: pl.BlockSpec((None, tm, db), lambda bi, i: (bi, i, cblk))
    return pl.pallas_call(
        _merge_kernel,
        grid=(b, t // tm),
        in_specs=[pl.BlockSpec((None, tm, d), lambda bi, i: (bi, i, 0)), tok, tok, tok, tok, tok,
                  col(4), col(7)] + [full(a) for a in consts],
        out_specs=pl.BlockSpec((None, tm, d), lambda bi, i: (bi, i, 0)),
        out_shape=jax.ShapeDtypeStruct((b, t, d), F32),
        compiler_params=pltpu.CompilerParams(dimension_semantics=("parallel", "parallel"),
                                             vmem_limit_bytes=VMEM_LIMIT),
        name="merge",
    )(x, o_f, o_r, g, bonus, conv, proj, proj, *consts)


def kernel(x, c, ctx, c_ctx, ada_w_e, ada_b_e, norm_e, in_e, out_e, pool_w, pool_scale, sconv_w,
           ada_w_o, ada_b_o, norm_o, in_o, out_o, rwkv_mu, w0, w1, w2, a0, a1, a2, g1, g2,
           k_k, k_a, r_k, lnx_g, lnx_b, conf_dw_w, conf_dw_b, conf_ln_g, conf_ln_b, final_g):
    b, t, d = x.shape
    db = d // 2
    assert b == 2
    cvec = _pad_to(jnp.concatenate([c, c_ctx[None, :]], axis=0), 0, SUBLANES)
    mod_e = _adaln(cvec, ada_w_e[0], ada_b_e[0])
    mod_o = _adaln(cvec, ada_w_o[0], ada_b_o[0])

    even = dict(norm_g=norm_e[0], w_in=in_e[0], w_out=out_e[0], pool_w=pool_w[0],
                pool_scale=pool_scale[0], sconv_w=sconv_w[0])
    x1 = _even_layer(x, mod_e, ctx=False, line=GRID_W, **even)
    xc1 = _even_layer(ctx, mod_e, ctx=True, line=ctx.shape[1], **even)

    p = dict(rwkv_mu=rwkv_mu[0], w0=w0[0], w1=w1[0], w2=w2[0], a0=a0[0], a1=a1[0], a2=a2[0],
             g1=g1[0], g2=g2[0], k_k=k_k[0], k_a=k_a[0], r_k=r_k[0], lnx_g=lnx_g[0], lnx_b=lnx_b[0],
             conf_ln_g=conf_ln_g[0], conf_ln_b=conf_ln_b[0], out_o=out_o[0])

    proj_c = _inproj(xc1, mod_o, norm_o[0], in_o[0][:, 0:4 * db], ctx=True)
    v_c, _, _, at_c, rt_c, bt_c, kt_c, gc_c = _rwkv_prep(proj_c, p)
    zero_state = jnp.zeros((b, 2, db // GROUP, GROUP, GROUP), F32)
    _, _, ctx_state = _rwkv_scan(at_c, rt_c, bt_c, kt_c, v_c, gc_c, zero_state)

    proj = _inproj(x1, mod_o, norm_o[0], in_o[0], ctx=False)
    v_l, g_l, bonus, at_l, rt_l, bt_l, kt_l, gc_l = _rwkv_prep(proj, p)
    o_f, o_r, _ = _rwkv_scan(at_l, rt_l, bt_l, kt_l, v_l, gc_l, ctx_state)
    conv = _conformer_conv(proj, conf_dw_w[0], conf_dw_b[0], stride=GRID_W)
    return _merge(x1, o_f, o_r, g_l, bonus, conv, proj, mod_o, p, final_g)
```

```python
import functools

import numpy as np
import jax
import jax.numpy as jnp
from jax import lax
from jax.experimental import pallas as pl
from jax.experimental.pallas import tpu as pltpu

F32 = jnp.float32
BF16 = jnp.bfloat16

GRID_W = 64
POOL_WINDOWS = (2, 4, 8, 16)
SHORT_CONV_W = 3
CONF_CONV_W = 31
RWKV_HEAD = 64
NORM_EPS = 1e-6
LNX_EPS = 64e-5
CONF_LN_EPS = 1e-5

CHUNK = 64
HEADS_PER_GROUP = 4
GROUP = HEADS_PER_GROUP * RWKV_HEAD
LORA_PAD = 128
SUBLANES = 8
VMEM_LIMIT = 52 * 1024 * 1024

TILE_TOKENS = 256
PREP_TOKENS = 128
CONV_ROWS = 128


def _sigmoid(x):
    return 1.0 / (1.0 + jnp.exp(-x))


def _silu(x):
    return x * _sigmoid(x)


def _split(x, n):
    if x.dtype == BF16:
        return [x]
    parts, r = [], x
    for i in range(n):
        p = r.astype(BF16)
        parts.append(p)
        if i + 1 < n:
            r = r - p.astype(F32)
    return parts


def _dot(a, b, nt):
    dims = (((1,), (1,)), ((), ())) if nt else (((1,), (0,)), ((), ()))
    return lax.dot_general(a, b, dims, preferred_element_type=F32)


def _mmp(ap, bp, nt=False):
    order = max(len(ap), len(bp))
    acc = None
    for i in reversed(range(len(ap))):
        for j in reversed(range(len(bp))):
            if i + j < order:
                t = _dot(ap[i], bp[j], nt)
                acc = t if acc is None else acc + t
    return acc


def _mm(a, b, na=1, nb=1, nt=False):
    return _mmp(_split(a, na), _split(b, nb), nt)


def _rmsnorm(x, g):
    ms = jnp.mean(x * x, axis=-1, keepdims=True)
    return x * lax.rsqrt(ms + NORM_EPS) * g


def _modulated(x, g, mod_row, d):
    shift, scale = mod_row[:, 0:d], mod_row[:, d:2 * d]
    return _rmsnorm(x, g) * (1.0 + scale) + shift


def _mod_row(mod_ref, ctx):
    row = 2 if ctx else pl.program_id(0)
    return mod_ref[pl.ds(row, 1), :]


def _adaln_kernel(c_ref, w_ref, b_ref, o_ref):
    o_ref[...] = _mm(_silu(c_ref[...]), w_ref[...], 2, 2) + b_ref[...]


def _adaln(cvec, w, b):
    d, n = w.shape
    tn = 768
    return pl.pallas_call(
        _adaln_kernel,
        grid=(n // tn,),
        in_specs=[pl.BlockSpec((SUBLANES, d), lambda j: (0, 0)),
                  pl.BlockSpec((d, tn), lambda j: (0, j)),
                  pl.BlockSpec((1, tn), lambda j: (0, j))],
        out_specs=pl.BlockSpec((SUBLANES, tn), lambda j: (0, j)),
        out_shape=jax.ShapeDtypeStruct((SUBLANES, n), F32),
        compiler_params=pltpu.CompilerParams(dimension_semantics=("parallel",),
                                             vmem_limit_bytes=VMEM_LIMIT),
        name="adaln",
    )(cvec, w, b.reshape(1, n))


def _even_kernel(x_ref, mod_ref, g_ref, win_ref, wout_ref, poolw_ref, pscale_ref, sconv_ref,
                 band_ref, invc_ref, o_ref, *, ctx, line):
    tm, d = x_ref.shape
    db = d // 2
    grp = db // len(POOL_WINDOWS)
    x = x_ref[...]
    mrow = _mod_row(mod_ref, ctx)
    h = _modulated(x, g_ref[...], mrow, d)
    proj = _mm(h, win_ref[...])
    u_a, z_a = proj[:, 0:db], proj[:, db:2 * db]
    v_b, g_b = proj[:, 2 * db:3 * db], proj[:, 3 * db:4 * db]
    g_c, z_b = proj[:, 4 * db:5 * db], proj[:, 5 * db:6 * db]

    ya = []
    for i in range(len(POOL_WINDOWS)):
        ug = u_a[:, i * grp:(i + 1) * grp]
        sums = _mm(band_ref[i], ug, 1, 2)
        pooled = sums * invc_ref[i] - ug
        ya.append(_mm(pooled, poolw_ref[i]))
    y_a = jnp.concatenate(ya, axis=-1) * pscale_ref[...] * _silu(z_a)

    q = g_c * v_b
    pos = lax.broadcasted_iota(jnp.int32, q.shape, 0) % line
    q_prev = jnp.where(pos >= 1, pltpu.roll(q, 1, 0), 0.0)
    q_next = jnp.where(pos <= line - 2, pltpu.roll(q, tm - 1, 0), 0.0)
    w = sconv_ref[...]
    conv = q_prev * w[0:1, :] + q * w[1:2, :] + q_next * w[2:3, :]
    y_b = g_b * conv * _silu(z_b)

    y = _mm(jnp.concatenate([y_a, y_b], axis=-1), wout_ref[...])
    o_ref[...] = x + mrow[:, 2 * d:3 * d] * y


def _pool_constants(tm, line):
    pos = np.arange(tm) % line
    lid = np.arange(tm) // line
    bands, invs = [], []
    for w in POOL_WINDOWS:
        lo = np.clip(pos - w // 2, 0, line)
        hi = np.clip(pos - w // 2 + w, 0, line)
        same = lid[:, None] == lid[None, :]
        band = same & (pos[None, :] >= lo[:, None]) & (pos[None, :] < hi[:, None])
        bands.append(band.astype(np.float32))
        invs.append(np.broadcast_to((1.0 / (hi - lo))[:, None], (tm, 128)).astype(np.float32))
    return jnp.asarray(np.stack(bands), BF16), jnp.asarray(np.stack(invs), F32)


def _even_layer(x, mod, norm_g, w_in, w_out, pool_w, pool_scale, sconv_w, *, ctx, line):
    b, t, d = x.shape
    tm = min(TILE_TOKENS, t)
    assert t % tm == 0 and tm % line == 0
    db = d // 2
    band, invc = _pool_constants(tm, line)
    full = lambda a: pl.BlockSpec(a.shape, lambda bi, i: (0,) * a.ndim)
    args = (mod, norm_g.reshape(1, d), w_in.astype(BF16), w_out.astype(BF16), pool_w.astype(BF16),
            pool_scale.reshape(1, db), sconv_w, band, invc)
    return pl.pallas_call(
        functools.partial(_even_kernel, ctx=ctx, line=line),
        grid=(b, t // tm),
        in_specs=[pl.BlockSpec((None, tm, d), lambda bi, i: (bi, i, 0))] + [full(a) for a in args],
        out_specs=pl.BlockSpec((None, tm, d), lambda bi, i: (bi, i, 0)),
        out_shape=jax.ShapeDtypeStruct((b, t, d), F32),
        compiler_params=pltpu.CompilerParams(dimension_semantics=("parallel", "parallel"),
                                             vmem_limit_bytes=VMEM_LIMIT),
        name="even_ctx" if ctx else "even_latent",
    )(x, *args)


def _inproj_kernel(x_ref, mod_ref, g_ref, w_ref, o_ref, *, ctx):
    d = x_ref.shape[-1]
    h = _modulated(x_ref[...], g_ref[...], _mod_row(mod_ref, ctx), d)
    o_ref[...] = _mm(h, w_ref[...])


def _inproj(x, mod, norm_g, w_in, *, ctx):
    b, t, d = x.shape
    n = w_in.shape[1]
    tm = min(TILE_TOKENS, t)
    assert t % tm == 0
    return pl.pallas_call(
        functools.partial(_inproj_kernel, ctx=ctx),
        grid=(b, t // tm),
        in_specs=[pl.BlockSpec((None, tm, d), lambda bi, i: (bi, i, 0)),
                  pl.BlockSpec(mod.shape, lambda bi, i: (0, 0)),
                  pl.BlockSpec((1, d), lambda bi, i: (0, 0)),
                  pl.BlockSpec((d, n), lambda bi, i: (0, 0))],
        out_specs=pl.BlockSpec((None, tm, n), lambda bi, i: (bi, i, 0)),
        out_shape=jax.ShapeDtypeStruct((b, t, n), F32),
        compiler_params=pltpu.CompilerParams(dimension_semantics=("parallel", "parallel"),
                                             vmem_limit_bytes=VMEM_LIMIT),
        name="inproj_ctx" if ctx else "inproj_latent",
    )(x, mod, norm_g.reshape(1, d), w_in.astype(BF16))


def _prep_kernel(pm_ref, pp_ref, pn_ref, mu_ref, w0_ref, a0_ref, w1_ref, w2_ref, a1_ref, a2_ref,
                 g1_ref, g2_ref, kk_ref, ka_ref, rk_ref, seg_ref, tri_ref, blk_ref,
                 v_ref, g_ref, bonus_ref, at_ref, rt_ref, bk_ref, gc_ref):
    tm = pm_ref.shape[0]
    db = v_ref.shape[-1]
    i, n = pl.program_id(1), pl.num_programs(1)
    z = pm_ref[...]
    rows = lax.broadcasted_iota(jnp.int32, z.shape, 0)
    prev_row = pp_ref[SUBLANES - 1:SUBLANES, :] * jnp.where(i > 0, 1.0, 0.0)
    next_row = pn_ref[0:1, :] * jnp.where(i < n - 1, 1.0, 0.0)
    z_prev = jnp.where(rows == 0, prev_row, pltpu.roll(z, 1, 0))
    z_next = jnp.where(rows == tm - 1, next_row, pltpu.roll(z, tm - 1, 0))
    dz = 0.5 * (z_prev + z_next) - z
    mu = mu_ref[...]
    u, du = z[:, 0:db], dz[:, 0:db]
    r = z[:, db:2 * db] + dz[:, db:2 * db] * mu[0:1, :]
    k = z[:, 2 * db:3 * db] + dz[:, 2 * db:3 * db] * mu[1:2, :]
    v = z[:, 3 * db:4 * db] + dz[:, 3 * db:4 * db] * mu[2:3, :]
    uw = u + du * mu[3:4, :]
    ua = u + du * mu[4:5, :]
    ug = u + du * mu[5:6, :]

    g_ref[...] = _mm(_sigmoid(_mm(ug, g1_ref[...])), g2_ref[...])
    v_ref[...] = v.astype(BF16)
    lw_pre = w0_ref[...] + _mm(jnp.tanh(_mm(uw, w1_ref[...])), w2_ref[...])
    a_pre = a0_ref[...] + _mm(_mm(ua, a1_ref[...]), a2_ref[...])

    kk = k * kk_ref[...]
    norm = jnp.sqrt(_mm(kk * kk, seg_ref[...], 3, 1))
    kk = kk / jnp.maximum(norm, 1e-12)

    first_chunk = lax.broadcasted_iota(jnp.int32, (db, tm), 1) < CHUNK
    ksum = None
    for d in range(2):
        x = -lw_pre[:, d * db:(d + 1) * db]
        softplus = jnp.maximum(x, 0.0) + jnp.log(1.0 + jnp.exp(-jnp.abs(x)))
        lw = -jnp.exp(-softplus - 0.5)
        a = _sigmoid(a_pre[:, d * db:(d + 1) * db])
        k_d = k * (1.0 + (a - 1.0) * ka_ref[...])
        ksum = k_d if ksum is None else ksum + k_d
        lw_parts = _split(lw, 3)
        cl = _mmp([tri_ref[d]], lw_parts)
        tot = _mmp([blk_ref[...]], lw_parts)
        rel = cl - tot
        at_ref[d] = (-kk * jnp.exp(rel - lw)).astype(BF16)
        rt_ref[d] = (r * jnp.exp(rel)).astype(BF16)
        gout = jnp.exp(-rel)
        bh_t = (kk * a * gout).T
        kh_t = (k_d * gout).T
        bh_r, kh_r = pltpu.roll(bh_t, CHUNK, 1), pltpu.roll(kh_t, CHUNK, 1)
        for c in range(2):
            own = first_chunk if c == 0 else jnp.logical_not(first_chunk)
            bk_ref[d, c] = jnp.concatenate([jnp.where(own, bh_t, bh_r), jnp.where(own, kh_t, kh_r)],
                                           axis=1).astype(BF16)
        gam = jnp.exp(tot)
        for c in range(tm // CHUNK):
            gc_ref[d, c] = gam[c * CHUNK:c * CHUNK + SUBLANES, :]
    bonus_ref[...] = _mm(r * ksum * rk_ref[...], seg_ref[...], 3, 1) * v


def _prep_constants(tm):
    t = np.arange(tm)
    same = (t[:, None] // CHUNK) == (t[None, :] // CHUNK)
    lower = same & (t[None, :] <= t[:, None])
    upper = same & (t[None, :] >= t[:, None])
    tri = np.stack([lower, upper]).astype(np.float32)
    return jnp.asarray(tri, BF16), jnp.asarray(same.astype(np.float32), BF16)


def _pad_to(a, axis, size):
    pad = [(0, 0)] * a.ndim
    pad[axis] = (0, size - a.shape[axis])
    return jnp.pad(a, pad)


def _rwkv_prep(proj, p):
    b, t, _ = proj.shape
    db = p["k_k"].shape[-1]
    tm = PREP_TOKENS
    assert t % tm == 0 and tm == 2 * CHUNK
    nc = t // CHUNK
    nb = t // SUBLANES
    per = tm // SUBLANES
    tri, blk = _prep_constants(tm)
    heads = db // RWKV_HEAD
    seg = jnp.asarray(np.kron(np.eye(heads), np.ones((RWKV_HEAD, RWKV_HEAD))), BF16)

    def lora_pair(w_a, w_b):
        rank = w_a.shape[-1]
        down = _pad_to(jnp.concatenate([w_a[0], w_a[1]], axis=-1), 1, LORA_PAD)
        up = jnp.zeros((LORA_PAD, 2 * db), F32)
        up = up.at[0:rank, 0:db].set(w_b[0]).at[rank:2 * rank, db:2 * db].set(w_b[1])
        return down.astype(BF16), up.astype(BF16)

    w1c, w2c = lora_pair(p["w1"], p["w2"])
    a1c, a2c = lora_pair(p["a1"], p["a2"])
    consts = (_pad_to(p["rwkv_mu"], 0, SUBLANES), p["w0"].reshape(1, 2 * db), p["a0"].reshape(1, 2 * db),
              w1c, w2c, a1c, a2c,
              _pad_to(p["g1"], 1, LORA_PAD).astype(BF16), _pad_to(p["g2"], 0, LORA_PAD).astype(BF16),
              p["k_k"].reshape(1, db), p["k_a"].reshape(1, db), p["r_k"].reshape(1, db), seg, tri, blk)
    full = lambda a: pl.BlockSpec(a.shape, lambda bi, i: (0,) * a.ndim)
    tok = pl.BlockSpec((None, tm, db), lambda bi, i: (bi, i, 0))
    dirtok = pl.BlockSpec((None, 2, tm, db), lambda bi, i: (bi, 0, i, 0))
    tok_shape = jax.ShapeDtypeStruct((b, t, db), F32)
    dir_shape = jax.ShapeDtypeStruct((b, 2, t, db), BF16)
    return pl.pallas_call(
        _prep_kernel,
        grid=(b, t // tm),
        in_specs=[pl.BlockSpec((None, tm, 4 * db), lambda bi, i: (bi, i, 0)),
                  pl.BlockSpec((None, SUBLANES, 4 * db), lambda bi, i: (bi, jnp.maximum(i * per - 1, 0), 0)),
                  pl.BlockSpec((None, SUBLANES, 4 * db),
                               lambda bi, i: (bi, jnp.minimum((i + 1) * per, nb - 1), 0))]
        + [full(a) for a in consts],
        out_specs=[tok, tok, tok, dirtok, dirtok,
                   pl.BlockSpec((None, 2, tm // CHUNK, db, 4 * CHUNK), lambda bi, i: (bi, 0, i, 0, 0)),
                   pl.BlockSpec((None, 2, tm // CHUNK, SUBLANES, db), lambda bi, i: (bi, 0, i, 0, 0))],
        out_shape=[jax.ShapeDtypeStruct((b, t, db), BF16), tok_shape, tok_shape, dir_shape, dir_shape,
                   jax.ShapeDtypeStruct((b, 2, nc, db, 4 * CHUNK), BF16),
                   jax.ShapeDtypeStruct((b, 2, nc, SUBLANES, db), F32)],
        compiler_params=pltpu.CompilerParams(dimension_semantics=("parallel", "parallel"),
                                             vmem_limit_bytes=VMEM_LIMIT),
        name="rwkv_prep",
    )(proj, proj, proj, *consts)


SCAN_PARTS = 1
STATE_PARTS = 1


def _scan_kernel(at0, rt0, bk0, v0, gc0, at1, rt1, bk1, v1, gc1, s0_ref,
                 hm_ref, ms_ref, mi_ref, idw_ref, bdm_ref, eye_ref,
                 of_ref, or_ref, st_ref):
    @pl.when(pl.program_id(0) == 0)
    def _():
        st_ref[...] = s0_ref[...]

    npart = SCAN_PARTS
    pair = GROUP // 2
    bdm_b = bdm_ref[...]
    bd2 = bdm_b[0:pair, 0:pair].astype(F32)
    eye2 = eye_ref[0:pair, 0:pair]

    def bd(x):
        return [jnp.concatenate([part * hm_ref[h] for h in range(HEADS_PER_GROUP)], axis=0)
                for part in _split(x, npart)]

    def cat_parts(plists):
        return [jnp.concatenate(ps, axis=-1) for ps in zip(*plists)]

    steps = int(np.log2(CHUNK)) - 1

    def load(c):
        at_r, rt_r, bk_r, v_r, gc_r = c["refs"]
        bi, g, sl = c["bi"], c["g"], c["sl"]
        c.update(at=at_r[bi, :, sl], rt=rt_r[bi, :, sl], v=v_r[bi, :, sl], gc=gc_r[bi, 0:1, sl],
                 bk=bk_r[bi, g * GROUP:(g + 1) * GROUP, :], st=st_ref[bi, c["d"], g])

    def interactions(c):
        ms, mi = ms_ref[c["d"]], mi_ref[c["d"]]
        bk = c["bk"]
        wts = jnp.concatenate([jnp.concatenate([bk[:, 0:pair]] * 2, axis=1) * bdm_b,
                               jnp.concatenate([bk[:, pair:]] * 2, axis=1) * bdm_b], axis=1)
        gmat = _dot(jnp.concatenate([c["at"], c["rt"]], axis=0), wts, False)
        lab = gmat[0:CHUNK, 0:GROUP] * ms
        c.update(lak=gmat[0:CHUNK, GROUP:] * ms, mrb=gmat[CHUNK:, 0:GROUP] * mi,
                 mrk=gmat[CHUNK:, GROUP:] * mi, pw=lab, tinv=idw_ref[...] + lab)

    def invert(s):
        def stage(c):
            pw, tinv = c["pw"], c["tinv"]
            wts = bd(pw)
            if s == 0:
                pw = _mmp(_split(pw, npart), wts)
            elif s < steps:
                res = _mmp(_split(jnp.concatenate([pw, tinv], axis=0), npart), wts)
                pw, tinv = res[0:CHUNK], tinv + res[CHUNK:]
            else:
                tinv = tinv + _mmp(_split(tinv, npart), wts)
            c.update(pw=pw, tinv=tinv)
        return stage

    def values(c):
        lv_mv = _mmp(_split(jnp.concatenate([c["lak"], c["mrk"]], axis=0), npart), bd(c["v"]))
        c.update(lv=lv_mv[0:CHUNK], mv=lv_mv[CHUNK:])

    def solve(c):
        aw = _mmp(_split(c["tinv"], npart), cat_parts([bd(c["at"]), bd(c["lv"])]))
        c.update(ahat=aw[:, 0:GROUP], wv=aw[:, GROUP:])

    def outputs(c):
        ro = _mmp(_split(c["mrb"], npart), cat_parts([bd(c["ahat"]), bd(c["wv"])]))
        c.update(rhat=(c["rt"].astype(F32) + ro[:, 0:GROUP]) * c["gc"], ointra=ro[:, GROUP:] + c["mv"])

    def transition(c):
        zero_rows = jnp.zeros((CHUNK, GROUP), BF16)
        zero_half = jnp.zeros((CHUNK, pair), BF16)
        zero_blk = jnp.zeros((pair, pair), F32)
        pms, qms = [], []
        for p in range(2):
            ps = slice(p * pair, (p + 1) * pair)
            top = jnp.concatenate([c["ahat"][:, ps], c["wv"][:, ps]], axis=1).astype(BF16)
            mid = jnp.concatenate([zero_half, c["v"][:, ps]], axis=1)
            rhs = jnp.concatenate([top, zero_rows, mid, zero_rows], axis=0)
            pq = _dot(c["bk"][ps, :], rhs, False)
            pblk = (pq[:, 0:pair] * bd2 + eye2) * c["gc"][:, ps]
            qblk = pq[:, pair:] * bd2
            pms.append(jnp.concatenate([pblk, zero_blk] if p == 0 else [zero_blk, pblk], axis=1))
            qms.append(jnp.concatenate([qblk, zero_blk] if p == 0 else [zero_blk, qblk], axis=1))
        c.update(pm=jnp.concatenate(pms, axis=0), qm=jnp.concatenate(qms, axis=0))

    def advance(c):
        rs = _mmp(_split(jnp.concatenate([c["rhat"], c["pm"]], axis=0), STATE_PARTS),
                  _split(c["st"], STATE_PARTS))
        c.update(o=rs[0:CHUNK] + c["ointra"], st_new=rs[CHUNK:] + c["qm"])

    def store(c):
        c["out"][c["bi"], :, c["sl"]] = c["o"]
        st_ref[c["bi"], c["d"], c["g"]] = c["st_new"]

    chains = []
    dirs = ((at0, rt0, bk0, v0, gc0, of_ref), (at1, rt1, bk1, v1, gc1, or_ref))
    for bi in range(at0.shape[0]):
        for d, refs in enumerate(dirs):
            for g in range(at0.shape[-1] // GROUP):
                chains.append(dict(bi=bi, d=d, g=g, sl=slice(g * GROUP, (g + 1) * GROUP),
                                   refs=refs[:5], out=refs[5]))
    stages = ([load, interactions] + [invert(s) for s in range(steps + 1)]
              + [values, solve, outputs, transition, advance, store])
    for stage in stages:
        for c in chains:
            stage(c)


def _scan_constants():
    t = np.arange(CHUNK)[:, None]
    lane = np.arange(GROUP)[None, :]
    s = lane % CHUNK
    strict = np.stack([s < t, s > t]).astype(np.float32)
    incl = np.stack([s <= t, s >= t]).astype(np.float32)
    idw = (s == t).astype(np.float32)
    hm = np.stack([np.broadcast_to((lane // RWKV_HEAD) == h, (CHUNK, GROUP)) for h in range(HEADS_PER_GROUP)])
    bdm = np.kron(np.eye(HEADS_PER_GROUP), np.ones((RWKV_HEAD, RWKV_HEAD)))
    return (jnp.asarray(hm.astype(np.float32), BF16), jnp.asarray(strict), jnp.asarray(incl), jnp.asarray(idw),
            jnp.asarray(bdm, BF16), jnp.eye(GROUP, dtype=F32))


def _rwkv_scan(at, rt, bk, v, gc, s0):
    b, _, t, db = at.shape
    nc = t // CHUNK
    consts = _scan_constants()
    full = lambda a: pl.BlockSpec(a.shape, lambda i: (0,) * a.ndim)
    fwd, rev = (lambda i: i), (lambda i: nc - 1 - i)
    dspec = lambda d, ci: pl.BlockSpec((b, None, CHUNK, db), lambda i: (0, d, ci(i), 0))
    kspec = lambda d, ci: pl.BlockSpec((b, None, None, db, 4 * CHUNK), lambda i: (0, d, ci(i), 0, 0))
    vspec = lambda ci: pl.BlockSpec((b, CHUNK, db), lambda i: (0, ci(i), 0))
    gspec = lambda d, ci: pl.BlockSpec((b, None, None, SUBLANES, db), lambda i: (0, d, ci(i), 0, 0))
    in_specs = ([dspec(0, fwd), dspec(0, fwd), kspec(0, fwd), vspec(fwd), gspec(0, fwd),
                 dspec(1, rev), dspec(1, rev), kspec(1, rev), vspec(rev), gspec(1, rev), full(s0)]
                + [full(a) for a in consts])
    return pl.pallas_call(
        _scan_kernel,
        grid=(nc,),
        in_specs=in_specs,
        out_specs=[vspec(fwd), vspec(rev), full(s0)],
        out_shape=[jax.ShapeDtypeStruct((b, t, db), F32), jax.ShapeDtypeStruct((b, t, db), F32),
                   jax.ShapeDtypeStruct(s0.shape, F32)],
        compiler_params=pltpu.CompilerParams(dimension_semantics=("arbitrary",),
                                             vmem_limit_bytes=VMEM_LIMIT),
        name="rwkv_scan",
    )(at, rt, bk, v, gc, at, rt, bk, v, gc, s0, *consts)


def _conv_kernel(p1_ref, p2_ref, w_ref, b_ref, o_ref, h_ref, *, stride):
    t, c = p1_ref.shape
    halo = (CONF_CONV_W // 2) * stride
    nsteps = t // CONV_ROWS
    zeros = jnp.zeros((halo, c), F32)
    h_ref[0:halo, :] = zeros
    h_ref[halo + t:halo + t + halo, :] = zeros

    def glu(j, carry):
        r0 = pl.multiple_of(j * CONV_ROWS, CONV_ROWS)
        h_ref[pl.ds(halo + r0, CONV_ROWS), :] = (p1_ref[pl.ds(r0, CONV_ROWS), :]
                                                  * _sigmoid(p2_ref[pl.ds(r0, CONV_ROWS), :]))
        return carry

    lax.fori_loop(0, nsteps, glu, 0)
    w = w_ref[...]
    bias = b_ref[...]

    def conv(j, carry):
        r0 = pl.multiple_of(j * CONV_ROWS, CONV_ROWS)
        acc = jnp.broadcast_to(bias, (CONV_ROWS, c))
        for tap in range(CONF_CONV_W):
            acc = acc + h_ref[pl.ds(r0 + tap * stride, CONV_ROWS), :] * w[tap:tap + 1, :]
        o_ref[pl.ds(r0, CONV_ROWS), :] = acc
        return carry

    lax.fori_loop(0, nsteps, conv, 0)


def _conformer_conv(proj, dw_w, dw_b, *, stride):
    b, t, n = proj.shape
    db = dw_w.shape[-1]
    lanes = 128
    assert t % CONV_ROWS == 0 and stride % SUBLANES == 0
    halo = (CONF_CONV_W // 2) * stride
    c1, c2 = 5 * db // lanes, 6 * db // lanes
    return pl.pallas_call(
        functools.partial(_conv_kernel, stride=stride),
        grid=(b, db // lanes),
        in_specs=[pl.BlockSpec((None, t, lanes), lambda bi, j: (bi, 0, c1 + j)),
                  pl.BlockSpec((None, t, lanes), lambda bi, j: (bi, 0, c2 + j)),
                  pl.BlockSpec((_round_up(CONF_CONV_W, SUBLANES), lanes), lambda bi, j: (0, j)),
                  pl.BlockSpec((1, lanes), lambda bi, j: (0, j))],
        out_specs=pl.BlockSpec((None, t, lanes), lambda bi, j: (bi, 0, j)),
        out_shape=jax.ShapeDtypeStruct((b, t, db), F32),
        scratch_shapes=[pltpu.VMEM((t + 2 * halo, lanes), F32)],
        compiler_params=pltpu.CompilerParams(dimension_semantics=("parallel", "parallel"),
                                             vmem_limit_bytes=VMEM_LIMIT),
        name="conformer_conv",
    )(proj, proj, _pad_to(dw_w, 0, _round_up(CONF_CONV_W, SUBLANES)), dw_b.reshape(1, db))


def _round_up(x, m):
    return (x + m - 1) // m * m


def _merge_kernel(x_ref, of_ref, or_ref, g_ref, bonus_ref, conv_ref, zc_ref, zd_ref, mod_ref,
                  lnxg_ref, lnxb_ref, clng_ref, clnb_ref, wout_ref, fg_ref, seg_ref, o_ref):
    d = x_ref.shape[-1]
    x = x_ref[...]
    mrow = _mod_row(mod_ref, False)
    o = of_ref[...] + or_ref[...]
    seg = seg_ref[...]
    dev = o - _mm(o, seg, 3, 1)
    var = _mm(dev * dev, seg, 3, 1)
    on = dev * lax.rsqrt(var + LNX_EPS) * lnxg_ref[...] + lnxb_ref[...]
    y_c = (on + bonus_ref[...]) * g_ref[...] * _silu(zc_ref[...])

    c = conv_ref[...]
    mean = jnp.mean(c, axis=-1, keepdims=True)
    cdev = c - mean
    cvar = jnp.mean(cdev * cdev, axis=-1, keepdims=True)
    ln = cdev * lax.rsqrt(cvar + CONF_LN_EPS) * clng_ref[...] + clnb_ref[...]
    y_d = _silu(ln) * _silu(zd_ref[...])

    y = _mm(jnp.concatenate([y_c, y_d], axis=-1), wout_ref[...])
    o_ref[...] = _rmsnorm(x + mrow[:, 2 * d:3 * d] * y, fg_ref[...])


def _merge(x, o_f, o_r, g, bonus, conv, proj, mod, p, final_g):
    b, t, d = x.shape
    db = d // 2
    tm = min(TILE_TOKENS, t)
    heads = db // RWKV_HEAD
    seg = jnp.asarray(np.kron(np.eye(heads), np.full((RWKV_HEAD, RWKV_HEAD), 1.0 / RWKV_HEAD)), BF16)
    consts = (mod, p["lnx_g"].reshape(1, db), p["lnx_b"].reshape(1, db), p["conf_ln_g"].reshape(1, db),
              p["conf_ln_b"].reshape(1, db), p["out_o"].astype(BF16), final_g.reshape(1, d), seg)
    full = lambda a: pl.BlockSpec(a.shape, lambda bi, i: (0,) * a.ndim)
    tok = pl.BlockSpec((None, tm, db), lambda bi, i: (bi, i, 0))
    col = lambda cblk: pl.BlockSpec((None, tm, db), lambda bi, i: (bi, i, cblk))
    return pl.pallas_call(
        _merge_kernel,
        grid=(b, t // tm),
        in_specs=[pl.BlockSpec((None, tm, d), lambda bi, i: (bi, i, 0)), tok, tok, tok, tok, tok,
                  col(4), col(7)] + [full(a) for a in consts],
        out_specs=pl.BlockSpec((None, tm, d), lambda bi, i: (bi, i, 0)),
        out_shape=jax.ShapeDtypeStruct((b, t, d), F32),
        compiler_params=pltpu.CompilerParams(dimension_semantics=("parallel", "parallel"),
                                             vmem_limit_bytes=VMEM_LIMIT),
        name="merge",
    )(x, o_f, o_r, g, bonus, conv, proj, proj, *consts)


def kernel(x, c, ctx, c_ctx, ada_w_e, ada_b_e, norm_e, in_e, out_e, pool_w, pool_scale, sconv_w,
           ada_w_o, ada_b_o, norm_o, in_o, out_o, rwkv_mu, w0, w1, w2, a0, a1, a2, g1, g2,
           k_k, k_a, r_k, lnx_g, lnx_b, conf_dw_w, conf_dw_b, conf_ln_g, conf_ln_b, final_g):
    b, t, d = x.shape
    db = d // 2
    assert b == 2
    cvec = _pad_to(jnp.concatenate([c, c_ctx[None, :]], axis=0), 0, SUBLANES)
    mod_e = _adaln(cvec, ada_w_e[0], ada_b_e[0])
    mod_o = _adaln(cvec, ada_w_o[0], ada_b_o[0])

    even = dict(norm_g=norm_e[0], w_in=in_e[0], w_out=out_e[0], pool_w=pool_w[0],
                pool_scale=pool_scale[0], sconv_w=sconv_w[0])
    x1 = _even_layer(x, mod_e, ctx=False, line=GRID_W, **even)
    xc1 = _even_layer(ctx, mod_e, ctx=True, line=ctx.shape[1], **even)

    p = dict(rwkv_mu=rwkv_mu[0], w0=w0[0], w1=w1[0], w2=w2[0], a0=a0[0], a1=a1[0], a2=a2[0],
             g1=g1[0], g2=g2[0], k_k=k_k[0], k_a=k_a[0], r_k=r_k[0], lnx_g=lnx_g[0], lnx_b=lnx_b[0],
             conf_ln_g=conf_ln_g[0], conf_ln_b=conf_ln_b[0], out_o=out_o[0])

    proj_c = _inproj(xc1, mod_o, norm_o[0], in_o[0][:, 0:4 * db], ctx=True)
    v_c, _, _, at_c, rt_c, bk_c, gc_c = _rwkv_prep(proj_c, p)
    zero_state = jnp.zeros((b, 2, db // GROUP, GROUP, GROUP), F32)
    _, _, ctx_state = _rwkv_scan(at_c, rt_c, bk_c, v_c, gc_c, zero_state)

    proj = _inproj(x1, mod_o, norm_o[0], in_o[0], ctx=False)
    v_l, g_l, bonus, at_l, rt_l, bk_l, gc_l = _rwkv_prep(proj, p)
    o_f, o_r, _ = _rwkv_scan(at_l, rt_l, bk_l, v_l, gc_l, ctx_state)
    conv = _conformer_conv(proj, conf_dw_w[0], conf_dw_b[0], stride=GRID_W)
    return _merge(x1, o_f, o_r, g_l, bonus, conv, proj, mod_o, p, final_g)
```

```python
import functools

import numpy as np
import jax
import jax.numpy as jnp
from jax import lax
from jax.experimental import pallas as pl
from jax.experimental.pallas import tpu as pltpu

F32 = jnp.float32
BF16 = jnp.bfloat16

GRID_W = 64
POOL_WINDOWS = (2, 4, 8, 16)
SHORT_CONV_W = 3
CONF_CONV_W = 31
RWKV_HEAD = 64
NORM_EPS = 1e-6
LNX_EPS = 64e-5
CONF_LN_EPS = 1e-5

CHUNK = 64
HEADS_PER_GROUP = 4
GROUP = HEADS_PER_GROUP * RWKV_HEAD
LORA_PAD = 128
SUBLANES = 8
VMEM_LIMIT = 52 * 1024 * 1024

TILE_TOKENS = 256
PREP_TOKENS = 128
CONV_ROWS = 128


def _sigmoid(x):
    return 1.0 / (1.0 + jnp.exp(-x))


def _silu(x):
    return x * _sigmoid(x)


def _split(x, n):
    if x.dtype == BF16:
        return [x]
    parts, r = [], x
    for i in range(n):
        p = r.astype(BF16)
        parts.append(p)
        if i + 1 < n:
            r = r - p.astype(F32)
    return parts


def _dot(a, b, nt):
    dims = (((1,), (1,)), ((), ())) if nt else (((1,), (0,)), ((), ()))
    return lax.dot_general(a, b, dims, preferred_element_type=F32)


def _mmp(ap, bp, nt=False):
    order = max(len(ap), len(bp))
    acc = None
    for i in reversed(range(len(ap))):
        for j in reversed(range(len(bp))):
            if i + j < order:
                t = _dot(ap[i], bp[j], nt)
                acc = t if acc is None else acc + t
    return acc


def _mm(a, b, na=1, nb=1, nt=False):
    return _mmp(_split(a, na), _split(b, nb), nt)


def _rmsnorm(x, g):
    ms = jnp.mean(x * x, axis=-1, keepdims=True)
    return x * lax.rsqrt(ms + NORM_EPS) * g


def _modulated(x, g, mod_row, d):
    shift, scale = mod_row[:, 0:d], mod_row[:, d:2 * d]
    return _rmsnorm(x, g) * (1.0 + scale) + shift


def _mod_row(mod_ref, ctx):
    row = 2 if ctx else pl.program_id(0)
    return mod_ref[pl.ds(row, 1), :]


def _adaln_kernel(c_ref, w_ref, b_ref, o_ref):
    o_ref[...] = _mm(_silu(c_ref[...]), w_ref[...], 2, 2) + b_ref[...]


def _adaln(cvec, w, b):
    d, n = w.shape
    tn = 768
    return pl.pallas_call(
        _adaln_kernel,
        grid=(n // tn,),
        in_specs=[pl.BlockSpec((SUBLANES, d), lambda j: (0, 0)),
                  pl.BlockSpec((d, tn), lambda j: (0, j)),
                  pl.BlockSpec((1, tn), lambda j: (0, j))],
        out_specs=pl.BlockSpec((SUBLANES, tn), lambda j: (0, j)),
        out_shape=jax.ShapeDtypeStruct((SUBLANES, n), F32),
        compiler_params=pltpu.CompilerParams(dimension_semantics=("parallel",),
                                             vmem_limit_bytes=VMEM_LIMIT),
        name="adaln",
    )(cvec, w, b.reshape(1, n))


def _even_kernel(x_ref, mod_ref, g_ref, win_ref, wout_ref, poolw_ref, pscale_ref, sconv_ref,
                 band_ref, invc_ref, o_ref, *, ctx, line):
    tm, d = x_ref.shape
    db = d // 2
    grp = db // len(POOL_WINDOWS)
    x = x_ref[...]
    mrow = _mod_row(mod_ref, ctx)
    h = _modulated(x, g_ref[...], mrow, d)
    proj = _mm(h, win_ref[...])
    u_a, z_a = proj[:, 0:db], proj[:, db:2 * db]
    v_b, g_b = proj[:, 2 * db:3 * db], proj[:, 3 * db:4 * db]
    g_c, z_b = proj[:, 4 * db:5 * db], proj[:, 5 * db:6 * db]

    ya = []
    for i in range(len(POOL_WINDOWS)):
        ug = u_a[:, i * grp:(i + 1) * grp]
        sums = _mm(band_ref[i], ug)
        pooled = sums * invc_ref[i] - ug
        ya.append(_mm(pooled, poolw_ref[i]))
    y_a = jnp.concatenate(ya, axis=-1) * pscale_ref[...] * _silu(z_a)

    q = g_c * v_b
    pos = lax.broadcasted_iota(jnp.int32, q.shape, 0) % line
    q_prev = jnp.where(pos >= 1, pltpu.roll(q, 1, 0), 0.0)
    q_next = jnp.where(pos <= line - 2, pltpu.roll(q, tm - 1, 0), 0.0)
    w = sconv_ref[...]
    conv = q_prev * w[0:1, :] + q * w[1:2, :] + q_next * w[2:3, :]
    y_b = g_b * conv * _silu(z_b)

    y = _mm(jnp.concatenate([y_a, y_b], axis=-1), wout_ref[...])
    o_ref[...] = x + mrow[:, 2 * d:3 * d] * y


def _pool_constants(tm, line):
    pos = np.arange(tm) % line
    lid = np.arange(tm) // line
    bands, invs = [], []
    for w in POOL_WINDOWS:
        lo = np.clip(pos - w // 2, 0, line)
        hi = np.clip(pos - w // 2 + w, 0, line)
        same = lid[:, None] == lid[None, :]
        band = same & (pos[None, :] >= lo[:, None]) & (pos[None, :] < hi[:, None])
        bands.append(band.astype(np.float32))
        invs.append(np.broadcast_to((1.0 / (hi - lo))[:, None], (tm, 128)).astype(np.float32))
    return jnp.asarray(np.stack(bands), BF16), jnp.asarray(np.stack(invs), F32)


def _even_layer(x, mod, norm_g, w_in, w_out, pool_w, pool_scale, sconv_w, *, ctx, line):
    b, t, d = x.shape
    tm = min(TILE_TOKENS, t)
    assert t % tm == 0 and tm % line == 0
    db = d // 2
    band, invc = _pool_constants(tm, line)
    full = lambda a: pl.BlockSpec(a.shape, lambda bi, i: (0,) * a.ndim)
    args = (mod, norm_g.reshape(1, d), w_in.astype(BF16), w_out.astype(BF16), pool_w.astype(BF16),
            pool_scale.reshape(1, db), sconv_w, band, invc)
    return pl.pallas_call(
        functools.partial(_even_kernel, ctx=ctx, line=line),
        grid=(b, t // tm),
        in_specs=[pl.BlockSpec((None, tm, d), lambda bi, i: (bi, i, 0))] + [full(a) for a in args],
        out_specs=pl.BlockSpec((None, tm, d), lambda bi, i: (bi, i, 0)),
        out_shape=jax.ShapeDtypeStruct((b, t, d), F32),
        compiler_params=pltpu.CompilerParams(dimension_semantics=("parallel", "parallel"),
                                             vmem_limit_bytes=VMEM_LIMIT),
        name="even_ctx" if ctx else "even_latent",
    )(x, *args)


def _front_kernel(x_ref, xp_ref, xn_ref, mod_ref, ng_ref, win_ref,
                  mu_ref, w0_ref, a0_ref, w1_ref, w2_ref, a1_ref, a2_ref,
                  g1_ref, g2_ref, kk_ref, ka_ref, rk_ref, seg_ref, later_ref, whole_ref,
                  rest_ref, v_ref, g_ref, bonus_ref, at_ref, rt_ref, bk_ref, gc_ref, *, ctx):
    tm, d = x_ref.shape
    db = v_ref.shape[-1]
    i, n = pl.program_id(1), pl.num_programs(1)
    mrow = _mod_row(mod_ref, ctx)
    h_prev = _modulated(xp_ref[...], ng_ref[...], mrow, d) * jnp.where(i > 0, 1.0, 0.0)
    h_next = _modulated(xn_ref[...], ng_ref[...], mrow, d) * jnp.where(i < n - 1, 1.0, 0.0)
    h = _modulated(x_ref[...], ng_ref[...], mrow, d)
    z_ext = _mm(jnp.concatenate([h_prev, h, h_next], axis=0), win_ref[:, 0:4 * db])
    te = tm + 2 * SUBLANES
    nbr = (pltpu.roll(z_ext, 1, 0) + pltpu.roll(z_ext, te - 1, 0))[SUBLANES:tm + SUBLANES]
    z = z_ext[SUBLANES:tm + SUBLANES]
    dz = 0.5 * nbr - z
    mu = mu_ref[...]
    u, du = z[:, 0:db], dz[:, 0:db]
    r = z[:, db:2 * db] + dz[:, db:2 * db] * mu[0:1, :]
    k = z[:, 2 * db:3 * db] + dz[:, 2 * db:3 * db] * mu[1:2, :]
    v = z[:, 3 * db:4 * db] + dz[:, 3 * db:4 * db] * mu[2:3, :]
    uw = u + du * mu[3:4, :]
    ua = u + du * mu[4:5, :]
    ug = u + du * mu[5:6, :]

    h_b = h.astype(BF16)

    def rest(j):
        cols = slice(j * db, (j + 1) * db)
        rest_ref[:, cols] = _dot(h_b, win_ref[:, 4 * db + j * db:4 * db + (j + 1) * db], False)

    rest(0)
    g_ref[...] = _mm(_sigmoid(_mm(ug, g1_ref[...])), g2_ref[...])
    v_ref[...] = v.astype(BF16)
    lw_pre = w0_ref[...] + _mm(jnp.tanh(_mm(uw, w1_ref[...])), w2_ref[...])
    a_pre = a0_ref[...] + _mm(_mm(ua, a1_ref[...]), a2_ref[...])

    kk = k * kk_ref[...]
    norm = jnp.sqrt(_mm(kk * kk, seg_ref[...], 2, 1))
    kk = kk / jnp.maximum(norm, 1e-12)
    rest(1)

    first_chunk = lax.broadcasted_iota(jnp.int32, (db, tm), 1) < CHUNK
    ksum = None
    for d in range(2):
        x = -lw_pre[:, d * db:(d + 1) * db]
        softplus = jnp.maximum(x, 0.0) + jnp.log(1.0 + jnp.exp(-jnp.abs(x)))
        lw = -jnp.exp(-softplus - 0.5)
        a = _sigmoid(a_pre[:, d * db:(d + 1) * db])
        k_d = k * (1.0 + (a - 1.0) * ka_ref[...])
        ksum = k_d if ksum is None else ksum + k_d
        lw_parts = _split(lw, 2)
        rel = -_mmp([later_ref[d]], lw_parts)
        at_ref[d] = (-kk * jnp.exp(rel - lw)).astype(BF16)
        rt_ref[d] = (r * jnp.exp(rel)).astype(BF16)
        gout = jnp.exp(-rel)
        bh_t = (kk * a * gout).T
        kh_t = (k_d * gout).T
        bh_r, kh_r = pltpu.roll(bh_t, CHUNK, 1), pltpu.roll(kh_t, CHUNK, 1)
        for c in range(2):
            own = first_chunk if c == 0 else jnp.logical_not(first_chunk)
            bk_ref[d, c] = jnp.concatenate([jnp.where(own, bh_t, bh_r), jnp.where(own, kh_t, kh_r)],
                                           axis=1).astype(BF16)
        gam = jnp.exp(_mmp([whole_ref[...]], lw_parts))
        for c in range(tm // CHUNK):
            gc_ref[d, c] = gam[c * SUBLANES:(c + 1) * SUBLANES, :]
        rest(2 + d)
    bonus_ref[...] = _mm(r * ksum * rk_ref[...], seg_ref[...], 2, 1) * v


def _prep_constants(tm):
    t = np.arange(tm)
    same = (t[:, None] // CHUNK) == (t[None, :] // CHUNK)
    later = np.stack([same & (t[None, :] > t[:, None]), same & (t[None, :] < t[:, None])]).astype(np.float32)
    whole = (np.arange(tm // CHUNK * SUBLANES)[:, None] // SUBLANES) == (t[None, :] // CHUNK)
    return jnp.asarray(later, BF16), jnp.asarray(whole.astype(np.float32), BF16)


def _pad_to(a, axis, size):
    pad = [(0, 0)] * a.ndim
    pad[axis] = (0, size - a.shape[axis])
    return jnp.pad(a, pad)


def _odd_front(x, mod, norm_g, w_in, p, *, ctx):
    b, t, d = x.shape
    db = p["k_k"].shape[-1]
    tm = PREP_TOKENS
    assert t % tm == 0 and tm == 2 * CHUNK
    nc = t // CHUNK
    nb = t // SUBLANES
    per = tm // SUBLANES
    later, whole = _prep_constants(tm)
    heads = db // RWKV_HEAD
    seg = jnp.asarray(np.kron(np.eye(heads), np.ones((RWKV_HEAD, RWKV_HEAD))), BF16)

    def lora_pair(w_a, w_b):
        rank = w_a.shape[-1]
        down = _pad_to(jnp.concatenate([w_a[0], w_a[1]], axis=-1), 1, LORA_PAD)
        up = jnp.zeros((LORA_PAD, 2 * db), F32)
        up = up.at[0:rank, 0:db].set(w_b[0]).at[rank:2 * rank, db:2 * db].set(w_b[1])
        return down.astype(BF16), up.astype(BF16)

    w1c, w2c = lora_pair(p["w1"], p["w2"])
    a1c, a2c = lora_pair(p["a1"], p["a2"])
    consts = (mod, norm_g.reshape(1, d), w_in.astype(BF16),
              _pad_to(p["rwkv_mu"], 0, SUBLANES), p["w0"].reshape(1, 2 * db), p["a0"].reshape(1, 2 * db),
              w1c, w2c, a1c, a2c,
              _pad_to(p["g1"], 1, LORA_PAD).astype(BF16), _pad_to(p["g2"], 0, LORA_PAD).astype(BF16),
              p["k_k"].reshape(1, db), p["k_a"].reshape(1, db), p["r_k"].reshape(1, db), seg, later, whole)
    full = lambda a: pl.BlockSpec(a.shape, lambda bi, i: (0,) * a.ndim)
    tok = pl.BlockSpec((None, tm, db), lambda bi, i: (bi, i, 0))
    dirtok = pl.BlockSpec((None, 2, tm, db), lambda bi, i: (bi, 0, i, 0))
    tok_shape = jax.ShapeDtypeStruct((b, t, db), F32)
    dir_shape = jax.ShapeDtypeStruct((b, 2, t, db), BF16)
    nrest = w_in.shape[1] - 4 * db
    return pl.pallas_call(
        functools.partial(_front_kernel, ctx=ctx),
        grid=(b, t // tm),
        in_specs=[pl.BlockSpec((None, tm, d), lambda bi, i: (bi, i, 0)),
                  pl.BlockSpec((None, SUBLANES, d), lambda bi, i: (bi, jnp.maximum(i * per - 1, 0), 0)),
                  pl.BlockSpec((None, SUBLANES, d), lambda bi, i: (bi, jnp.minimum((i + 1) * per, nb - 1), 0))]
        + [full(a) for a in consts],
        out_specs=[pl.BlockSpec((None, tm, nrest), lambda bi, i: (bi, i, 0)), tok, tok, tok, dirtok, dirtok,
                   pl.BlockSpec((None, 2, tm // CHUNK, db, 4 * CHUNK), lambda bi, i: (bi, 0, i, 0, 0)),
                   pl.BlockSpec((None, 2, tm // CHUNK, SUBLANES, db), lambda bi, i: (bi, 0, i, 0, 0))],
        out_shape=[jax.ShapeDtypeStruct((b, t, nrest), F32),
                   jax.ShapeDtypeStruct((b, t, db), BF16), tok_shape, tok_shape, dir_shape, dir_shape,
                   jax.ShapeDtypeStruct((b, 2, nc, db, 4 * CHUNK), BF16),
                   jax.ShapeDtypeStruct((b, 2, nc, SUBLANES, db), F32)],
        compiler_params=pltpu.CompilerParams(dimension_semantics=("parallel", "parallel"),
                                             vmem_limit_bytes=VMEM_LIMIT),
        name="front_ctx" if ctx else "front_latent",
    )(x, x, x, *consts)


SCAN_PARTS = 1
STATE_PARTS = 1


def _scan_kernel(at0, rt0, bk0, v0, gc0, at1, rt1, bk1, v1, gc1, s0_ref,
                 hm_ref, ms_ref, mi_ref, idw_ref, bdm_ref, eye_ref,
                 of_ref, or_ref, st_ref):
    @pl.when(pl.program_id(0) == 0)
    def _():
        st_ref[...] = s0_ref[...]

    npart = SCAN_PARTS
    pair = GROUP // 2
    bdm_b = bdm_ref[...]
    bd2 = bdm_b[0:pair, 0:pair].astype(F32)
    eye2 = eye_ref[0:pair, 0:pair]

    def bd(x):
        return [jnp.concatenate([part * hm_ref[h] for h in range(HEADS_PER_GROUP)], axis=0)
                for part in _split(x, npart)]

    def cat_parts(plists):
        return [jnp.concatenate(ps, axis=-1) for ps in zip(*plists)]

    steps = int(np.log2(CHUNK)) - 1

    def load(c):
        at_r, rt_r, bk_r, v_r, gc_r = c["refs"]
        bi, g, sl = c["bi"], c["g"], c["sl"]
        c.update(at=at_r[bi, :, sl], rt=rt_r[bi, :, sl], v=v_r[bi, :, sl], gc=gc_r[bi, 0:1, sl],
                 bk=bk_r[bi, g * GROUP:(g + 1) * GROUP, :], st=st_ref[bi, c["d"], g])

    def interactions(c):
        ms, mi = ms_ref[c["d"]], mi_ref[c["d"]]
        bk = c["bk"]
        wts = jnp.concatenate([jnp.concatenate([bk[:, 0:pair]] * 2, axis=1) * bdm_b,
                               jnp.concatenate([bk[:, pair:]] * 2, axis=1) * bdm_b], axis=1)
        gmat = _dot(jnp.concatenate([c["at"], c["rt"]], axis=0), wts, False)
        lab = gmat[0:CHUNK, 0:GROUP] * ms
        c.update(lak=gmat[0:CHUNK, GROUP:] * ms, mrb=gmat[CHUNK:, 0:GROUP] * mi,
                 mrk=gmat[CHUNK:, GROUP:] * mi, pw=lab, tinv=idw_ref[...] + lab)

    def invert(s):
        def stage(c):
            pw, tinv = c["pw"], c["tinv"]
            wts = bd(pw)
            if s == 0:
                pw = _mmp(_split(pw, npart), wts)
            elif s < steps:
                res = _mmp(_split(jnp.concatenate([pw, tinv], axis=0), npart), wts)
                pw, tinv = res[0:CHUNK], tinv + res[CHUNK:]
            else:
                tinv = tinv + _mmp(_split(tinv, npart), wts)
            c.update(pw=pw, tinv=tinv)
        return stage

    def values(c):
        lv_mv = _mmp(_split(jnp.concatenate([c["lak"], c["mrk"]], axis=0), npart), bd(c["v"]))
        c.update(lv=lv_mv[0:CHUNK], mv=lv_mv[CHUNK:])

    def solve(c):
        aw = _mmp(_split(c["tinv"], npart), cat_parts([bd(c["at"]), bd(c["lv"])]))
        c.update(ahat=aw[:, 0:GROUP], wv=aw[:, GROUP:])

    def outputs(c):
        ro = _mmp(_split(c["mrb"], npart), cat_parts([bd(c["ahat"]), bd(c["wv"])]))
        c.update(rhat=(c["rt"].astype(F32) + ro[:, 0:GROUP]) * c["gc"], ointra=ro[:, GROUP:] + c["mv"])

    def transition(c):
        zero_rows = jnp.zeros((CHUNK, GROUP), BF16)
        zero_half = jnp.zeros((CHUNK, pair), BF16)
        zero_blk = jnp.zeros((pair, pair), F32)
        pms, qms = [], []
        for p in range(2):
            ps = slice(p * pair, (p + 1) * pair)
            top = jnp.concatenate([c["ahat"][:, ps], c["wv"][:, ps]], axis=1).astype(BF16)
            mid = jnp.concatenate([zero_half, c["v"][:, ps]], axis=1)
            rhs = jnp.concatenate([top, zero_rows, mid, zero_rows], axis=0)
            pq = _dot(c["bk"][ps, :], rhs, False)
            pblk = (pq[:, 0:pair] * bd2 + eye2) * c["gc"][:, ps]
            qblk = pq[:, pair:] * bd2
            pms.append(jnp.concatenate([pblk, zero_blk] if p == 0 else [zero_blk, pblk], axis=1))
            qms.append(jnp.concatenate([qblk, zero_blk] if p == 0 else [zero_blk, qblk], axis=1))
        c.update(pm=jnp.concatenate(pms, axis=0), qm=jnp.concatenate(qms, axis=0))

    def advance(c):
        rs = _mmp(_split(jnp.concatenate([c["rhat"], c["pm"]], axis=0), STATE_PARTS),
                  _split(c["st"], STATE_PARTS))
        c.update(o=rs[0:CHUNK] + c["ointra"], st_new=rs[CHUNK:] + c["qm"])

    def store(c):
        c["out"][c["bi"], :, c["sl"]] = c["o"]
        st_ref[c["bi"], c["d"], c["g"]] = c["st_new"]

    chains = []
    dirs = ((at0, rt0, bk0, v0, gc0, of_ref), (at1, rt1, bk1, v1, gc1, or_ref))
    for bi in range(at0.shape[0]):
        for d, refs in enumerate(dirs):
            for g in range(at0.shape[-1] // GROUP):
                chains.append(dict(bi=bi, d=d, g=g, sl=slice(g * GROUP, (g + 1) * GROUP),
                                   refs=refs[:5], out=refs[5]))
    stages = ([load, interactions] + [invert(s) for s in range(steps + 1)]
              + [values, solve, outputs, transition, advance, store])
    for stage in stages:
        for c in chains:
            stage(c)


def _scan_constants():
    t = np.arange(CHUNK)[:, None]
    lane = np.arange(GROUP)[None, :]
    s = lane % CHUNK
    strict = np.stack([s < t, s > t]).astype(np.float32)
    incl = np.stack([s <= t, s >= t]).astype(np.float32)
    idw = (s == t).astype(np.float32)
    hm = np.stack([np.broadcast_to((lane // RWKV_HEAD) == h, (CHUNK, GROUP)) for h in range(HEADS_PER_GROUP)])
    bdm = np.kron(np.eye(HEADS_PER_GROUP), np.ones((RWKV_HEAD, RWKV_HEAD)))
    return (jnp.asarray(hm.astype(np.float32), BF16), jnp.asarray(strict), jnp.asarray(incl), jnp.asarray(idw),
            jnp.asarray(bdm, BF16), jnp.eye(GROUP, dtype=F32))


def _rwkv_scan(at, rt, bk, v, gc, s0):
    b, _, t, db = at.shape
    nc = t // CHUNK
    consts = _scan_constants()
    full = lambda a: pl.BlockSpec(a.shape, lambda i: (0,) * a.ndim)
    fwd, rev = (lambda i: i), (lambda i: nc - 1 - i)
    dspec = lambda d, ci: pl.BlockSpec((b, None, CHUNK, db), lambda i: (0, d, ci(i), 0))
    kspec = lambda d, ci: pl.BlockSpec((b, None, None, db, 4 * CHUNK), lambda i: (0, d, ci(i), 0, 0))
    vspec = lambda ci: pl.BlockSpec((b, CHUNK, db), lambda i: (0, ci(i), 0))
    gspec = lambda d, ci: pl.BlockSpec((b, None, None, SUBLANES, db), lambda i: (0, d, ci(i), 0, 0))
    in_specs = ([dspec(0, fwd), dspec(0, fwd), kspec(0, fwd), vspec(fwd), gspec(0, fwd),
                 dspec(1, rev), dspec(1, rev), kspec(1, rev), vspec(rev), gspec(1, rev), full(s0)]
                + [full(a) for a in consts])
    return pl.pallas_call(
        _scan_kernel,
        grid=(nc,),
        in_specs=in_specs,
        out_specs=[vspec(fwd), vspec(rev), full(s0)],
        out_shape=[jax.ShapeDtypeStruct((b, t, db), F32), jax.ShapeDtypeStruct((b, t, db), F32),
                   jax.ShapeDtypeStruct(s0.shape, F32)],
        compiler_params=pltpu.CompilerParams(dimension_semantics=("arbitrary",),
                                             vmem_limit_bytes=VMEM_LIMIT),
        name="rwkv_scan",
    )(at, rt, bk, v, gc, at, rt, bk, v, gc, s0, *consts)


def _conv_kernel(p1_ref, p2_ref, w_ref, b_ref, o_ref, h_ref, *, stride):
    t, c = p1_ref.shape
    halo = (CONF_CONV_W // 2) * stride
    nsteps = t // CONV_ROWS
    zeros = jnp.zeros((halo, c), F32)
    h_ref[0:halo, :] = zeros
    h_ref[halo + t:halo + t + halo, :] = zeros

    def glu(j, carry):
        r0 = pl.multiple_of(j * CONV_ROWS, CONV_ROWS)
        h_ref[pl.ds(halo + r0, CONV_ROWS), :] = (p1_ref[pl.ds(r0, CONV_ROWS), :]
                                                  * _sigmoid(p2_ref[pl.ds(r0, CONV_ROWS), :]))
        return carry

    lax.fori_loop(0, nsteps, glu, 0)
    w = w_ref[...]
    bias = b_ref[...]

    def conv(j, carry):
        r0 = pl.multiple_of(j * CONV_ROWS, CONV_ROWS)
        acc = jnp.broadcast_to(bias, (CONV_ROWS, c))
        for tap in range(CONF_CONV_W):
            acc = acc + h_ref[pl.ds(r0 + tap * stride, CONV_ROWS), :] * w[tap:tap + 1, :]
        o_ref[pl.ds(r0, CONV_ROWS), :] = acc
        return carry

    lax.fori_loop(0, nsteps, conv, 0)


def _conformer_conv(proj, dw_w, dw_b, *, stride):
    b, t, n = proj.shape
    db = dw_w.shape[-1]
    lanes = 128
    assert t % CONV_ROWS == 0 and stride % SUBLANES == 0
    halo = (CONF_CONV_W // 2) * stride
    c1, c2 = db // lanes, 2 * db // lanes
    return pl.pallas_call(
        functools.partial(_conv_kernel, stride=stride),
        grid=(b, db // lanes),
        in_specs=[pl.BlockSpec((None, t, lanes), lambda bi, j: (bi, 0, c1 + j)),
                  pl.BlockSpec((None, t, lanes), lambda bi, j: (bi, 0, c2 + j)),
                  pl.BlockSpec((_round_up(CONF_CONV_W, SUBLANES), lanes), lambda bi, j: (0, j)),
                  pl.BlockSpec((1, lanes), lambda bi, j: (0, j))],
        out_specs=pl.BlockSpec((None, t, lanes), lambda bi, j: (bi, 0, j)),
        out_shape=jax.ShapeDtypeStruct((b, t, db), F32),
        scratch_shapes=[pltpu.VMEM((t + 2 * halo, lanes), F32)],
        compiler_params=pltpu.CompilerParams(dimension_semantics=("parallel", "parallel"),
                                             vmem_limit_bytes=VMEM_LIMIT),
        name="conformer_conv",
    )(proj, proj, _pad_to(dw_w, 0, _round_up(CONF_CONV_W, SUBLANES)), dw_b.reshape(1, db))


def _round_up(x, m):
    return (x + m - 1) // m * m


def _merge_kernel(x_ref, of_ref, or_ref, g_ref, bonus_ref, conv_ref, zc_ref, zd_ref, mod_ref,
                  lnxg_ref, lnxb_ref, clng_ref, clnb_ref, wout_ref, fg_ref, seg_ref, o_ref):
    d = x_ref.shape[-1]
    x = x_ref[...]
    mrow = _mod_row(mod_ref, False)
    o = of_ref[...] + or_ref[...]
    seg = seg_ref[...]
    dev = o - _mm(o, seg, 2, 1)
    var = _mm(dev * dev, seg, 2, 1)
    on = dev * lax.rsqrt(var + LNX_EPS) * lnxg_ref[...] + lnxb_ref[...]
    y_c = (on + bonus_ref[...]) * g_ref[...] * _silu(zc_ref[...])

    c = conv_ref[...]
    mean = jnp.mean(c, axis=-1, keepdims=True)
    cdev = c - mean
    cvar = jnp.mean(cdev * cdev, axis=-1, keepdims=True)
    ln = cdev * lax.rsqrt(cvar + CONF_LN_EPS) * clng_ref[...] + clnb_ref[...]
    y_d = _silu(ln) * _silu(zd_ref[...])

    y = _mm(jnp.concatenate([y_c, y_d], axis=-1), wout_ref[...])
    o_ref[...] = _rmsnorm(x + mrow[:, 2 * d:3 * d] * y, fg_ref[...])


def _merge(x, o_f, o_r, g, bonus, conv, proj, mod, p, final_g):
    b, t, d = x.shape
    db = d // 2
    tm = min(TILE_TOKENS, t)
    heads = db // RWKV_HEAD
    seg = jnp.asarray(np.kron(np.eye(heads), np.full((RWKV_HEAD, RWKV_HEAD), 1.0 / RWKV_HEAD)), BF16)
    consts = (mod, p["lnx_g"].reshape(1, db), p["lnx_b"].reshape(1, db), p["conf_ln_g"].reshape(1, db),
              p["conf_ln_b"].reshape(1, db), p["out_o"].astype(BF16), final_g.reshape(1, d), seg)
    full = lambda a: pl.BlockSpec(a.shape, lambda bi, i: (0,) * a.ndim)
    tok = pl.BlockSpec((None, tm, db), lambda bi, i: (bi, i, 0))
    col = lambda cblk: pl.BlockSpec((None, tm, db), lambda bi, i: (bi, i, cblk))
    return pl.pallas_call(
        _merge_kernel,
        grid=(b, t // tm),
        in_specs=[pl.BlockSpec((None, tm, d), lambda bi, i: (bi, i, 0)), tok, tok, tok, tok, tok,
                  col(0), col(3)] + [full(a) for a in consts],
        out_specs=pl.BlockSpec((None, tm, d), lambda bi, i: (bi, i, 0)),
        out_shape=jax.ShapeDtypeStruct((b, t, d), F32),
        compiler_params=pltpu.CompilerParams(dimension_semantics=("parallel", "parallel"),
                                             vmem_limit_bytes=VMEM_LIMIT),
        name="merge",
    )(x, o_f, o_r, g, bonus, conv, proj, proj, *consts)


def kernel(x, c, ctx, c_ctx, ada_w_e, ada_b_e, norm_e, in_e, out_e, pool_w, pool_scale, sconv_w,
           ada_w_o, ada_b_o, norm_o, in_o, out_o, rwkv_mu, w0, w1, w2, a0, a1, a2, g1, g2,
           k_k, k_a, r_k, lnx_g, lnx_b, conf_dw_w, conf_dw_b, conf_ln_g, conf_ln_b, final_g):
    b, t, d = x.shape
    db = d // 2
    assert b == 2
    cvec = _pad_to(jnp.concatenate([c, c_ctx[None, :]], axis=0), 0, SUBLANES)
    mod_e = _adaln(cvec, ada_w_e[0], ada_b_e[0])
    mod_o = _adaln(cvec, ada_w_o[0], ada_b_o[0])

    even = dict(norm_g=norm_e[0], w_in=in_e[0], w_out=out_e[0], pool_w=pool_w[0],
                pool_scale=pool_scale[0], sconv_w=sconv_w[0])
    x1 = _even_layer(x, mod_e, ctx=False, line=GRID_W, **even)
    xc1 = _even_layer(ctx, mod_e, ctx=True, line=ctx.shape[1], **even)

    p = dict(rwkv_mu=rwkv_mu[0], w0=w0[0], w1=w1[0], w2=w2[0], a0=a0[0], a1=a1[0], a2=a2[0],
             g1=g1[0], g2=g2[0], k_k=k_k[0], k_a=k_a[0], r_k=r_k[0], lnx_g=lnx_g[0], lnx_b=lnx_b[0],
             conf_ln_g=conf_ln_g[0], conf_ln_b=conf_ln_b[0], out_o=out_o[0])

    _, v_c, _, _, at_c, rt_c, bk_c, gc_c = _odd_front(xc1, mod_o, norm_o[0], in_o[0], p, ctx=True)
    zero_state = jnp.zeros((b, 2, db // GROUP, GROUP, GROUP), F32)
    _, _, ctx_state = _rwkv_scan(at_c, rt_c, bk_c, v_c, gc_c, zero_state)

    proj, v_l, g_l, bonus, at_l, rt_l, bk_l, gc_l = _odd_front(x1, mod_o, norm_o[0], in_o[0], p, ctx=False)
    o_f, o_r, _ = _rwkv_scan(at_l, rt_l, bk_l, v_l, gc_l, ctx_state)
    conv = _conformer_conv(proj, conf_dw_w[0], conf_dw_b[0], stride=GRID_W)
    return _merge(x1, o_f, o_r, g_l, bonus, conv, proj, mod_o, p, final_g)
```

```python
import functools

import numpy as np
import jax
import jax.numpy as jnp
from jax import lax
from jax.experimental import pallas as pl
from jax.experimental.pallas import tpu as pltpu

F32 = jnp.float32
BF16 = jnp.bfloat16

GRID_W = 64
POOL_WINDOWS = (2, 4, 8, 16)
SHORT_CONV_W = 3
CONF_CONV_W = 31
RWKV_HEAD = 64
NORM_EPS = 1e-6
LNX_EPS = 64e-5
CONF_LN_EPS = 1e-5
DECAY_SCALE = float(np.exp(-0.5))

CHUNK = 64
HEADS_PER_GROUP = 4
GROUP = HEADS_PER_GROUP * RWKV_HEAD
LORA_PAD = 128
SUBLANES = 8
VMEM_LIMIT = 52 * 1024 * 1024

TILE_TOKENS = 256
PREP_TOKENS = 128
CONV_ROWS = 128
MERGE_SUBTILES = 2


def _sigmoid(x):
    return 0.5 * jnp.tanh(0.5 * x) + 0.5


def _silu(x):
    return x * _sigmoid(x)


def _split(x, n):
    if x.dtype == BF16:
        return [x]
    parts, r = [], x
    for i in range(n):
        p = r.astype(BF16)
        parts.append(p)
        if i + 1 < n:
            r = r - p.astype(F32)
    return parts


def _dot(a, b, nt):
    dims = (((1,), (1,)), ((), ())) if nt else (((1,), (0,)), ((), ()))
    return lax.dot_general(a, b, dims, preferred_element_type=F32)


def _mmp(ap, bp, nt=False):
    order = max(len(ap), len(bp))
    acc = None
    for i in reversed(range(len(ap))):
        for j in reversed(range(len(bp))):
            if i + j < order:
                t = _dot(ap[i], bp[j], nt)
                acc = t if acc is None else acc + t
    return acc


def _mm(a, b, na=1, nb=1, nt=False):
    return _mmp(_split(a, na), _split(b, nb), nt)


def _rmsnorm(x, g):
    ms = jnp.mean(x * x, axis=-1, keepdims=True)
    return x * lax.rsqrt(ms + NORM_EPS) * g


def _modulated(x, g, mod_row, d):
    shift, scale = mod_row[:, 0:d], mod_row[:, d:2 * d]
    return _rmsnorm(x, g) * (1.0 + scale) + shift


def _mod_row(mod_ref, ctx):
    row = 2 if ctx else pl.program_id(0)
    return mod_ref[pl.ds(row, 1), :]


def _adaln_kernel(c_ref, w_ref, b_ref, o_ref):
    o_ref[...] = _mm(_silu(c_ref[...]), w_ref[...], 2, 2) + b_ref[...]


def _adaln(cvec, w, b):
    d, n = w.shape
    tn = 768
    return pl.pallas_call(
        _adaln_kernel,
        grid=(n // tn,),
        in_specs=[pl.BlockSpec((SUBLANES, d), lambda j: (0, 0)),
                  pl.BlockSpec((d, tn), lambda j: (0, j)),
                  pl.BlockSpec((1, tn), lambda j: (0, j))],
        out_specs=pl.BlockSpec((SUBLANES, tn), lambda j: (0, j)),
        out_shape=jax.ShapeDtypeStruct((SUBLANES, n), F32),
        compiler_params=pltpu.CompilerParams(dimension_semantics=("parallel",),
                                             vmem_limit_bytes=VMEM_LIMIT),
        name="adaln",
    )(cvec, w, b.reshape(1, n))


def _even_kernel(x_ref, mod_ref, g_ref, win_ref, wout_ref, poolw_ref, pscale_ref, sconv_ref,
                 band_ref, invc_ref, o_ref, *, ctx, line):
    tm, d = x_ref.shape
    ts = band_ref.shape[-1]
    db = d // 2
    grp = db // len(POOL_WINDOWS)
    mrow = _mod_row(mod_ref, ctx)

    def project(c):
        x = x_ref[c["rows"], :]
        c.update(x=x, proj=_mm(_modulated(x, g_ref[...], mrow, d), win_ref[...]))

    def pool(c):
        u_a = c["proj"][:, 0:db]
        pooled = []
        for i in range(len(POOL_WINDOWS)):
            ug = u_a[:, i * grp:(i + 1) * grp]
            pooled.append(_mm(band_ref[i], ug) * invc_ref[i] - ug)
        c.update(pooled=pooled)

    def mix(c):
        proj = c["proj"]
        z_a = proj[:, db:2 * db]
        v_b, g_b = proj[:, 2 * db:3 * db], proj[:, 3 * db:4 * db]
        g_c, z_b = proj[:, 4 * db:5 * db], proj[:, 5 * db:6 * db]
        ya = [_mm(c["pooled"][i], poolw_ref[i]) for i in range(len(POOL_WINDOWS))]
        y_a = jnp.concatenate(ya, axis=-1) * pscale_ref[...] * _silu(z_a)
        q = g_c * v_b
        pos = lax.broadcasted_iota(jnp.int32, q.shape, 0) % line
        q_prev = jnp.where(pos >= 1, pltpu.roll(q, 1, 0), 0.0)
        q_next = jnp.where(pos <= line - 2, pltpu.roll(q, ts - 1, 0), 0.0)
        w = sconv_ref[...]
        conv = q_prev * w[0:1, :] + q * w[1:2, :] + q_next * w[2:3, :]
        y_b = g_b * conv * _silu(z_b)
        c.update(y=jnp.concatenate([y_a, y_b], axis=-1))

    def emit(c):
        o_ref[c["rows"], :] = c["x"] + mrow[:, 2 * d:3 * d] * _mm(c["y"], wout_ref[...])

    subs = [dict(rows=slice(s * ts, (s + 1) * ts)) for s in range(tm // ts)]
    for stage in (project, pool, mix, emit):
        for c in subs:
            stage(c)


def _pool_constants(tm, line):
    pos = np.arange(tm) % line
    lid = np.arange(tm) // line
    bands, invs = [], []
    for w in POOL_WINDOWS:
        lo = np.clip(pos - w // 2, 0, line)
        hi = np.clip(pos - w // 2 + w, 0, line)
        same = lid[:, None] == lid[None, :]
        band = same & (pos[None, :] >= lo[:, None]) & (pos[None, :] < hi[:, None])
        bands.append(band.astype(np.float32))
        invs.append(np.broadcast_to((1.0 / (hi - lo))[:, None], (tm, 128)).astype(np.float32))
    return jnp.asarray(np.stack(bands), BF16), jnp.asarray(np.stack(invs), F32)


def _even_layer(x, mod, norm_g, w_in, w_out, pool_w, pool_scale, sconv_w, *, ctx, line):
    b, t, d = x.shape
    tm = min(TILE_TOKENS, t)
    assert t % tm == 0 and tm % line == 0
    db = d // 2
    ts = tm // 2 if (tm // 2) % line == 0 else tm
    band, invc = _pool_constants(ts, line)
    full = lambda a: pl.BlockSpec(a.shape, lambda bi, i: (0,) * a.ndim)
    args = (mod, norm_g.reshape(1, d), w_in.astype(BF16), w_out.astype(BF16), pool_w.astype(BF16),
            pool_scale.reshape(1, db), sconv_w, band, invc)
    return pl.pallas_call(
        functools.partial(_even_kernel, ctx=ctx, line=line),
        grid=(b, t // tm),
        in_specs=[pl.BlockSpec((None, tm, d), lambda bi, i: (bi, i, 0))] + [full(a) for a in args],
        out_specs=pl.BlockSpec((None, tm, d), lambda bi, i: (bi, i, 0)),
        out_shape=jax.ShapeDtypeStruct((b, t, d), F32),
        compiler_params=pltpu.CompilerParams(dimension_semantics=("parallel", "parallel"),
                                             vmem_limit_bytes=VMEM_LIMIT),
        name="even_ctx" if ctx else "even_latent",
    )(x, *args)


def _front_kernel(x_ref, xp_ref, xn_ref, mod_ref, ng_ref, win_ref,
                  mu_ref, w0_ref, a0_ref, w1_ref, w2_ref, a1_ref, a2_ref,
                  g1_ref, g2_ref, kk_ref, ka_ref, rk_ref, seg_ref, later_ref, whole_ref,
                  rest_ref, v_ref, g_ref, bonus_ref, at_ref, rt_ref, bk_ref, gc_ref, *, ctx):
    tm, d = x_ref.shape
    db = v_ref.shape[-1]
    ts = PREP_TOKENS
    te = ts + 2 * SUBLANES
    i, n = pl.program_id(1), pl.num_programs(1)
    mrow = _mod_row(mod_ref, ctx)
    h_prev = _modulated(xp_ref[...], ng_ref[...], mrow, d) * jnp.where(i > 0, 1.0, 0.0)
    h_next = _modulated(xn_ref[...], ng_ref[...], mrow, d) * jnp.where(i < n - 1, 1.0, 0.0)
    h = _modulated(x_ref[...], ng_ref[...], mrow, d)
    h_b = h.astype(BF16)
    h_ext = jnp.concatenate([h_prev, h, h_next], axis=0).astype(BF16)
    mu = mu_ref[...]
    first_chunk = lax.broadcasted_iota(jnp.int32, (db, ts), 1) < CHUNK

    def rest(c, j):
        rest_ref[c["rows"], j * db:(j + 1) * db] = _dot(h_b[c["rows"]], win_ref[:, (4 + j) * db:(5 + j) * db], False)

    def project(c):
        c.update(z_ext=_dot(h_ext[c["s"] * ts:c["s"] * ts + te], win_ref[:, 0:4 * db], False))

    def shift(c):
        z_ext = c["z_ext"]
        nbr = (pltpu.roll(z_ext, 1, 0) + pltpu.roll(z_ext, te - 1, 0))[SUBLANES:ts + SUBLANES]
        z = z_ext[SUBLANES:ts + SUBLANES]
        dz = 0.5 * nbr - z
        u, du = z[:, 0:db], dz[:, 0:db]
        v = z[:, 3 * db:4 * db] + dz[:, 3 * db:4 * db] * mu[2:3, :]
        c.update(r=z[:, db:2 * db] + dz[:, db:2 * db] * mu[0:1, :],
                 k=z[:, 2 * db:3 * db] + dz[:, 2 * db:3 * db] * mu[1:2, :], v=v,
                 uw=u + du * mu[3:4, :], ua=u + du * mu[4:5, :], ug=u + du * mu[5:6, :])
        v_ref[c["rows"], :] = v.astype(BF16)
        rest(c, 0)

    def loras(c):
        g_ref[c["rows"], :] = _mm(_sigmoid(_mm(c["ug"], g1_ref[...])), g2_ref[...])
        kk = c["k"] * kk_ref[...]
        inv_norm = lax.rsqrt(jnp.maximum(_mm(kk * kk, seg_ref[...], 2, 1), 1e-24))
        c.update(lw_pre=w0_ref[...] + _mm(jnp.tanh(_mm(c["uw"], w1_ref[...])), w2_ref[...]),
                 a_pre=a0_ref[...] + _mm(_mm(c["ua"], a1_ref[...]), a2_ref[...]),
                 kk=kk * inv_norm, ksum=None)
        rest(c, 1)

    def direction(dr):
        def stage(c):
            rows, k, kk = c["rows"], c["k"], c["kk"]
            lw = -DECAY_SCALE * _sigmoid(c["lw_pre"][:, dr * db:(dr + 1) * db])
            a = _sigmoid(c["a_pre"][:, dr * db:(dr + 1) * db])
            k_d = k * (1.0 + (a - 1.0) * ka_ref[...])
            c["ksum"] = k_d if c["ksum"] is None else c["ksum"] + k_d
            lw_parts = _split(lw, 2)
            rel = -_mmp([later_ref[dr]], lw_parts)
            at_ref[dr, rows, :] = (-kk * jnp.exp(rel - lw)).astype(BF16)
            rt_ref[dr, rows, :] = (c["r"] * jnp.exp(rel)).astype(BF16)
            gout = jnp.exp(-rel)
            bh_t = (kk * a * gout).T
            kh_t = (k_d * gout).T
            bh_r, kh_r = pltpu.roll(bh_t, CHUNK, 1), pltpu.roll(kh_t, CHUNK, 1)
            gam = jnp.exp(_mmp([whole_ref[...]], lw_parts))
            for q in range(ts // CHUNK):
                own = first_chunk if q == 0 else jnp.logical_not(first_chunk)
                chunk = c["s"] * (ts // CHUNK) + q
                bk_ref[dr, chunk] = jnp.concatenate([jnp.where(own, bh_t, bh_r), jnp.where(own, kh_t, kh_r)],
                                                    axis=1).astype(BF16)
                gc_ref[dr, chunk] = gam[q * SUBLANES:(q + 1) * SUBLANES, :]
            rest(c, 2 + dr)
        return stage

    def bonus(c):
        bonus_ref[c["rows"], :] = _mm(c["r"] * c["ksum"] * rk_ref[...], seg_ref[...], 2, 1) * c["v"]

    subs = [dict(s=s, rows=slice(s * ts, (s + 1) * ts)) for s in range(tm // ts)]
    for stage in (project, shift, loras, direction(0), direction(1), bonus):
        for c in subs:
            stage(c)


def _prep_constants(tm):
    t = np.arange(tm)
    same = (t[:, None] // CHUNK) == (t[None, :] // CHUNK)
    later = np.stack([same & (t[None, :] > t[:, None]), same & (t[None, :] < t[:, None])]).astype(np.float32)
    whole = (np.arange(tm // CHUNK * SUBLANES)[:, None] // SUBLANES) == (t[None, :] // CHUNK)
    return jnp.asarray(later, BF16), jnp.asarray(whole.astype(np.float32), BF16)


def _pad_to(a, axis, size):
    pad = [(0, 0)] * a.ndim
    pad[axis] = (0, size - a.shape[axis])
    return jnp.pad(a, pad)


def _odd_front(x, mod, norm_g, w_in, p, *, ctx):
    b, t, d = x.shape
    db = p["k_k"].shape[-1]
    tm = min(TILE_TOKENS, t)
    assert t % tm == 0 and tm % PREP_TOKENS == 0 and PREP_TOKENS == 2 * CHUNK
    nc = t // CHUNK
    nb = t // SUBLANES
    per = tm // SUBLANES
    later, whole = _prep_constants(PREP_TOKENS)
    heads = db // RWKV_HEAD
    seg = jnp.asarray(np.kron(np.eye(heads), np.ones((RWKV_HEAD, RWKV_HEAD))), BF16)

    def lora_pair(w_a, w_b):
        rank = w_a.shape[-1]
        down = _pad_to(jnp.concatenate([w_a[0], w_a[1]], axis=-1), 1, LORA_PAD)
        up = jnp.zeros((LORA_PAD, 2 * db), F32)
        up = up.at[0:rank, 0:db].set(w_b[0]).at[rank:2 * rank, db:2 * db].set(w_b[1])
        return down.astype(BF16), up.astype(BF16)

    w1c, w2c = lora_pair(p["w1"], p["w2"])
    a1c, a2c = lora_pair(p["a1"], p["a2"])
    consts = (mod, norm_g.reshape(1, d), w_in.astype(BF16),
              _pad_to(p["rwkv_mu"], 0, SUBLANES), p["w0"].reshape(1, 2 * db), p["a0"].reshape(1, 2 * db),
              w1c, w2c, a1c, a2c,
              _pad_to(p["g1"], 1, LORA_PAD).astype(BF16), _pad_to(p["g2"], 0, LORA_PAD).astype(BF16),
              p["k_k"].reshape(1, db), p["k_a"].reshape(1, db), p["r_k"].reshape(1, db), seg, later, whole)
    full = lambda a: pl.BlockSpec(a.shape, lambda bi, i: (0,) * a.ndim)
    tok = pl.BlockSpec((None, tm, db), lambda bi, i: (bi, i, 0))
    dirtok = pl.BlockSpec((None, 2, tm, db), lambda bi, i: (bi, 0, i, 0))
    tok_shape = jax.ShapeDtypeStruct((b, t, db), F32)
    dir_shape = jax.ShapeDtypeStruct((b, 2, t, db), BF16)
    nrest = w_in.shape[1] - 4 * db
    return pl.pallas_call(
        functools.partial(_front_kernel, ctx=ctx),
        grid=(b, t // tm),
        in_specs=[pl.BlockSpec((None, tm, d), lambda bi, i: (bi, i, 0)),
                  pl.BlockSpec((None, SUBLANES, d), lambda bi, i: (bi, jnp.maximum(i * per - 1, 0), 0)),
                  pl.BlockSpec((None, SUBLANES, d), lambda bi, i: (bi, jnp.minimum((i + 1) * per, nb - 1), 0))]
        + [full(a) for a in consts],
        out_specs=[pl.BlockSpec((None, tm, nrest), lambda bi, i: (bi, i, 0)), tok, tok, tok, dirtok, dirtok,
                   pl.BlockSpec((None, 2, tm // CHUNK, db, 4 * CHUNK), lambda bi, i: (bi, 0, i, 0, 0)),
                   pl.BlockSpec((None, 2, tm // CHUNK, SUBLANES, db), lambda bi, i: (bi, 0, i, 0, 0))],
        out_shape=[jax.ShapeDtypeStruct((b, t, nrest), F32),
                   jax.ShapeDtypeStruct((b, t, db), BF16), tok_shape, tok_shape, dir_shape, dir_shape,
                   jax.ShapeDtypeStruct((b, 2, nc, db, 4 * CHUNK), BF16),
                   jax.ShapeDtypeStruct((b, 2, nc, SUBLANES, db), F32)],
        compiler_params=pltpu.CompilerParams(dimension_semantics=("parallel", "parallel"),
                                             vmem_limit_bytes=VMEM_LIMIT),
        name="front_ctx" if ctx else "front_latent",
    )(x, x, x, *consts)


SCAN_PARTS = 1
SCAN_CHUNKS_PER_STEP = 2


def _scan_kernel(at0, rt0, bk0, v0, gc0, at1, rt1, bk1, v1, gc1, s0_ref,
                 hm_ref, ms_ref, mi_ref, idw_ref, bdm_ref, eye_ref,
                 of_ref, or_ref, st_ref):
    @pl.when(pl.program_id(0) == 0)
    def _():
        st_ref[...] = s0_ref[...]

    npart = SCAN_PARTS
    pair = GROUP // 2
    bdm_b = bdm_ref[...]
    bd2 = bdm_b[0:pair, 0:pair].astype(F32)
    eye2 = eye_ref[0:pair, 0:pair]

    def bd(x):
        return [jnp.concatenate([part * hm_ref[h] for h in range(HEADS_PER_GROUP)], axis=0)
                for part in _split(x, npart)]

    def cat_parts(plists):
        return [jnp.concatenate(ps, axis=-1) for ps in zip(*plists)]

    steps = int(np.log2(CHUNK)) - 1

    def load(c):
        at_r, rt_r, bk_r, v_r, gc_r = c["refs"]
        bi, g, sl, q, rows = c["bi"], c["g"], c["sl"], c["q"], c["rows"]
        c.update(at=at_r[bi, rows, sl], rt=rt_r[bi, rows, sl], v=v_r[bi, rows, sl], gc=gc_r[bi, q, 0:1, sl],
                 bk=bk_r[bi, q, g * GROUP:(g + 1) * GROUP, :])

    def interactions(c):
        ms, mi = ms_ref[c["d"]], mi_ref[c["d"]]
        bk = c["bk"]
        wts = jnp.concatenate([jnp.concatenate([bk[:, 0:pair]] * 2, axis=1) * bdm_b,
                               jnp.concatenate([bk[:, pair:]] * 2, axis=1) * bdm_b], axis=1)
        gmat = _dot(jnp.concatenate([c["at"], c["rt"]], axis=0), wts, False)
        lab = gmat[0:CHUNK, 0:GROUP] * ms
        c.update(lak=gmat[0:CHUNK, GROUP:] * ms, mrb=gmat[CHUNK:, 0:GROUP] * mi,
                 mrk=gmat[CHUNK:, GROUP:] * mi, pw=lab, tinv=idw_ref[...] + lab)

    def invert(s):
        def stage(c):
            pw, tinv = c["pw"], c["tinv"]
            wts = bd(pw)
            if s == 0:
                pw = _mmp(_split(pw, npart), wts)
            elif s < steps:
                res = _mmp(_split(jnp.concatenate([pw, tinv], axis=0), npart), wts)
                pw, tinv = res[0:CHUNK], tinv + res[CHUNK:]
            else:
                tinv = tinv + _mmp(_split(tinv, npart), wts)
            c.update(pw=pw, tinv=tinv)
        return stage

    def values(c):
        lv_mv = _mmp(_split(jnp.concatenate([c["lak"], c["mrk"]], axis=0), npart), bd(c["v"]))
        c.update(lv=lv_mv[0:CHUNK], mv=lv_mv[CHUNK:])

    def solve(c):
        aw = _mmp(_split(c["tinv"], npart), cat_parts([bd(c["at"]), bd(c["lv"])]))
        c.update(ahat=aw[:, 0:GROUP], wv=aw[:, GROUP:])

    def outputs(c):
        ro = _mmp(_split(c["mrb"], npart), cat_parts([bd(c["ahat"]), bd(c["wv"])]))
        c.update(rhat=(c["rt"].astype(F32) + ro[:, 0:GROUP]) * c["gc"], ointra=ro[:, GROUP:] + c["mv"])

    def transition(c):
        zero_rows = jnp.zeros((CHUNK, GROUP), BF16)
        zero_half = jnp.zeros((CHUNK, pair), BF16)
        zero_blk = jnp.zeros((pair, pair), F32)
        pms, qms = [], []
        for p in range(2):
            ps = slice(p * pair, (p + 1) * pair)
            top = jnp.concatenate([c["ahat"][:, ps], c["wv"][:, ps]], axis=1).astype(BF16)
            mid = jnp.concatenate([zero_half, c["v"][:, ps]], axis=1)
            rhs = jnp.concatenate([top, zero_rows, mid, zero_rows], axis=0)
            pq = _dot(c["bk"][ps, :], rhs, False)
            pblk = (pq[:, 0:pair] * bd2 + eye2) * c["gc"][:, ps]
            qblk = pq[:, pair:] * bd2
            pms.append(jnp.concatenate([pblk, zero_blk] if p == 0 else [zero_blk, pblk], axis=1))
            qms.append(jnp.concatenate([qblk, zero_blk] if p == 0 else [zero_blk, qblk], axis=1))
        c.update(pm=jnp.concatenate(pms, axis=0), qm=jnp.concatenate(qms, axis=0))

    def advance(c, st):
        rs = _dot(jnp.concatenate([c["rhat"], c["pm"]], axis=0).astype(BF16), st.astype(BF16), False)
        c["out"][c["bi"], c["rows"], c["sl"]] = rs[0:CHUNK] + c["ointra"]
        return rs[CHUNK:] + c["qm"]

    nq = at0.shape[1] // CHUNK
    dirs = ((at0, rt0, bk0, v0, gc0, of_ref), (at1, rt1, bk1, v1, gc1, or_ref))
    groups = {}
    for bi in range(at0.shape[0]):
        for d, refs in enumerate(dirs):
            for g in range(at0.shape[-1] // GROUP):
                order = range(nq) if d == 0 else reversed(range(nq))
                groups[(bi, d, g)] = [
                    dict(q=q, rows=slice(q * CHUNK, (q + 1) * CHUNK), bi=bi, d=d, g=g,
                         sl=slice(g * GROUP, (g + 1) * GROUP), refs=refs[:5], out=refs[5]) for q in order]
    chains = [c for k in range(nq) for cs in groups.values() for c in [cs[k]]]
    stages = ([load, interactions] + [invert(s) for s in range(steps + 1)]
              + [values, solve, outputs, transition])
    for stage in stages:
        for c in chains:
            stage(c)
    states = {key: st_ref[key[0], key[1], key[2]] for key in groups}
    for k in range(nq):
        for key, cs in groups.items():
            states[key] = advance(cs[k], states[key])
    for key in groups:
        st_ref[key[0], key[1], key[2]] = states[key]


def _scan_constants():
    t = np.arange(CHUNK)[:, None]
    lane = np.arange(GROUP)[None, :]
    s = lane % CHUNK
    strict = np.stack([s < t, s > t]).astype(np.float32)
    incl = np.stack([s <= t, s >= t]).astype(np.float32)
    idw = (s == t).astype(np.float32)
    hm = np.stack([np.broadcast_to((lane // RWKV_HEAD) == h, (CHUNK, GROUP)) for h in range(HEADS_PER_GROUP)])
    bdm = np.kron(np.eye(HEADS_PER_GROUP), np.ones((RWKV_HEAD, RWKV_HEAD)))
    return (jnp.asarray(hm.astype(np.float32), BF16), jnp.asarray(strict), jnp.asarray(incl), jnp.asarray(idw),
            jnp.asarray(bdm, BF16), jnp.eye(GROUP, dtype=F32))


def _rwkv_scan(at, rt, bk, v, gc, s0):
    b, _, t, db = at.shape
    nc = t // CHUNK
    consts = _scan_constants()
    full = lambda a: pl.BlockSpec(a.shape, lambda i: (0,) * a.ndim)
    nq = min(SCAN_CHUNKS_PER_STEP, nc)
    ns = nc // nq
    assert nc % nq == 0
    fwd, rev = (lambda i: i), (lambda i: ns - 1 - i)
    dspec = lambda d, ci: pl.BlockSpec((b, None, nq * CHUNK, db), lambda i: (0, d, ci(i), 0))
    kspec = lambda d, ci: pl.BlockSpec((b, None, nq, db, 4 * CHUNK), lambda i: (0, d, ci(i), 0, 0))
    vspec = lambda ci: pl.BlockSpec((b, nq * CHUNK, db), lambda i: (0, ci(i), 0))
    gspec = lambda d, ci: pl.BlockSpec((b, None, nq, SUBLANES, db), lambda i: (0, d, ci(i), 0, 0))
    in_specs = ([dspec(0, fwd), dspec(0, fwd), kspec(0, fwd), vspec(fwd), gspec(0, fwd),
                 dspec(1, rev), dspec(1, rev), kspec(1, rev), vspec(rev), gspec(1, rev), full(s0)]
                + [full(a) for a in consts])
    return pl.pallas_call(
        _scan_kernel,
        grid=(ns,),
        in_specs=in_specs,
        out_specs=[vspec(fwd), vspec(rev), full(s0)],
        out_shape=[jax.ShapeDtypeStruct((b, t, db), F32), jax.ShapeDtypeStruct((b, t, db), F32),
                   jax.ShapeDtypeStruct(s0.shape, F32)],
        compiler_params=pltpu.CompilerParams(dimension_semantics=("arbitrary",),
                                             vmem_limit_bytes=VMEM_LIMIT),
        name="rwkv_scan",
    )(at, rt, bk, v, gc, at, rt, bk, v, gc, s0, *consts)


def _conv_kernel(p1_ref, p2_ref, w_ref, b_ref, o_ref, h_ref, *, stride):
    t, c = p1_ref.shape
    halo = (CONF_CONV_W // 2) * stride
    nsteps = t // CONV_ROWS
    zeros = jnp.zeros((halo, c), F32)
    h_ref[0:halo, :] = zeros
    h_ref[halo + t:halo + t + halo, :] = zeros

    def glu(j, carry):
        r0 = pl.multiple_of(j * CONV_ROWS, CONV_ROWS)
        h_ref[pl.ds(halo + r0, CONV_ROWS), :] = (p1_ref[pl.ds(r0, CONV_ROWS), :]
                                                  * _sigmoid(p2_ref[pl.ds(r0, CONV_ROWS), :]))
        return carry

    lax.fori_loop(0, nsteps, glu, 0)
    w = w_ref[...]
    bias = b_ref[...]

    def conv(j, carry):
        r0 = pl.multiple_of(j * CONV_ROWS, CONV_ROWS)
        acc = jnp.broadcast_to(bias, (CONV_ROWS, c))
        for tap in range(CONF_CONV_W):
            acc = acc + h_ref[pl.ds(r0 + tap * stride, CONV_ROWS), :] * w[tap:tap + 1, :]
        o_ref[pl.ds(r0, CONV_ROWS), :] = acc
        return carry

    lax.fori_loop(0, nsteps, conv, 0)


def _conformer_conv(proj, dw_w, dw_b, *, stride):
    b, t, n = proj.shape
    db = dw_w.shape[-1]
    lanes = 128
    assert t % CONV_ROWS == 0 and stride % SUBLANES == 0
    halo = (CONF_CONV_W // 2) * stride
    c1, c2 = db // lanes, 2 * db // lanes
    return pl.pallas_call(
        functools.partial(_conv_kernel, stride=stride),
        grid=(b, db // lanes),
        in_specs=[pl.BlockSpec((None, t, lanes), lambda bi, j: (bi, 0, c1 + j)),
                  pl.BlockSpec((None, t, lanes), lambda bi, j: (bi, 0, c2 + j)),
                  pl.BlockSpec((_round_up(CONF_CONV_W, SUBLANES), lanes), lambda bi, j: (0, j)),
                  pl.BlockSpec((1, lanes), lambda bi, j: (0, j))],
        out_specs=pl.BlockSpec((None, t, lanes), lambda bi, j: (bi, 0, j)),
        out_shape=jax.ShapeDtypeStruct((b, t, db), F32),
        scratch_shapes=[pltpu.VMEM((t + 2 * halo, lanes), F32)],
        compiler_params=pltpu.CompilerParams(dimension_semantics=("parallel", "parallel"),
                                             vmem_limit_bytes=VMEM_LIMIT),
        name="conformer_conv",
    )(proj, proj, _pad_to(dw_w, 0, _round_up(CONF_CONV_W, SUBLANES)), dw_b.reshape(1, db))


def _round_up(x, m):
    return (x + m - 1) // m * m


def _merge_kernel(x_ref, of_ref, or_ref, g_ref, bonus_ref, conv_ref, zc_ref, zd_ref, mod_ref,
                  lnxg_ref, lnxb_ref, clng_ref, clnb_ref, wout_ref, fg_ref, seg_ref, o_ref):
    tm, d = x_ref.shape
    mrow = _mod_row(mod_ref, False)
    seg = seg_ref[...]

    def centre(c):
        o = of_ref[c["rows"], :] + or_ref[c["rows"], :]
        c.update(dev=o - _mm(o, seg, 2, 1))

    def normalise(c):
        rows, dev = c["rows"], c["dev"]
        var = _mm(dev * dev, seg, 2, 1)
        on = dev * lax.rsqrt(var + LNX_EPS) * lnxg_ref[...] + lnxb_ref[...]
        y_c = (on + bonus_ref[rows, :]) * g_ref[rows, :] * _silu(zc_ref[rows, :])
        cv = conv_ref[rows, :]
        cdev = cv - jnp.mean(cv, axis=-1, keepdims=True)
        cvar = jnp.mean(cdev * cdev, axis=-1, keepdims=True)
        ln = cdev * lax.rsqrt(cvar + CONF_LN_EPS) * clng_ref[...] + clnb_ref[...]
        y_d = _silu(ln) * _silu(zd_ref[rows, :])
        c.update(y=jnp.concatenate([y_c, y_d], axis=-1))

    def emit(c):
        y = _mm(c["y"], wout_ref[...])
        o_ref[c["rows"], :] = _rmsnorm(x_ref[c["rows"], :] + mrow[:, 2 * d:3 * d] * y, fg_ref[...])

    ts = tm // MERGE_SUBTILES
    subs = [dict(rows=slice(s * ts, (s + 1) * ts)) for s in range(MERGE_SUBTILES)]
    for stage in (centre, normalise, emit):
        for c in subs:
            stage(c)


def _merge(x, o_f, o_r, g, bonus, conv, proj, mod, p, final_g):
    b, t, d = x.shape
    db = d // 2
    tm = min(TILE_TOKENS, t)
    heads = db // RWKV_HEAD
    seg = jnp.asarray(np.kron(np.eye(heads), np.full((RWKV_HEAD, RWKV_HEAD), 1.0 / RWKV_HEAD)), BF16)
    consts = (mod, p["lnx_g"].reshape(1, db), p["lnx_b"].reshape(1, db), p["conf_ln_g"].reshape(1, db),
              p["conf_ln_b"].reshape(1, db), p["out_o"].astype(BF16), final_g.reshape(1, d), seg)
    full = lambda a: pl.BlockSpec(a.shape, lambda bi, i: (0,) * a.ndim)
    tok = pl.BlockSpec((None, tm, db), lambda bi, i: (bi, i, 0))
    col = lambda cblk: pl.BlockSpec((None, tm, db), lambda bi, i: (bi, i, cblk))
    return pl.pallas_call(
        _merge_kernel,
        grid=(b, t // tm),
        in_specs=[pl.BlockSpec((None, tm, d), lambda bi, i: (bi, i, 0)), tok, tok, tok, tok, tok,
                  col(0), col(3)] + [full(a) for a in consts],
        out_specs=pl.BlockSpec((None, tm, d), lambda bi, i: (bi, i, 0)),
        out_shape=jax.ShapeDtypeStruct((b, t, d), F32),
        compiler_params=pltpu.CompilerParams(dimension_semantics=("parallel", "parallel"),
                                             vmem_limit_bytes=VMEM_LIMIT),
        name="merge",
    )(x, o_f, o_r, g, bonus, conv, proj, proj, *consts)


def kernel(x, c, ctx, c_ctx, ada_w_e, ada_b_e, norm_e, in_e, out_e, pool_w, pool_scale, sconv_w,
           ada_w_o, ada_b_o, norm_o, in_o, out_o, rwkv_mu, w0, w1, w2, a0, a1, a2, g1, g2,
           k_k, k_a, r_k, lnx_g, lnx_b, conf_dw_w, conf_dw_b, conf_ln_g, conf_ln_b, final_g):
    b, t, d = x.shape
    db = d // 2
    assert b == 2
    cvec = _pad_to(jnp.concatenate([c, c_ctx[None, :]], axis=0), 0, SUBLANES)
    mod_e = _adaln(cvec, ada_w_e[0], ada_b_e[0])
    mod_o = _adaln(cvec, ada_w_o[0], ada_b_o[0])

    even = dict(norm_g=norm_e[0], w_in=in_e[0], w_out=out_e[0], pool_w=pool_w[0],
                pool_scale=pool_scale[0], sconv_w=sconv_w[0])
    x1 = _even_layer(x, mod_e, ctx=False, line=GRID_W, **even)
    xc1 = _even_layer(ctx, mod_e, ctx=True, line=ctx.shape[1], **even)

    p = dict(rwkv_mu=rwkv_mu[0], w0=w0[0], w1=w1[0], w2=w2[0], a0=a0[0], a1=a1[0], a2=a2[0],
             g1=g1[0], g2=g2[0], k_k=k_k[0], k_a=k_a[0], r_k=r_k[0], lnx_g=lnx_g[0], lnx_b=lnx_b[0],
             conf_ln_g=conf_ln_g[0], conf_ln_b=conf_ln_b[0], out_o=out_o[0])

    _, v_c, _, _, at_c, rt_c, bk_c, gc_c = _odd_front(xc1, mod_o, norm_o[0], in_o[0], p, ctx=True)
    zero_state = jnp.zeros((b, 2, db // GROUP, GROUP, GROUP), F32)
    _, _, ctx_state = _rwkv_scan(at_c, rt_c, bk_c, v_c, gc_c, zero_state)

    proj, v_l, g_l, bonus, at_l, rt_l, bk_l, gc_l = _odd_front(x1, mod_o, norm_o[0], in_o[0], p, ctx=False)
    o_f, o_r, _ = _rwkv_scan(at_l, rt_l, bk_l, v_l, gc_l, ctx_state)
    conv = _conformer_conv(proj, conf_dw_w[0], conf_dw_b[0], stride=GRID_W)
    return _merge(x1, o_f, o_r, g_l, bonus, conv, proj, mod_o, p, final_g)
```

```python
import functools

import numpy as np
import jax
import jax.numpy as jnp
from jax import lax
from jax.experimental import pallas as pl
from jax.experimental.pallas import tpu as pltpu

F32 = jnp.float32
BF16 = jnp.bfloat16

GRID_W = 64
POOL_WINDOWS = (2, 4, 8, 16)
SHORT_CONV_W = 3
CONF_CONV_W = 31
RWKV_HEAD = 64
NORM_EPS = 1e-6
LNX_EPS = 64e-5
CONF_LN_EPS = 1e-5
DECAY_SCALE = float(np.exp(-0.5))

CHUNK = 64
HEADS_PER_GROUP = 4
GROUP = HEADS_PER_GROUP * RWKV_HEAD
LORA_PAD = 128
SUBLANES = 8
VMEM_LIMIT = 52 * 1024 * 1024

TILE_TOKENS = 256
PREP_TOKENS = 128
CONV_ROWS = 128
MERGE_SUBTILES = 2


def _sigmoid(x):
    return 0.5 * jnp.tanh(0.5 * x) + 0.5


def _silu(x):
    return x * _sigmoid(x)


def _split(x, n):
    if x.dtype == BF16:
        return [x]
    parts, r = [], x
    for i in range(n):
        p = r.astype(BF16)
        parts.append(p)
        if i + 1 < n:
            r = r - p.astype(F32)
    return parts


def _dot(a, b, nt):
    dims = (((1,), (1,)), ((), ())) if nt else (((1,), (0,)), ((), ()))
    return lax.dot_general(a, b, dims, preferred_element_type=F32)


def _mmp(ap, bp, nt=False):
    order = max(len(ap), len(bp))
    acc = None
    for i in reversed(range(len(ap))):
        for j in reversed(range(len(bp))):
            if i + j < order:
                t = _dot(ap[i], bp[j], nt)
                acc = t if acc is None else acc + t
    return acc


def _mm(a, b, na=1, nb=1, nt=False):
    return _mmp(_split(a, na), _split(b, nb), nt)


def _rmsnorm(x, g):
    ms = jnp.mean(x * x, axis=-1, keepdims=True)
    return x * lax.rsqrt(ms + NORM_EPS) * g


def _modulated(x, g, mod_row, d):
    shift, scale = mod_row[:, 0:d], mod_row[:, d:2 * d]
    return _rmsnorm(x, g) * (1.0 + scale) + shift


def _mod_row(mod_ref, ctx):
    row = 2 if ctx else pl.program_id(0)
    return mod_ref[pl.ds(row, 1), :]


def _adaln_kernel(c_ref, w_ref, b_ref, o_ref):
    o_ref[...] = _mm(_silu(c_ref[...]), w_ref[...], 2, 2) + b_ref[...]


def _adaln(cvec, w, b):
    d, n = w.shape
    tn = 768
    return pl.pallas_call(
        _adaln_kernel,
        grid=(n // tn,),
        in_specs=[pl.BlockSpec((SUBLANES, d), lambda j: (0, 0)),
                  pl.BlockSpec((d, tn), lambda j: (0, j)),
                  pl.BlockSpec((1, tn), lambda j: (0, j))],
        out_specs=pl.BlockSpec((SUBLANES, tn), lambda j: (0, j)),
        out_shape=jax.ShapeDtypeStruct((SUBLANES, n), F32),
        compiler_params=pltpu.CompilerParams(dimension_semantics=("parallel",),
                                             vmem_limit_bytes=VMEM_LIMIT),
        name="adaln",
    )(cvec, w, b.reshape(1, n))


def _even_kernel(x_ref, mod_ref, g_ref, win_ref, wout_ref, poolw_ref, pscale_ref, sconv_ref,
                 band_ref, invc_ref, o_ref, *, ctx, line):
    tm, d = x_ref.shape
    ts = band_ref.shape[-1]
    db = d // 2
    grp = db // len(POOL_WINDOWS)
    mrow = _mod_row(mod_ref, ctx)

    def project(c):
        x = x_ref[c["rows"], :]
        c.update(x=x, proj=_mm(_modulated(x, g_ref[...], mrow, d), win_ref[...]))

    def pool(c):
        u_a = c["proj"][:, 0:db]
        pooled = []
        for i in range(len(POOL_WINDOWS)):
            ug = u_a[:, i * grp:(i + 1) * grp]
            pooled.append(_mm(band_ref[i], ug) * invc_ref[i] - ug)
        c.update(pooled=pooled)

    def mix(c):
        proj = c["proj"]
        z_a = proj[:, db:2 * db]
        v_b, g_b = proj[:, 2 * db:3 * db], proj[:, 3 * db:4 * db]
        g_c, z_b = proj[:, 4 * db:5 * db], proj[:, 5 * db:6 * db]
        ya = [_mm(c["pooled"][i], poolw_ref[i]) for i in range(len(POOL_WINDOWS))]
        y_a = jnp.concatenate(ya, axis=-1) * pscale_ref[...] * _silu(z_a)
        q = g_c * v_b
        pos = lax.broadcasted_iota(jnp.int32, q.shape, 0) % line
        q_prev = jnp.where(pos >= 1, pltpu.roll(q, 1, 0), 0.0)
        q_next = jnp.where(pos <= line - 2, pltpu.roll(q, ts - 1, 0), 0.0)
        w = sconv_ref[...]
        conv = q_prev * w[0:1, :] + q * w[1:2, :] + q_next * w[2:3, :]
        y_b = g_b * conv * _silu(z_b)
        c.update(y=jnp.concatenate([y_a, y_b], axis=-1))

    def emit(c):
        o_ref[c["rows"], :] = c["x"] + mrow[:, 2 * d:3 * d] * _mm(c["y"], wout_ref[...])

    subs = [dict(rows=slice(s * ts, (s + 1) * ts)) for s in range(tm // ts)]
    for stage in (project, pool, mix, emit):
        for c in subs:
            stage(c)


def _pool_constants(tm, line):
    pos = np.arange(tm) % line
    lid = np.arange(tm) // line
    bands, invs = [], []
    for w in POOL_WINDOWS:
        lo = np.clip(pos - w // 2, 0, line)
        hi = np.clip(pos - w // 2 + w, 0, line)
        same = lid[:, None] == lid[None, :]
        band = same & (pos[None, :] >= lo[:, None]) & (pos[None, :] < hi[:, None])
        bands.append(band.astype(np.float32))
        invs.append(np.broadcast_to((1.0 / (hi - lo))[:, None], (tm, 128)).astype(np.float32))
    return jnp.asarray(np.stack(bands), BF16), jnp.asarray(np.stack(invs), F32)


def _even_layer(x, mod, norm_g, w_in, w_out, pool_w, pool_scale, sconv_w, *, ctx, line):
    b, t, d = x.shape
    tm = min(2 * TILE_TOKENS, t)
    assert t % tm == 0 and tm % line == 0
    db = d // 2
    ts = max(PREP_TOKENS, line)
    assert tm % ts == 0 and ts % line == 0
    band, invc = _pool_constants(ts, line)
    full = lambda a: pl.BlockSpec(a.shape, lambda bi, i: (0,) * a.ndim)
    args = (mod, norm_g.reshape(1, d), w_in.astype(BF16), w_out.astype(BF16), pool_w.astype(BF16),
            pool_scale.reshape(1, db), sconv_w, band, invc)
    return pl.pallas_call(
        functools.partial(_even_kernel, ctx=ctx, line=line),
        grid=(b, t // tm),
        in_specs=[pl.BlockSpec((None, tm, d), lambda bi, i: (bi, i, 0))] + [full(a) for a in args],
        out_specs=pl.BlockSpec((None, tm, d), lambda bi, i: (bi, i, 0)),
        out_shape=jax.ShapeDtypeStruct((b, t, d), F32),
        compiler_params=pltpu.CompilerParams(dimension_semantics=("parallel", "parallel"),
                                             vmem_limit_bytes=VMEM_LIMIT),
        name="even_ctx" if ctx else "even_latent",
    )(x, *args)


def _front_kernel(x_ref, xp_ref, xn_ref, mod_ref, ng_ref, win_ref,
                  mu_ref, w0_ref, a0_ref, w1_ref, w2_ref, a1_ref, a2_ref,
                  g1_ref, g2_ref, kk_ref, ka_ref, rk_ref, seg_ref, later_ref, whole_ref,
                  rest_ref, v_ref, g_ref, bonus_ref, at_ref, rt_ref, bk_ref, gc_ref, *, ctx):
    tm, d = x_ref.shape
    db = v_ref.shape[-1]
    ts = PREP_TOKENS
    te = ts + 2 * SUBLANES
    i, n = pl.program_id(1), pl.num_programs(1)
    mrow = _mod_row(mod_ref, ctx)
    h_prev = _modulated(xp_ref[...], ng_ref[...], mrow, d) * jnp.where(i > 0, 1.0, 0.0)
    h_next = _modulated(xn_ref[...], ng_ref[...], mrow, d) * jnp.where(i < n - 1, 1.0, 0.0)
    h = _modulated(x_ref[...], ng_ref[...], mrow, d)
    h_b = h.astype(BF16)
    h_ext = jnp.concatenate([h_prev, h, h_next], axis=0).astype(BF16)
    mu = mu_ref[...]
    first_chunk = lax.broadcasted_iota(jnp.int32, (db, ts), 1) < CHUNK

    def rest(c, j):
        rows = c["rows"]
        if j == 1:
            p12 = _dot(h_b[rows], win_ref[:, 5 * db:7 * db], False)
            rest_ref[rows, db:2 * db] = (p12[:, 0:db] * _sigmoid(p12[:, db:])).astype(BF16)
        else:
            col = 4 if j == 0 else 7
            rest_ref[rows, j * db:(j + 1) * db] = _dot(h_b[rows], win_ref[:, col * db:(col + 1) * db],
                                                        False).astype(BF16)

    def project(c):
        c.update(z_ext=_dot(h_ext[c["s"] * ts:c["s"] * ts + te], win_ref[:, 0:4 * db], False))

    def shift(c):
        z_ext = c["z_ext"]
        nbr = (pltpu.roll(z_ext, 1, 0) + pltpu.roll(z_ext, te - 1, 0))[SUBLANES:ts + SUBLANES]
        z = z_ext[SUBLANES:ts + SUBLANES]
        dz = 0.5 * nbr - z
        u, du = z[:, 0:db], dz[:, 0:db]
        v = z[:, 3 * db:4 * db] + dz[:, 3 * db:4 * db] * mu[2:3, :]
        c.update(r=z[:, db:2 * db] + dz[:, db:2 * db] * mu[0:1, :],
                 k=z[:, 2 * db:3 * db] + dz[:, 2 * db:3 * db] * mu[1:2, :], v=v,
                 uw=u + du * mu[3:4, :], ua=u + du * mu[4:5, :], ug=u + du * mu[5:6, :])
        v_ref[c["rows"], :] = v.astype(BF16)
        rest(c, 0)

    def loras(c):
        g_ref[c["rows"], :] = _mm(_sigmoid(_mm(c["ug"], g1_ref[...])), g2_ref[...]).astype(BF16)
        kk = c["k"] * kk_ref[...]
        inv_norm = lax.rsqrt(jnp.maximum(_mm(kk * kk, seg_ref[...], 2, 1), 1e-24))
        c.update(lw_pre=w0_ref[...] + _mm(jnp.tanh(_mm(c["uw"], w1_ref[...])), w2_ref[...]),
                 a_pre=a0_ref[...] + _mm(_mm(c["ua"], a1_ref[...]), a2_ref[...]),
                 kk=kk * inv_norm, ksum=None)
        rest(c, 1)

    def direction(dr):
        def stage(c):
            rows, k, kk = c["rows"], c["k"], c["kk"]
            lw = -DECAY_SCALE * _sigmoid(c["lw_pre"][:, dr * db:(dr + 1) * db])
            a = _sigmoid(c["a_pre"][:, dr * db:(dr + 1) * db])
            k_d = k * (1.0 + (a - 1.0) * ka_ref[...])
            c["ksum"] = k_d if c["ksum"] is None else c["ksum"] + k_d
            lw_parts = _split(lw, 2)
            rel = -_mmp([later_ref[dr]], lw_parts)
            at_ref[dr, rows, :] = (-kk * jnp.exp(rel - lw)).astype(BF16)
            rt_ref[dr, rows, :] = (c["r"] * jnp.exp(rel)).astype(BF16)
            gout = jnp.exp(-rel)
            bh_t = (kk * a * gout).T
            kh_t = (k_d * gout).T
            bh_r, kh_r = pltpu.roll(bh_t, CHUNK, 1), pltpu.roll(kh_t, CHUNK, 1)
            gam = jnp.exp(_mmp([whole_ref[...]], lw_parts))
            for q in range(ts // CHUNK):
                own = first_chunk if q == 0 else jnp.logical_not(first_chunk)
                chunk = c["s"] * (ts // CHUNK) + q
                bk_ref[dr, chunk] = jnp.concatenate([jnp.where(own, bh_t, bh_r), jnp.where(own, kh_t, kh_r)],
                                                    axis=1).astype(BF16)
                gc_ref[dr, chunk] = gam[q * SUBLANES:(q + 1) * SUBLANES, :]
            if dr == 0:
                rest(c, 2)
        return stage

    def bonus(c):
        bonus_ref[c["rows"], :] = _mm(c["r"] * c["ksum"] * rk_ref[...], seg_ref[...], 2, 1) * c["v"]

    subs = [dict(s=s, rows=slice(s * ts, (s + 1) * ts)) for s in range(tm // ts)]
    for stage in (project, shift, loras, direction(0), direction(1), bonus):
        for c in subs:
            stage(c)


def _prep_constants(tm):
    t = np.arange(tm)
    same = (t[:, None] // CHUNK) == (t[None, :] // CHUNK)
    later = np.stack([same & (t[None, :] > t[:, None]), same & (t[None, :] < t[:, None])]).astype(np.float32)
    whole = (np.arange(tm // CHUNK * SUBLANES)[:, None] // SUBLANES) == (t[None, :] // CHUNK)
    return jnp.asarray(later, BF16), jnp.asarray(whole.astype(np.float32), BF16)


def _pad_to(a, axis, size):
    pad = [(0, 0)] * a.ndim
    pad[axis] = (0, size - a.shape[axis])
    return jnp.pad(a, pad)


def _odd_front(x, mod, norm_g, w_in, p, *, ctx):
    b, t, d = x.shape
    db = p["k_k"].shape[-1]
    tm = min(2 * TILE_TOKENS, t)
    assert t % tm == 0 and tm % PREP_TOKENS == 0 and PREP_TOKENS == 2 * CHUNK
    nc = t // CHUNK
    nb = t // SUBLANES
    per = tm // SUBLANES
    later, whole = _prep_constants(PREP_TOKENS)
    heads = db // RWKV_HEAD
    seg = jnp.asarray(np.kron(np.eye(heads), np.ones((RWKV_HEAD, RWKV_HEAD))), BF16)

    def lora_pair(w_a, w_b):
        rank = w_a.shape[-1]
        down = _pad_to(jnp.concatenate([w_a[0], w_a[1]], axis=-1), 1, LORA_PAD)
        up = jnp.zeros((LORA_PAD, 2 * db), F32)
        up = up.at[0:rank, 0:db].set(w_b[0]).at[rank:2 * rank, db:2 * db].set(w_b[1])
        return down.astype(BF16), up.astype(BF16)

    w1c, w2c = lora_pair(p["w1"], p["w2"])
    a1c, a2c = lora_pair(p["a1"], p["a2"])
    consts = (mod, norm_g.reshape(1, d), w_in.astype(BF16),
              _pad_to(p["rwkv_mu"], 0, SUBLANES), p["w0"].reshape(1, 2 * db), p["a0"].reshape(1, 2 * db),
              w1c, w2c, a1c, a2c,
              _pad_to(p["g1"], 1, LORA_PAD).astype(BF16), _pad_to(p["g2"], 0, LORA_PAD).astype(BF16),
              p["k_k"].reshape(1, db), p["k_a"].reshape(1, db), p["r_k"].reshape(1, db), seg, later, whole)
    full = lambda a: pl.BlockSpec(a.shape, lambda bi, i: (0,) * a.ndim)
    tok = pl.BlockSpec((None, tm, db), lambda bi, i: (bi, i, 0))
    dirtok = pl.BlockSpec((None, 2, tm, db), lambda bi, i: (bi, 0, i, 0))
    tok_shape = jax.ShapeDtypeStruct((b, t, db), F32)
    dir_shape = jax.ShapeDtypeStruct((b, 2, t, db), BF16)
    nrest = 3 * db
    assert w_in.shape[1] == 8 * db
    tok_b = jax.ShapeDtypeStruct((b, t, db), BF16)
    return pl.pallas_call(
        functools.partial(_front_kernel, ctx=ctx),
        grid=(b, t // tm),
        in_specs=[pl.BlockSpec((None, tm, d), lambda bi, i: (bi, i, 0)),
                  pl.BlockSpec((None, SUBLANES, d), lambda bi, i: (bi, jnp.maximum(i * per - 1, 0), 0)),
                  pl.BlockSpec((None, SUBLANES, d), lambda bi, i: (bi, jnp.minimum((i + 1) * per, nb - 1), 0))]
        + [full(a) for a in consts],
        out_specs=[pl.BlockSpec((None, tm, nrest), lambda bi, i: (bi, i, 0)), tok, tok, tok, dirtok, dirtok,
                   pl.BlockSpec((None, 2, tm // CHUNK, db, 4 * CHUNK), lambda bi, i: (bi, 0, i, 0, 0)),
                   pl.BlockSpec((None, 2, tm // CHUNK, SUBLANES, db), lambda bi, i: (bi, 0, i, 0, 0))],
        out_shape=[jax.ShapeDtypeStruct((b, t, nrest), BF16), tok_b, tok_b, tok_shape, dir_shape, dir_shape,
                   jax.ShapeDtypeStruct((b, 2, nc, db, 4 * CHUNK), BF16),
                   jax.ShapeDtypeStruct((b, 2, nc, SUBLANES, db), F32)],
        compiler_params=pltpu.CompilerParams(dimension_semantics=("parallel", "parallel"),
                                             vmem_limit_bytes=VMEM_LIMIT),
        name="front_ctx" if ctx else "front_latent",
    )(x, x, x, *consts)


SCAN_PARTS = 1
SCAN_CHUNKS_PER_STEP = 4


def _scan_kernel(at0, rt0, bk0, v0, gc0, at1, rt1, bk1, v1, gc1, s0_ref,
                 hm_ref, ms_ref, mi_ref, idw_ref, bdm_ref, eye_ref,
                 of_ref, or_ref, st_ref):
    @pl.when(pl.program_id(0) == 0)
    def _():
        st_ref[...] = s0_ref[...]

    npart = SCAN_PARTS
    pair = GROUP // 2
    bdm_b = bdm_ref[...]
    bd2 = bdm_b[0:pair, 0:pair].astype(F32)
    eye2 = eye_ref[0:pair, 0:pair]

    def bd(x):
        return [jnp.concatenate([part * hm_ref[h] for h in range(HEADS_PER_GROUP)], axis=0)
                for part in _split(x, npart)]

    def cat_parts(plists):
        return [jnp.concatenate(ps, axis=-1) for ps in zip(*plists)]

    steps = int(np.log2(CHUNK)) - 1

    def load(c):
        at_r, rt_r, bk_r, v_r, gc_r = c["refs"]
        bi, g, sl, q, rows = c["bi"], c["g"], c["sl"], c["q"], c["rows"]
        c.update(at=at_r[bi, rows, sl], rt=rt_r[bi, rows, sl], v=v_r[bi, rows, sl], gc=gc_r[bi, q, 0:1, sl],
                 bk=bk_r[bi, q, g * GROUP:(g + 1) * GROUP, :])

    def interactions(c):
        ms, mi = ms_ref[c["d"]], mi_ref[c["d"]]
        bk = c["bk"]
        wts = jnp.concatenate([jnp.concatenate([bk[:, 0:pair]] * 2, axis=1) * bdm_b,
                               jnp.concatenate([bk[:, pair:]] * 2, axis=1) * bdm_b], axis=1)
        gmat = _dot(jnp.concatenate([c["at"], c["rt"]], axis=0), wts, False)
        lab = gmat[0:CHUNK, 0:GROUP] * ms
        c.update(lak=gmat[0:CHUNK, GROUP:] * ms, mrb=gmat[CHUNK:, 0:GROUP] * mi,
                 mrk=gmat[CHUNK:, GROUP:] * mi, pw=lab, tinv=idw_ref[...] + lab)

    def invert(s):
        def stage(c):
            pw, tinv = c["pw"], c["tinv"]
            wts = bd(pw)
            if s == 0:
                pw = _mmp(_split(pw, npart), wts)
            elif s < steps:
                res = _mmp(_split(jnp.concatenate([pw, tinv], axis=0), npart), wts)
                pw, tinv = res[0:CHUNK], tinv + res[CHUNK:]
            else:
                tinv = tinv + _mmp(_split(tinv, npart), wts)
            c.update(pw=pw, tinv=tinv)
        return stage

    def values(c):
        lv_mv = _mmp(_split(jnp.concatenate([c["lak"], c["mrk"]], axis=0), npart), bd(c["v"]))
        c.update(lv=lv_mv[0:CHUNK], mv=lv_mv[CHUNK:])

    def solve(c):
        aw = _mmp(_split(c["tinv"], npart), cat_parts([bd(c["at"]), bd(c["lv"])]))
        c.update(ahat=aw[:, 0:GROUP], wv=aw[:, GROUP:])

    def outputs(c):
        ro = _mmp(_split(c["mrb"], npart), cat_parts([bd(c["ahat"]), bd(c["wv"])]))
        c.update(rhat=(c["rt"].astype(F32) + ro[:, 0:GROUP]) * c["gc"], ointra=ro[:, GROUP:] + c["mv"])

    def transition(c):
        zero_rows = jnp.zeros((CHUNK, GROUP), BF16)
        zero_half = jnp.zeros((CHUNK, pair), BF16)
        zero_blk = jnp.zeros((pair, pair), F32)
        pms, qms = [], []
        for p in range(2):
            ps = slice(p * pair, (p + 1) * pair)
            top = jnp.concatenate([c["ahat"][:, ps], c["wv"][:, ps]], axis=1).astype(BF16)
            mid = jnp.concatenate([zero_half, c["v"][:, ps]], axis=1)
            rhs = jnp.concatenate([top, zero_rows, mid, zero_rows], axis=0)
            pq = _dot(c["bk"][ps, :], rhs, False)
            pblk = (pq[:, 0:pair] * bd2 + eye2) * c["gc"][:, ps]
            qblk = pq[:, pair:] * bd2
            pms.append(jnp.concatenate([pblk, zero_blk] if p == 0 else [zero_blk, pblk], axis=1))
            qms.append(jnp.concatenate([qblk, zero_blk] if p == 0 else [zero_blk, qblk], axis=1))
        c.update(pm=jnp.concatenate(pms, axis=0), qm=jnp.concatenate(qms, axis=0))

    def advance(c, st):
        rs = _dot(jnp.concatenate([c["rhat"], c["pm"]], axis=0).astype(BF16), st.astype(BF16), False)
        c["out"][c["bi"], c["rows"], c["sl"]] = rs[0:CHUNK] + c["ointra"]
        return rs[CHUNK:] + c["qm"]

    nq = at0.shape[1] // CHUNK
    dirs = ((at0, rt0, bk0, v0, gc0, of_ref), (at1, rt1, bk1, v1, gc1, or_ref))
    groups = {}
    for bi in range(at0.shape[0]):
        for d, refs in enumerate(dirs):
            for g in range(at0.shape[-1] // GROUP):
                order = range(nq) if d == 0 else reversed(range(nq))
                groups[(bi, d, g)] = [
                    dict(q=q, rows=slice(q * CHUNK, (q + 1) * CHUNK), bi=bi, d=d, g=g,
                         sl=slice(g * GROUP, (g + 1) * GROUP), refs=refs[:5], out=refs[5]) for q in order]
    chains = [c for k in range(nq) for cs in groups.values() for c in [cs[k]]]
    stages = ([load, interactions] + [invert(s) for s in range(steps + 1)]
              + [values, solve, outputs, transition])
    for stage in stages:
        for c in chains:
            stage(c)
    states = {key: st_ref[key[0], key[1], key[2]] for key in groups}
    for k in range(nq):
        for key, cs in groups.items():
            states[key] = advance(cs[k], states[key])
    for key in groups:
        st_ref[key[0], key[1], key[2]] = states[key]


def _scan_constants():
    t = np.arange(CHUNK)[:, None]
    lane = np.arange(GROUP)[None, :]
    s = lane % CHUNK
    strict = np.stack([s < t, s > t]).astype(np.float32)
    incl = np.stack([s <= t, s >= t]).astype(np.float32)
    idw = (s == t).astype(np.float32)
    hm = np.stack([np.broadcast_to((lane // RWKV_HEAD) == h, (CHUNK, GROUP)) for h in range(HEADS_PER_GROUP)])
    bdm = np.kron(np.eye(HEADS_PER_GROUP), np.ones((RWKV_HEAD, RWKV_HEAD)))
    return (jnp.asarray(hm.astype(np.float32), BF16), jnp.asarray(strict), jnp.asarray(incl), jnp.asarray(idw),
            jnp.asarray(bdm, BF16), jnp.eye(GROUP, dtype=F32))


def _rwkv_scan(at, rt, bk, v, gc, s0):
    b, _, t, db = at.shape
    nc = t // CHUNK
    consts = _scan_constants()
    full = lambda a: pl.BlockSpec(a.shape, lambda i: (0,) * a.ndim)
    nq = min(SCAN_CHUNKS_PER_STEP, nc)
    ns = nc // nq
    assert nc % nq == 0
    fwd, rev = (lambda i: i), (lambda i: ns - 1 - i)
    dspec = lambda d, ci: pl.BlockSpec((b, None, nq * CHUNK, db), lambda i: (0, d, ci(i), 0))
    kspec = lambda d, ci: pl.BlockSpec((b, None, nq, db, 4 * CHUNK), lambda i: (0, d, ci(i), 0, 0))
    vspec = lambda ci: pl.BlockSpec((b, nq * CHUNK, db), lambda i: (0, ci(i), 0))
    gspec = lambda d, ci: pl.BlockSpec((b, None, nq, SUBLANES, db), lambda i: (0, d, ci(i), 0, 0))
    in_specs = ([dspec(0, fwd), dspec(0, fwd), kspec(0, fwd), vspec(fwd), gspec(0, fwd),
                 dspec(1, rev), dspec(1, rev), kspec(1, rev), vspec(rev), gspec(1, rev), full(s0)]
                + [full(a) for a in consts])
    return pl.pallas_call(
        _scan_kernel,
        grid=(ns,),
        in_specs=in_specs,
        out_specs=[vspec(fwd), vspec(rev), full(s0)],
        out_shape=[jax.ShapeDtypeStruct((b, t, db), F32), jax.ShapeDtypeStruct((b, t, db), F32),
                   jax.ShapeDtypeStruct(s0.shape, F32)],
        compiler_params=pltpu.CompilerParams(dimension_semantics=("arbitrary",),
                                             vmem_limit_bytes=VMEM_LIMIT),
        name="rwkv_scan",
    )(at, rt, bk, v, gc, at, rt, bk, v, gc, s0, *consts)


def _conv_kernel(glu_ref, w_ref, b_ref, o_ref, h_ref, *, stride):
    t, c = glu_ref.shape
    halo = (CONF_CONV_W // 2) * stride
    nsteps = t // CONV_ROWS
    zeros = jnp.zeros((halo, c), F32)
    h_ref[0:halo, :] = zeros
    h_ref[halo + t:halo + t + halo, :] = zeros

    def widen(j, carry):
        r0 = pl.multiple_of(j * CONV_ROWS, CONV_ROWS)
        h_ref[pl.ds(halo + r0, CONV_ROWS), :] = glu_ref[pl.ds(r0, CONV_ROWS), :].astype(F32)
        return carry

    lax.fori_loop(0, nsteps, widen, 0)
    w = w_ref[...]
    bias = b_ref[...]

    def conv(j, carry):
        r0 = pl.multiple_of(j * CONV_ROWS, CONV_ROWS)
        acc = jnp.broadcast_to(bias, (CONV_ROWS, c))
        for tap in range(CONF_CONV_W):
            acc = acc + h_ref[pl.ds(r0 + tap * stride, CONV_ROWS), :] * w[tap:tap + 1, :]
        o_ref[pl.ds(r0, CONV_ROWS), :] = acc
        return carry

    lax.fori_loop(0, nsteps, conv, 0)


def _conformer_conv(proj, dw_w, dw_b, *, stride):
    b, t, n = proj.shape
    db = dw_w.shape[-1]
    lanes = 128
    assert t % CONV_ROWS == 0 and stride % SUBLANES == 0
    halo = (CONF_CONV_W // 2) * stride
    c1 = db // lanes
    return pl.pallas_call(
        functools.partial(_conv_kernel, stride=stride),
        grid=(b, db // lanes),
        in_specs=[pl.BlockSpec((None, t, lanes), lambda bi, j: (bi, 0, c1 + j)),
                  pl.BlockSpec((_round_up(CONF_CONV_W, SUBLANES), lanes), lambda bi, j: (0, j)),
                  pl.BlockSpec((1, lanes), lambda bi, j: (0, j))],
        out_specs=pl.BlockSpec((None, t, lanes), lambda bi, j: (bi, 0, j)),
        out_shape=jax.ShapeDtypeStruct((b, t, db), F32),
        scratch_shapes=[pltpu.VMEM((t + 2 * halo, lanes), F32)],
        compiler_params=pltpu.CompilerParams(dimension_semantics=("parallel", "parallel"),
                                             vmem_limit_bytes=VMEM_LIMIT),
        name="conformer_conv",
    )(proj, _pad_to(dw_w, 0, _round_up(CONF_CONV_W, SUBLANES)), dw_b.reshape(1, db))


def _round_up(x, m):
    return (x + m - 1) // m * m


def _merge_kernel(x_ref, of_ref, or_ref, g_ref, bonus_ref, conv_ref, zc_ref, zd_ref, mod_ref,
                  lnxg_ref, lnxb_ref, clng_ref, clnb_ref, wout_ref, fg_ref, seg_ref, o_ref):
    tm, d = x_ref.shape
    mrow = _mod_row(mod_ref, False)
    seg = seg_ref[...]

    def centre(c):
        o = of_ref[c["rows"], :] + or_ref[c["rows"], :]
        c.update(dev=o - _mm(o, seg, 2, 1))

    def normalise(c):
        rows, dev = c["rows"], c["dev"]
        var = _mm(dev * dev, seg, 2, 1)
        on = dev * lax.rsqrt(var + LNX_EPS) * lnxg_ref[...] + lnxb_ref[...]
        y_c = (on + bonus_ref[rows, :]) * g_ref[rows, :].astype(F32) * _silu(zc_ref[rows, :].astype(F32))
        cv = conv_ref[rows, :]
        cdev = cv - jnp.mean(cv, axis=-1, keepdims=True)
        cvar = jnp.mean(cdev * cdev, axis=-1, keepdims=True)
        ln = cdev * lax.rsqrt(cvar + CONF_LN_EPS) * clng_ref[...] + clnb_ref[...]
        y_d = _silu(ln) * _silu(zd_ref[rows, :].astype(F32))
        c.update(y=jnp.concatenate([y_c, y_d], axis=-1))

    def emit(c):
        y = _mm(c["y"], wout_ref[...])
        o_ref[c["rows"], :] = _rmsnorm(x_ref[c["rows"], :] + mrow[:, 2 * d:3 * d] * y, fg_ref[...])

    ts = tm // MERGE_SUBTILES
    subs = [dict(rows=slice(s * ts, (s + 1) * ts)) for s in range(MERGE_SUBTILES)]
    for stage in (centre, normalise, emit):
        for c in subs:
            stage(c)


def _merge(x, o_f, o_r, g, bonus, conv, proj, mod, p, final_g):
    b, t, d = x.shape
    db = d // 2
    tm = min(TILE_TOKENS, t)
    heads = db // RWKV_HEAD
    seg = jnp.asarray(np.kron(np.eye(heads), np.full((RWKV_HEAD, RWKV_HEAD), 1.0 / RWKV_HEAD)), BF16)
    consts = (mod, p["lnx_g"].reshape(1, db), p["lnx_b"].reshape(1, db), p["conf_ln_g"].reshape(1, db),
              p["conf_ln_b"].reshape(1, db), p["out_o"].astype(BF16), final_g.reshape(1, d), seg)
    full = lambda a: pl.BlockSpec(a.shape, lambda bi, i: (0,) * a.ndim)
    tok = pl.BlockSpec((None, tm, db), lambda bi, i: (bi, i, 0))
    col = lambda cblk: pl.BlockSpec((None, tm, db), lambda bi, i: (bi, i, cblk))
    return pl.pallas_call(
        _merge_kernel,
        grid=(b, t // tm),
        in_specs=[pl.BlockSpec((None, tm, d), lambda bi, i: (bi, i, 0)), tok, tok, tok, tok, tok,
                  col(0), col(2)] + [full(a) for a in consts],
        out_specs=pl.BlockSpec((None, tm, d), lambda bi, i: (bi, i, 0)),
        out_shape=jax.ShapeDtypeStruct((b, t, d), F32),
        compiler_params=pltpu.CompilerParams(dimension_semantics=("parallel", "parallel"),
                                             vmem_limit_bytes=VMEM_LIMIT),
        name="merge",
    )(x, o_f, o_r, g, bonus, conv, proj, proj, *consts)


def kernel(x, c, ctx, c_ctx, ada_w_e, ada_b_e, norm_e, in_e, out_e, pool_w, pool_scale, sconv_w,
           ada_w_o, ada_b_o, norm_o, in_o, out_o, rwkv_mu, w0, w1, w2, a0, a1, a2, g1, g2,
           k_k, k_a, r_k, lnx_g, lnx_b, conf_dw_w, conf_dw_b, conf_ln_g, conf_ln_b, final_g):
    b, t, d = x.shape
    db = d // 2
    assert b == 2
    cvec = _pad_to(jnp.concatenate([c, c_ctx[None, :]], axis=0), 0, SUBLANES)
    mod_e = _adaln(cvec, ada_w_e[0], ada_b_e[0])
    mod_o = _adaln(cvec, ada_w_o[0], ada_b_o[0])

    even = dict(norm_g=norm_e[0], w_in=in_e[0], w_out=out_e[0], pool_w=pool_w[0],
                pool_scale=pool_scale[0], sconv_w=sconv_w[0])
    x1 = _even_layer(x, mod_e, ctx=False, line=GRID_W, **even)
    xc1 = _even_layer(ctx, mod_e, ctx=True, line=ctx.shape[1], **even)

    p = dict(rwkv_mu=rwkv_mu[0], w0=w0[0], w1=w1[0], w2=w2[0], a0=a0[0], a1=a1[0], a2=a2[0],
             g1=g1[0], g2=g2[0], k_k=k_k[0], k_a=k_a[0], r_k=r_k[0], lnx_g=lnx_g[0], lnx_b=lnx_b[0],
             conf_ln_g=conf_ln_g[0], conf_ln_b=conf_ln_b[0], out_o=out_o[0])

    _, v_c, _, _, at_c, rt_c, bk_c, gc_c = _odd_front(xc1, mod_o, norm_o[0], in_o[0], p, ctx=True)
    zero_state = jnp.zeros((b, 2, db // GROUP, GROUP, GROUP), F32)
    _, _, ctx_state = _rwkv_scan(at_c, rt_c, bk_c, v_c, gc_c, zero_state)

    proj, v_l, g_l, bonus, at_l, rt_l, bk_l, gc_l = _odd_front(x1, mod_o, norm_o[0], in_o[0], p, ctx=False)
    o_f, o_r, _ = _rwkv_scan(at_l, rt_l, bk_l, v_l, gc_l, ctx_state)
    conv = _conformer_conv(proj, conf_dw_w[0], conf_dw_b[0], stride=GRID_W)
    return _merge(x1, o_f, o_r, g_l, bonus, conv, proj, mod_o, p, final_g)
```

```python
import functools

import numpy as np
import jax
import jax.numpy as jnp
from jax import lax
from jax.experimental import pallas as pl
from jax.experimental.pallas import tpu as pltpu

F32 = jnp.float32
BF16 = jnp.bfloat16

GRID_W = 64
POOL_WINDOWS = (2, 4, 8, 16)
SHORT_CONV_W = 3
CONF_CONV_W = 31
RWKV_HEAD = 64
NORM_EPS = 1e-6
LNX_EPS = 64e-5
CONF_LN_EPS = 1e-5
DECAY_SCALE = float(np.exp(-0.5))

CHUNK = 64
HEADS_PER_GROUP = 4
GROUP = HEADS_PER_GROUP * RWKV_HEAD
LORA_PAD = 128
SUBLANES = 8
VMEM_LIMIT = 52 * 1024 * 1024

TILE_TOKENS = 256
PREP_TOKENS = 128
CONV_ROWS = 128
MERGE_SUBTILES = 2


def _sigmoid(x):
    return 0.5 * jnp.tanh(0.5 * x) + 0.5


def _silu(x):
    return x * _sigmoid(x)


def _split(x, n):
    if x.dtype == BF16:
        return [x]
    parts, r = [], x
    for i in range(n):
        p = r.astype(BF16)
        parts.append(p)
        if i + 1 < n:
            r = r - p.astype(F32)
    return parts


def _dot(a, b, nt):
    dims = (((1,), (1,)), ((), ())) if nt else (((1,), (0,)), ((), ()))
    return lax.dot_general(a, b, dims, preferred_element_type=F32)


def _mmp(ap, bp, nt=False):
    order = max(len(ap), len(bp))
    acc = None
    for i in reversed(range(len(ap))):
        for j in reversed(range(len(bp))):
            if i + j < order:
                t = _dot(ap[i], bp[j], nt)
                acc = t if acc is None else acc + t
    return acc


def _mm(a, b, na=1, nb=1, nt=False):
    return _mmp(_split(a, na), _split(b, nb), nt)


def _rmsnorm(x, g):
    ms = jnp.mean(x * x, axis=-1, keepdims=True)
    return x * lax.rsqrt(ms + NORM_EPS) * g


def _modulated(x, g, mod_row, d):
    shift, scale = mod_row[:, 0:d], mod_row[:, d:2 * d]
    return _rmsnorm(x, g) * (1.0 + scale) + shift


def _mod_row(mod_ref, ctx):
    row = 2 if ctx else pl.program_id(0)
    return mod_ref[pl.ds(row, 1), :]


def _adaln_kernel(c_ref, w_ref, b_ref, o_ref):
    o_ref[...] = _mm(_silu(c_ref[...]), w_ref[...], 2, 2) + b_ref[...]


def _adaln(cvec, w, b):
    d, n = w.shape
    tn = 768
    return pl.pallas_call(
        _adaln_kernel,
        grid=(n // tn,),
        in_specs=[pl.BlockSpec((SUBLANES, d), lambda j: (0, 0)),
                  pl.BlockSpec((d, tn), lambda j: (0, j)),
                  pl.BlockSpec((1, tn), lambda j: (0, j))],
        out_specs=pl.BlockSpec((SUBLANES, tn), lambda j: (0, j)),
        out_shape=jax.ShapeDtypeStruct((SUBLANES, n), F32),
        compiler_params=pltpu.CompilerParams(dimension_semantics=("parallel",),
                                             vmem_limit_bytes=VMEM_LIMIT),
        name="adaln",
    )(cvec, w, b.reshape(1, n))


def _even_kernel(x_ref, mod_ref, g_ref, win_ref, wout_ref, poolw_ref, pscale_ref, sconv_ref,
                 band_ref, invc_ref, o_ref, *, ctx, line):
    tm, d = x_ref.shape
    ts = band_ref.shape[-1]
    db = d // 2
    grp = db // len(POOL_WINDOWS)
    mrow = _mod_row(mod_ref, ctx)

    def project(c):
        x = x_ref[c["rows"], :]
        c.update(x=x, proj=_mm(_modulated(x, g_ref[...], mrow, d), win_ref[...]))

    def pool(c):
        u_a = c["proj"][:, 0:db]
        pooled = []
        for i in range(len(POOL_WINDOWS)):
            ug = u_a[:, i * grp:(i + 1) * grp]
            pooled.append(_mm(band_ref[i], ug) * invc_ref[i] - ug)
        c.update(pooled=pooled)

    def mix(c):
        proj = c["proj"]
        z_a = proj[:, db:2 * db]
        v_b, g_b = proj[:, 2 * db:3 * db], proj[:, 3 * db:4 * db]
        g_c, z_b = proj[:, 4 * db:5 * db], proj[:, 5 * db:6 * db]
        ya = [_mm(c["pooled"][i], poolw_ref[i]) for i in range(len(POOL_WINDOWS))]
        y_a = jnp.concatenate(ya, axis=-1) * pscale_ref[...] * _silu(z_a)
        q = g_c * v_b
        pos = lax.broadcasted_iota(jnp.int32, q.shape, 0) % line
        q_prev = jnp.where(pos >= 1, pltpu.roll(q, 1, 0), 0.0)
        q_next = jnp.where(pos <= line - 2, pltpu.roll(q, ts - 1, 0), 0.0)
        w = sconv_ref[...]
        conv = q_prev * w[0:1, :] + q * w[1:2, :] + q_next * w[2:3, :]
        y_b = g_b * conv * _silu(z_b)
        c.update(y=jnp.concatenate([y_a, y_b], axis=-1))

    def emit(c):
        o_ref[c["rows"], :] = c["x"] + mrow[:, 2 * d:3 * d] * _mm(c["y"], wout_ref[...])

    subs = [dict(rows=slice(s * ts, (s + 1) * ts)) for s in range(tm // ts)]
    for stage in (project, pool, mix, emit):
        for c in subs:
            stage(c)


def _pool_constants(tm, line):
    pos = np.arange(tm) % line
    lid = np.arange(tm) // line
    bands, invs = [], []
    for w in POOL_WINDOWS:
        lo = np.clip(pos - w // 2, 0, line)
        hi = np.clip(pos - w // 2 + w, 0, line)
        same = lid[:, None] == lid[None, :]
        band = same & (pos[None, :] >= lo[:, None]) & (pos[None, :] < hi[:, None])
        bands.append(band.astype(np.float32))
        invs.append(np.broadcast_to((1.0 / (hi - lo))[:, None], (tm, 128)).astype(np.float32))
    return jnp.asarray(np.stack(bands), BF16), jnp.asarray(np.stack(invs), F32)


def _even_layer(x, mod, norm_g, w_in, w_out, pool_w, pool_scale, sconv_w, *, ctx, line):
    b, t, d = x.shape
    tm = min(2 * TILE_TOKENS, t)
    assert t % tm == 0 and tm % line == 0
    db = d // 2
    ts = max(PREP_TOKENS, line)
    assert tm % ts == 0 and ts % line == 0
    band, invc = _pool_constants(ts, line)
    full = lambda a: pl.BlockSpec(a.shape, lambda bi, i: (0,) * a.ndim)
    args = (mod, norm_g.reshape(1, d), w_in.astype(BF16), w_out.astype(BF16), pool_w.astype(BF16),
            pool_scale.reshape(1, db), sconv_w, band, invc)
    return pl.pallas_call(
        functools.partial(_even_kernel, ctx=ctx, line=line),
        grid=(b, t // tm),
        in_specs=[pl.BlockSpec((None, tm, d), lambda bi, i: (bi, i, 0))] + [full(a) for a in args],
        out_specs=pl.BlockSpec((None, tm, d), lambda bi, i: (bi, i, 0)),
        out_shape=jax.ShapeDtypeStruct((b, t, d), F32),
        compiler_params=pltpu.CompilerParams(dimension_semantics=("parallel", "parallel"),
                                             vmem_limit_bytes=VMEM_LIMIT),
        name="even_ctx" if ctx else "even_latent",
    )(x, *args)


def _front_kernel(x_ref, xp_ref, xn_ref, mod_ref, ng_ref, win_ref,
                  mu_ref, w0_ref, a0_ref, w1_ref, w2_ref, a1_ref, a2_ref,
                  g1_ref, g2_ref, kk_ref, ka_ref, rk_ref, seg_ref, later_ref, whole_ref,
                  rest_ref, v_ref, g_ref, bonus_ref, at_ref, rt_ref, bk_ref, gc_ref, *, ctx):
    tm, d = x_ref.shape
    db = v_ref.shape[-1]
    ts = PREP_TOKENS
    te = ts + 2 * SUBLANES
    i, n = pl.program_id(1), pl.num_programs(1)
    mrow = _mod_row(mod_ref, ctx)
    h_prev = _modulated(xp_ref[...], ng_ref[...], mrow, d) * jnp.where(i > 0, 1.0, 0.0)
    h_next = _modulated(xn_ref[...], ng_ref[...], mrow, d) * jnp.where(i < n - 1, 1.0, 0.0)
    h = _modulated(x_ref[...], ng_ref[...], mrow, d)
    h_b = h.astype(BF16)
    h_ext = jnp.concatenate([h_prev, h, h_next], axis=0).astype(BF16)
    mu = mu_ref[...]
    first_chunk = lax.broadcasted_iota(jnp.int32, (db, ts), 1) < CHUNK

    def rest(c, j):
        rows = c["rows"]
        if j == 1:
            p12 = _dot(h_b[rows], win_ref[:, 5 * db:7 * db], False)
            rest_ref[rows, db:2 * db] = (p12[:, 0:db] * _sigmoid(p12[:, db:])).astype(BF16)
        else:
            col = 4 if j == 0 else 7
            rest_ref[rows, j * db:(j + 1) * db] = _dot(h_b[rows], win_ref[:, col * db:(col + 1) * db],
                                                        False).astype(BF16)

    def project(c):
        c.update(z_ext=_dot(h_ext[c["s"] * ts:c["s"] * ts + te], win_ref[:, 0:4 * db], False))

    def shift(c):
        z_ext = c["z_ext"]
        nbr = (pltpu.roll(z_ext, 1, 0) + pltpu.roll(z_ext, te - 1, 0))[SUBLANES:ts + SUBLANES]
        z = z_ext[SUBLANES:ts + SUBLANES]
        dz = 0.5 * nbr - z
        u, du = z[:, 0:db], dz[:, 0:db]
        v = z[:, 3 * db:4 * db] + dz[:, 3 * db:4 * db] * mu[2:3, :]
        c.update(r=z[:, db:2 * db] + dz[:, db:2 * db] * mu[0:1, :],
                 k=z[:, 2 * db:3 * db] + dz[:, 2 * db:3 * db] * mu[1:2, :], v=v,
                 uw=u + du * mu[3:4, :], ua=u + du * mu[4:5, :], ug=u + du * mu[5:6, :])
        v_ref[c["rows"], :] = v.astype(BF16)
        rest(c, 0)

    def loras(c):
        g_ref[c["rows"], :] = _mm(_sigmoid(_mm(c["ug"], g1_ref[...])), g2_ref[...]).astype(BF16)
        kk = c["k"] * kk_ref[...]
        inv_norm = lax.rsqrt(jnp.maximum(_mm(kk * kk, seg_ref[...]), 1e-24))
        c.update(lw_pre=w0_ref[...] + _mm(jnp.tanh(_mm(c["uw"], w1_ref[...])), w2_ref[...]),
                 a_pre=a0_ref[...] + _mm(_mm(c["ua"], a1_ref[...]), a2_ref[...]),
                 kk=kk * inv_norm, ksum=None)
        rest(c, 1)

    def direction(dr):
        def stage(c):
            rows, k, kk = c["rows"], c["k"], c["kk"]
            lw = -DECAY_SCALE * _sigmoid(c["lw_pre"][:, dr * db:(dr + 1) * db])
            a = _sigmoid(c["a_pre"][:, dr * db:(dr + 1) * db])
            k_d = k * (1.0 + (a - 1.0) * ka_ref[...])
            c["ksum"] = k_d if c["ksum"] is None else c["ksum"] + k_d
            lw_parts = _split(lw, 2)
            rel = -_mmp([later_ref[dr]], lw_parts)
            at_ref[dr, rows, :] = (-kk * jnp.exp(rel - lw)).astype(BF16)
            rt_ref[dr, rows, :] = (c["r"] * jnp.exp(rel)).astype(BF16)
            gout = jnp.exp(-rel)
            bh_t = (kk * a * gout).T
            kh_t = (k_d * gout).T
            bh_r, kh_r = pltpu.roll(bh_t, CHUNK, 1), pltpu.roll(kh_t, CHUNK, 1)
            gam = jnp.exp(_mmp([whole_ref[...]], lw_parts))
            for q in range(ts // CHUNK):
                own = first_chunk if q == 0 else jnp.logical_not(first_chunk)
                chunk = c["s"] * (ts // CHUNK) + q
                bk_ref[dr, chunk] = jnp.concatenate([jnp.where(own, bh_t, bh_r), jnp.where(own, kh_t, kh_r)],
                                                    axis=1).astype(BF16)
                gc_ref[dr, chunk] = gam[q * SUBLANES:(q + 1) * SUBLANES, :]
            if dr == 0:
                rest(c, 2)
        return stage

    def bonus(c):
        bonus_ref[c["rows"], :] = (_mm(c["r"] * c["ksum"] * rk_ref[...], seg_ref[...]) * c["v"]).astype(BF16)

    subs = [dict(s=s, rows=slice(s * ts, (s + 1) * ts)) for s in range(tm // ts)]
    for stage in (project, shift, loras, direction(0), direction(1), bonus):
        for c in subs:
            stage(c)


def _prep_constants(tm):
    t = np.arange(tm)
    same = (t[:, None] // CHUNK) == (t[None, :] // CHUNK)
    later = np.stack([same & (t[None, :] > t[:, None]), same & (t[None, :] < t[:, None])]).astype(np.float32)
    whole = (np.arange(tm // CHUNK * SUBLANES)[:, None] // SUBLANES) == (t[None, :] // CHUNK)
    return jnp.asarray(later, BF16), jnp.asarray(whole.astype(np.float32), BF16)


def _pad_to(a, axis, size):
    pad = [(0, 0)] * a.ndim
    pad[axis] = (0, size - a.shape[axis])
    return jnp.pad(a, pad)


def _odd_front(x, mod, norm_g, w_in, p, *, ctx):
    b, t, d = x.shape
    db = p["k_k"].shape[-1]
    tm = min(2 * TILE_TOKENS, t)
    assert t % tm == 0 and tm % PREP_TOKENS == 0 and PREP_TOKENS == 2 * CHUNK
    nc = t // CHUNK
    nb = t // SUBLANES
    per = tm // SUBLANES
    later, whole = _prep_constants(PREP_TOKENS)
    heads = db // RWKV_HEAD
    seg = jnp.asarray(np.kron(np.eye(heads), np.ones((RWKV_HEAD, RWKV_HEAD))), BF16)

    def lora_pair(w_a, w_b):
        rank = w_a.shape[-1]
        down = _pad_to(jnp.concatenate([w_a[0], w_a[1]], axis=-1), 1, LORA_PAD)
        up = jnp.zeros((LORA_PAD, 2 * db), F32)
        up = up.at[0:rank, 0:db].set(w_b[0]).at[rank:2 * rank, db:2 * db].set(w_b[1])
        return down.astype(BF16), up.astype(BF16)

    w1c, w2c = lora_pair(p["w1"], p["w2"])
    a1c, a2c = lora_pair(p["a1"], p["a2"])
    consts = (mod, norm_g.reshape(1, d), w_in.astype(BF16),
              _pad_to(p["rwkv_mu"], 0, SUBLANES), p["w0"].reshape(1, 2 * db), p["a0"].reshape(1, 2 * db),
              w1c, w2c, a1c, a2c,
              _pad_to(p["g1"], 1, LORA_PAD).astype(BF16), _pad_to(p["g2"], 0, LORA_PAD).astype(BF16),
              p["k_k"].reshape(1, db), p["k_a"].reshape(1, db), p["r_k"].reshape(1, db), seg, later, whole)
    full = lambda a: pl.BlockSpec(a.shape, lambda bi, i: (0,) * a.ndim)
    tok = pl.BlockSpec((None, tm, db), lambda bi, i: (bi, i, 0))
    dirtok = pl.BlockSpec((None, 2, tm, db), lambda bi, i: (bi, 0, i, 0))
    tok_shape = jax.ShapeDtypeStruct((b, t, db), F32)
    dir_shape = jax.ShapeDtypeStruct((b, 2, t, db), BF16)
    nrest = 3 * db
    assert w_in.shape[1] == 8 * db
    tok_b = jax.ShapeDtypeStruct((b, t, db), BF16)
    return pl.pallas_call(
        functools.partial(_front_kernel, ctx=ctx),
        grid=(b, t // tm),
        in_specs=[pl.BlockSpec((None, tm, d), lambda bi, i: (bi, i, 0)),
                  pl.BlockSpec((None, SUBLANES, d), lambda bi, i: (bi, jnp.maximum(i * per - 1, 0), 0)),
                  pl.BlockSpec((None, SUBLANES, d), lambda bi, i: (bi, jnp.minimum((i + 1) * per, nb - 1), 0))]
        + [full(a) for a in consts],
        out_specs=[pl.BlockSpec((None, tm, nrest), lambda bi, i: (bi, i, 0)), tok, tok, tok, dirtok, dirtok,
                   pl.BlockSpec((None, 2, tm // CHUNK, db, 4 * CHUNK), lambda bi, i: (bi, 0, i, 0, 0)),
                   pl.BlockSpec((None, 2, tm // CHUNK, SUBLANES, db), lambda bi, i: (bi, 0, i, 0, 0))],
        out_shape=[jax.ShapeDtypeStruct((b, t, nrest), BF16), tok_b, tok_b, tok_b, dir_shape, dir_shape,
                   jax.ShapeDtypeStruct((b, 2, nc, db, 4 * CHUNK), BF16),
                   jax.ShapeDtypeStruct((b, 2, nc, SUBLANES, db), F32)],
        compiler_params=pltpu.CompilerParams(dimension_semantics=("parallel", "parallel"),
                                             vmem_limit_bytes=VMEM_LIMIT),
        name="front_ctx" if ctx else "front_latent",
    )(x, x, x, *consts)


SCAN_PARTS = 1
SCAN_CHUNKS_PER_STEP = 4


def _scan_kernel(at0, rt0, bk0, v0, gc0, at1, rt1, bk1, v1, gc1, s0_ref,
                 hm_ref, ms_ref, mi_ref, idw_ref, bdm_ref, eye_ref,
                 of_ref, or_ref, st_ref):
    @pl.when(pl.program_id(0) == 0)
    def _():
        st_ref[...] = s0_ref[...]

    npart = SCAN_PARTS
    pair = GROUP // 2
    bdm_b = bdm_ref[...]
    bd2 = bdm_b[0:pair, 0:pair].astype(F32)
    eye2 = eye_ref[0:pair, 0:pair]

    def bd(x):
        return [jnp.concatenate([part * hm_ref[h] for h in range(HEADS_PER_GROUP)], axis=0)
                for part in _split(x, npart)]

    def cat_parts(plists):
        return [jnp.concatenate(ps, axis=-1) for ps in zip(*plists)]

    steps = int(np.log2(CHUNK)) - 1

    def load(c):
        at_r, rt_r, bk_r, v_r, gc_r = c["refs"]
        bi, g, sl, q, rows = c["bi"], c["g"], c["sl"], c["q"], c["rows"]
        c.update(at=at_r[bi, rows, sl], rt=rt_r[bi, rows, sl], v=v_r[bi, rows, sl], gc=gc_r[bi, q, 0:1, sl],
                 bk=bk_r[bi, q, g * GROUP:(g + 1) * GROUP, :])

    def interactions(c):
        ms, mi = ms_ref[c["d"]], mi_ref[c["d"]]
        bk = c["bk"]
        wts = jnp.concatenate([jnp.concatenate([bk[:, 0:pair]] * 2, axis=1) * bdm_b,
                               jnp.concatenate([bk[:, pair:]] * 2, axis=1) * bdm_b], axis=1)
        gmat = _dot(jnp.concatenate([c["at"], c["rt"]], axis=0), wts, False)
        lab = gmat[0:CHUNK, 0:GROUP] * ms
        c.update(lak=gmat[0:CHUNK, GROUP:] * ms, mrb=gmat[CHUNK:, 0:GROUP] * mi,
                 mrk=gmat[CHUNK:, GROUP:] * mi, pw=lab, tinv=idw_ref[...] + lab)

    def invert(s):
        def stage(c):
            pw, tinv = c["pw"], c["tinv"]
            wts = bd(pw)
            if s == 0:
                pw = _mmp(_split(pw, npart), wts)
            elif s < steps:
                res = _mmp(_split(jnp.concatenate([pw, tinv], axis=0), npart), wts)
                pw, tinv = res[0:CHUNK], tinv + res[CHUNK:]
            else:
                tinv = tinv + _mmp(_split(tinv, npart), wts)
            c.update(pw=pw, tinv=tinv)
        return stage

    def values(c):
        lv_mv = _mmp(_split(jnp.concatenate([c["lak"], c["mrk"]], axis=0), npart), bd(c["v"]))
        c.update(lv=lv_mv[0:CHUNK], mv=lv_mv[CHUNK:])

    def solve(c):
        aw = _mmp(_split(c["tinv"], npart), cat_parts([bd(c["at"]), bd(c["lv"])]))
        c.update(ahat=aw[:, 0:GROUP], wv=aw[:, GROUP:])

    def outputs(c):
        ro = _mmp(_split(c["mrb"], npart), cat_parts([bd(c["ahat"]), bd(c["wv"])]))
        c.update(rhat=(c["rt"].astype(F32) + ro[:, 0:GROUP]) * c["gc"], ointra=ro[:, GROUP:] + c["mv"])

    def transition(c):
        zero_rows = jnp.zeros((CHUNK, GROUP), BF16)
        zero_half = jnp.zeros((CHUNK, pair), BF16)
        zero_blk = jnp.zeros((pair, pair), F32)
        pms, qms = [], []
        for p in range(2):
            ps = slice(p * pair, (p + 1) * pair)
            top = jnp.concatenate([c["ahat"][:, ps], c["wv"][:, ps]], axis=1).astype(BF16)
            mid = jnp.concatenate([zero_half, c["v"][:, ps]], axis=1)
            rhs = jnp.concatenate([top, zero_rows, mid, zero_rows], axis=0)
            pq = _dot(c["bk"][ps, :], rhs, False)
            pblk = (pq[:, 0:pair] * bd2 + eye2) * c["gc"][:, ps]
            qblk = pq[:, pair:] * bd2
            pms.append(jnp.concatenate([pblk, zero_blk] if p == 0 else [zero_blk, pblk], axis=1))
            qms.append(jnp.concatenate([qblk, zero_blk] if p == 0 else [zero_blk, qblk], axis=1))
        c.update(pm=jnp.concatenate(pms, axis=0), qm=jnp.concatenate(qms, axis=0))

    def advance(c, st):
        rs = _dot(jnp.concatenate([c["rhat"], c["pm"]], axis=0).astype(BF16), st.astype(BF16), False)
        c["out"][c["bi"], c["rows"], c["sl"]] = (rs[0:CHUNK] + c["ointra"]).astype(BF16)
        return rs[CHUNK:] + c["qm"]

    nq = at0.shape[1] // CHUNK
    dirs = ((at0, rt0, bk0, v0, gc0, of_ref), (at1, rt1, bk1, v1, gc1, or_ref))
    groups = {}
    for bi in range(at0.shape[0]):
        for d, refs in enumerate(dirs):
            for g in range(at0.shape[-1] // GROUP):
                order = range(nq) if d == 0 else reversed(range(nq))
                groups[(bi, d, g)] = [
                    dict(q=q, rows=slice(q * CHUNK, (q + 1) * CHUNK), bi=bi, d=d, g=g,
                         sl=slice(g * GROUP, (g + 1) * GROUP), refs=refs[:5], out=refs[5]) for q in order]
    chains = [c for k in range(nq) for cs in groups.values() for c in [cs[k]]]
    stages = ([load, interactions] + [invert(s) for s in range(steps + 1)]
              + [values, solve, outputs, transition])
    for stage in stages:
        for c in chains:
            stage(c)
    states = {key: st_ref[key[0], key[1], key[2]] for key in groups}
    for k in range(nq):
        for key, cs in groups.items():
            states[key] = advance(cs[k], states[key])
    for key in groups:
        st_ref[key[0], key[1], key[2]] = states[key]


def _scan_constants():
    t = np.arange(CHUNK)[:, None]
    lane = np.arange(GROUP)[None, :]
    s = lane % CHUNK
    strict = np.stack([s < t, s > t]).astype(np.float32)
    incl = np.stack([s <= t, s >= t]).astype(np.float32)
    idw = (s == t).astype(np.float32)
    hm = np.stack([np.broadcast_to((lane // RWKV_HEAD) == h, (CHUNK, GROUP)) for h in range(HEADS_PER_GROUP)])
    bdm = np.kron(np.eye(HEADS_PER_GROUP), np.ones((RWKV_HEAD, RWKV_HEAD)))
    return (jnp.asarray(hm.astype(np.float32), BF16), jnp.asarray(strict), jnp.asarray(incl), jnp.asarray(idw),
            jnp.asarray(bdm, BF16), jnp.eye(GROUP, dtype=F32))


def _rwkv_scan(at, rt, bk, v, gc, s0):
    b, _, t, db = at.shape
    nc = t // CHUNK
    consts = _scan_constants()
    full = lambda a: pl.BlockSpec(a.shape, lambda i: (0,) * a.ndim)
    nq = min(SCAN_CHUNKS_PER_STEP, nc)
    ns = nc // nq
    assert nc % nq == 0
    fwd, rev = (lambda i: i), (lambda i: ns - 1 - i)
    dspec = lambda d, ci: pl.BlockSpec((b, None, nq * CHUNK, db), lambda i: (0, d, ci(i), 0))
    kspec = lambda d, ci: pl.BlockSpec((b, None, nq, db, 4 * CHUNK), lambda i: (0, d, ci(i), 0, 0))
    vspec = lambda ci: pl.BlockSpec((b, nq * CHUNK, db), lambda i: (0, ci(i), 0))
    gspec = lambda d, ci: pl.BlockSpec((b, None, nq, SUBLANES, db), lambda i: (0, d, ci(i), 0, 0))
    in_specs = ([dspec(0, fwd), dspec(0, fwd), kspec(0, fwd), vspec(fwd), gspec(0, fwd),
                 dspec(1, rev), dspec(1, rev), kspec(1, rev), vspec(rev), gspec(1, rev), full(s0)]
                + [full(a) for a in consts])
    return pl.pallas_call(
        _scan_kernel,
        grid=(ns,),
        in_specs=in_specs,
        out_specs=[vspec(fwd), vspec(rev), full(s0)],
        out_shape=[jax.ShapeDtypeStruct((b, t, db), BF16), jax.ShapeDtypeStruct((b, t, db), BF16),
                   jax.ShapeDtypeStruct(s0.shape, F32)],
        compiler_params=pltpu.CompilerParams(dimension_semantics=("arbitrary",),
                                             vmem_limit_bytes=VMEM_LIMIT),
        name="rwkv_scan",
    )(at, rt, bk, v, gc, at, rt, bk, v, gc, s0, *consts)


def _conv_kernel(glu_ref, w_ref, b_ref, o_ref, h_ref, *, stride):
    t, c = glu_ref.shape
    halo = (CONF_CONV_W // 2) * stride
    nsteps = t // CONV_ROWS
    zeros = jnp.zeros((halo, c), F32)
    h_ref[0:halo, :] = zeros
    h_ref[halo + t:halo + t + halo, :] = zeros

    def widen(j, carry):
        r0 = pl.multiple_of(j * CONV_ROWS, CONV_ROWS)
        h_ref[pl.ds(halo + r0, CONV_ROWS), :] = glu_ref[pl.ds(r0, CONV_ROWS), :].astype(F32)
        return carry

    lax.fori_loop(0, nsteps, widen, 0)
    w = w_ref[...]
    bias = b_ref[...]

    def conv(j, carry):
        r0 = pl.multiple_of(j * CONV_ROWS, CONV_ROWS)
        acc = jnp.broadcast_to(bias, (CONV_ROWS, c))
        for tap in range(CONF_CONV_W):
            acc = acc + h_ref[pl.ds(r0 + tap * stride, CONV_ROWS), :] * w[tap:tap + 1, :]
        o_ref[pl.ds(r0, CONV_ROWS), :] = acc.astype(BF16)
        return carry

    lax.fori_loop(0, nsteps, conv, 0)


def _conformer_conv(proj, dw_w, dw_b, *, stride):
    b, t, n = proj.shape
    db = dw_w.shape[-1]
    lanes = 128
    assert t % CONV_ROWS == 0 and stride % SUBLANES == 0
    halo = (CONF_CONV_W // 2) * stride
    c1 = db // lanes
    return pl.pallas_call(
        functools.partial(_conv_kernel, stride=stride),
        grid=(b, db // lanes),
        in_specs=[pl.BlockSpec((None, t, lanes), lambda bi, j: (bi, 0, c1 + j)),
                  pl.BlockSpec((_round_up(CONF_CONV_W, SUBLANES), lanes), lambda bi, j: (0, j)),
                  pl.BlockSpec((1, lanes), lambda bi, j: (0, j))],
        out_specs=pl.BlockSpec((None, t, lanes), lambda bi, j: (bi, 0, j)),
        out_shape=jax.ShapeDtypeStruct((b, t, db), BF16),
        scratch_shapes=[pltpu.VMEM((t + 2 * halo, lanes), F32)],
        compiler_params=pltpu.CompilerParams(dimension_semantics=("parallel", "parallel"),
                                             vmem_limit_bytes=VMEM_LIMIT),
        name="conformer_conv",
    )(proj, _pad_to(dw_w, 0, _round_up(CONF_CONV_W, SUBLANES)), dw_b.reshape(1, db))


def _round_up(x, m):
    return (x + m - 1) // m * m


def _merge_kernel(x_ref, of_ref, or_ref, g_ref, bonus_ref, conv_ref, zc_ref, zd_ref, mod_ref,
                  lnxg_ref, lnxb_ref, clng_ref, clnb_ref, wout_ref, fg_ref, seg_ref, o_ref):
    tm, d = x_ref.shape
    mrow = _mod_row(mod_ref, False)
    seg = seg_ref[...]

    def centre(c):
        o = of_ref[c["rows"], :].astype(F32) + or_ref[c["rows"], :].astype(F32)
        c.update(dev=o - _mm(o, seg))

    def normalise(c):
        rows, dev = c["rows"], c["dev"]
        var = _mm(dev * dev, seg)
        on = dev * lax.rsqrt(var + LNX_EPS) * lnxg_ref[...] + lnxb_ref[...]
        y_c = ((on + bonus_ref[rows, :].astype(F32)) * g_ref[rows, :].astype(F32)
               * _silu(zc_ref[rows, :].astype(F32)))
        cv = conv_ref[rows, :].astype(F32)
        cdev = cv - jnp.mean(cv, axis=-1, keepdims=True)
        cvar = jnp.mean(cdev * cdev, axis=-1, keepdims=True)
        ln = cdev * lax.rsqrt(cvar + CONF_LN_EPS) * clng_ref[...] + clnb_ref[...]
        y_d = _silu(ln) * _silu(zd_ref[rows, :].astype(F32))
        c.update(y=jnp.concatenate([y_c, y_d], axis=-1))

    def emit(c):
        y = _mm(c["y"], wout_ref[...])
        o_ref[c["rows"], :] = _rmsnorm(x_ref[c["rows"], :] + mrow[:, 2 * d:3 * d] * y, fg_ref[...])

    ts = tm // MERGE_SUBTILES
    subs = [dict(rows=slice(s * ts, (s + 1) * ts)) for s in range(MERGE_SUBTILES)]
    for stage in (centre, normalise, emit):
        for c in subs:
            stage(c)


def _merge(x, o_f, o_r, g, bonus, conv, proj, mod, p, final_g):
    b, t, d = x.shape
    db = d // 2
    tm = min(TILE_TOKENS, t)
    heads = db // RWKV_HEAD
    seg = jnp.asarray(np.kron(np.eye(heads), np.full((RWKV_HEAD, RWKV_HEAD), 1.0 / RWKV_HEAD)), BF16)
    consts = (mod, p["lnx_g"].reshape(1, db), p["lnx_b"].reshape(1, db), p["conf_ln_g"].reshape(1, db),
              p["conf_ln_b"].reshape(1, db), p["out_o"].astype(BF16), final_g.reshape(1, d), seg)
    full = lambda a: pl.BlockSpec(a.shape, lambda bi, i: (0,) * a.ndim)
    tok = pl.BlockSpec((None, tm, db), lambda bi, i: (bi, i, 0))
    col = lambda cblk: pl.BlockSpec((None, tm, db), lambda bi, i: (bi, i, cblk))
    return pl.pallas_call(
        _merge_kernel,
        grid=(b, t // tm),
        in_specs=[pl.BlockSpec((None, tm, d), lambda bi, i: (bi, i, 0)), tok, tok, tok, tok, tok,
                  col(0), col(2)] + [full(a) for a in consts],
        out_specs=pl.BlockSpec((None, tm, d), lambda bi, i: (bi, i, 0)),
        out_shape=jax.ShapeDtypeStruct((b, t, d), F32),
        compiler_params=pltpu.CompilerParams(dimension_semantics=("parallel", "parallel"),
                                             vmem_limit_bytes=VMEM_LIMIT),
        name="merge",
    )(x, o_f, o_r, g, bonus, conv, proj, proj, *consts)


def kernel(x, c, ctx, c_ctx, ada_w_e, ada_b_e, norm_e, in_e, out_e, pool_w, pool_scale, sconv_w,
           ada_w_o, ada_b_o, norm_o, in_o, out_o, rwkv_mu, w0, w1, w2, a0, a1, a2, g1, g2,
           k_k, k_a, r_k, lnx_g, lnx_b, conf_dw_w, conf_dw_b, conf_ln_g, conf_ln_b, final_g):
    b, t, d = x.shape
    db = d // 2
    assert b == 2
    cvec = _pad_to(jnp.concatenate([c, c_ctx[None, :]], axis=0), 0, SUBLANES)
    mod_e = _adaln(cvec, ada_w_e[0], ada_b_e[0])
    mod_o = _adaln(cvec, ada_w_o[0], ada_b_o[0])

    even = dict(norm_g=norm_e[0], w_in=in_e[0], w_out=out_e[0], pool_w=pool_w[0],
                pool_scale=pool_scale[0], sconv_w=sconv_w[0])
    x1 = _even_layer(x, mod_e, ctx=False, line=GRID_W, **even)
    xc1 = _even_layer(ctx, mod_e, ctx=True, line=ctx.shape[1], **even)

    p = dict(rwkv_mu=rwkv_mu[0], w0=w0[0], w1=w1[0], w2=w2[0], a0=a0[0], a1=a1[0], a2=a2[0],
             g1=g1[0], g2=g2[0], k_k=k_k[0], k_a=k_a[0], r_k=r_k[0], lnx_g=lnx_g[0], lnx_b=lnx_b[0],
             conf_ln_g=conf_ln_g[0], conf_ln_b=conf_ln_b[0], out_o=out_o[0])

    _, v_c, _, _, at_c, rt_c, bk_c, gc_c = _odd_front(xc1, mod_o, norm_o[0], in_o[0], p, ctx=True)
    zero_state = jnp.zeros((b, 2, db // GROUP, GROUP, GROUP), F32)
    _, _, ctx_state = _rwkv_scan(at_c, rt_c, bk_c, v_c, gc_c, zero_state)

    proj, v_l, g_l, bonus, at_l, rt_l, bk_l, gc_l = _odd_front(x1, mod_o, norm_o[0], in_o[0], p, ctx=False)
    o_f, o_r, _ = _rwkv_scan(at_l, rt_l, bk_l, v_l, gc_l, ctx_state)
    conv = _conformer_conv(proj, conf_dw_w[0], conf_dw_b[0], stride=GRID_W)
    return _merge(x1, o_f, o_r, g_l, bonus, conv, proj, mod_o, p, final_g)
```

```python
import functools

import numpy as np
import jax
import jax.numpy as jnp
from jax import lax
from jax.experimental import pallas as pl
from jax.experimental.pallas import tpu as pltpu

F32 = jnp.float32
BF16 = jnp.bfloat16

GRID_W = 64
POOL_WINDOWS = (2, 4, 8, 16)
SHORT_CONV_W = 3
CONF_CONV_W = 31
RWKV_HEAD = 64
NORM_EPS = 1e-6
LNX_EPS = 64e-5
CONF_LN_EPS = 1e-5
DECAY_SCALE = float(np.exp(-0.5))

CHUNK = 64
HEADS_PER_GROUP = 4
GROUP = HEADS_PER_GROUP * RWKV_HEAD
LORA_PAD = 128
SUBLANES = 8
VMEM_LIMIT = 52 * 1024 * 1024

TILE_TOKENS = 256
PREP_TOKENS = 128
CONV_ROWS = 128
MERGE_SUBTILES = 4


def _sigmoid(x):
    return 0.5 * jnp.tanh(0.5 * x) + 0.5


def _silu(x):
    return x * _sigmoid(x)


def _split(x, n):
    if x.dtype == BF16:
        return [x]
    parts, r = [], x
    for i in range(n):
        p = r.astype(BF16)
        parts.append(p)
        if i + 1 < n:
            r = r - p.astype(F32)
    return parts


def _dot(a, b, nt):
    dims = (((1,), (1,)), ((), ())) if nt else (((1,), (0,)), ((), ()))
    return lax.dot_general(a, b, dims, preferred_element_type=F32)


def _mmp(ap, bp, nt=False):
    order = max(len(ap), len(bp))
    acc = None
    for i in reversed(range(len(ap))):
        for j in reversed(range(len(bp))):
            if i + j < order:
                t = _dot(ap[i], bp[j], nt)
                acc = t if acc is None else acc + t
    return acc


def _mm(a, b, na=1, nb=1, nt=False):
    return _mmp(_split(a, na), _split(b, nb), nt)


def _rmsnorm(x, g):
    ms = jnp.mean(x * x, axis=-1, keepdims=True)
    return x * lax.rsqrt(ms + NORM_EPS) * g


def _modulated(x, g, mod_row, d):
    shift, scale = mod_row[:, 0:d], mod_row[:, d:2 * d]
    return _rmsnorm(x, g) * (1.0 + scale) + shift


def _mod_row(mod_ref, ctx):
    row = 2 if ctx else pl.program_id(0)
    return mod_ref[pl.ds(row, 1), :]


def _adaln_kernel(c_ref, w_ref, b_ref, o_ref):
    o_ref[...] = _mm(_silu(c_ref[...]), w_ref[...], 2, 2) + b_ref[...]


def _adaln(cvec, w, b):
    d, n = w.shape
    tn = 768
    return pl.pallas_call(
        _adaln_kernel,
        grid=(n // tn,),
        in_specs=[pl.BlockSpec((SUBLANES, d), lambda j: (0, 0)),
                  pl.BlockSpec((d, tn), lambda j: (0, j)),
                  pl.BlockSpec((1, tn), lambda j: (0, j))],
        out_specs=pl.BlockSpec((SUBLANES, tn), lambda j: (0, j)),
        out_shape=jax.ShapeDtypeStruct((SUBLANES, n), F32),
        compiler_params=pltpu.CompilerParams(dimension_semantics=("parallel",),
                                             vmem_limit_bytes=VMEM_LIMIT),
        name="adaln",
    )(cvec, w, b.reshape(1, n))


def _even_kernel(x_ref, mod_ref, g_ref, win_ref, wout_ref, poolw_ref, pscale_ref, sconv_ref,
                 band_ref, invc_ref, o_ref, *, ctx, line):
    tm, d = x_ref.shape
    ts = band_ref.shape[-1]
    db = d // 2
    grp = db // len(POOL_WINDOWS)
    mrow = _mod_row(mod_ref, ctx)

    def project(c):
        x = x_ref[c["rows"], :]
        c.update(x=x, proj=_mm(_modulated(x, g_ref[...], mrow, d), win_ref[...]))

    def pool(c):
        u_a = c["proj"][:, 0:db]
        pooled = []
        for i in range(len(POOL_WINDOWS)):
            ug = u_a[:, i * grp:(i + 1) * grp]
            pooled.append(_mm(band_ref[i], ug) * invc_ref[i] - ug)
        c.update(pooled=pooled)

    def mix(c):
        proj = c["proj"]
        z_a = proj[:, db:2 * db]
        v_b, g_b = proj[:, 2 * db:3 * db], proj[:, 3 * db:4 * db]
        g_c, z_b = proj[:, 4 * db:5 * db], proj[:, 5 * db:6 * db]
        ya = [_mm(c["pooled"][i], poolw_ref[i]) for i in range(len(POOL_WINDOWS))]
        y_a = jnp.concatenate(ya, axis=-1) * pscale_ref[...] * _silu(z_a)
        q = g_c * v_b
        pos = lax.broadcasted_iota(jnp.int32, q.shape, 0) % line
        q_prev = jnp.where(pos >= 1, pltpu.roll(q, 1, 0), 0.0)
        q_next = jnp.where(pos <= line - 2, pltpu.roll(q, ts - 1, 0), 0.0)
        w = sconv_ref[...]
        conv = q_prev * w[0:1, :] + q * w[1:2, :] + q_next * w[2:3, :]
        y_b = g_b * conv * _silu(z_b)
        c.update(y=jnp.concatenate([y_a, y_b], axis=-1))

    def emit(c):
        o_ref[c["rows"], :] = c["x"] + mrow[:, 2 * d:3 * d] * _mm(c["y"], wout_ref[...])

    subs = [dict(rows=slice(s * ts, (s + 1) * ts)) for s in range(tm // ts)]
    for stage in (project, pool, mix, emit):
        for c in subs:
            stage(c)


def _pool_constants(tm, line):
    pos = np.arange(tm) % line
    lid = np.arange(tm) // line
    bands, invs = [], []
    for w in POOL_WINDOWS:
        lo = np.clip(pos - w // 2, 0, line)
        hi = np.clip(pos - w // 2 + w, 0, line)
        same = lid[:, None] == lid[None, :]
        band = same & (pos[None, :] >= lo[:, None]) & (pos[None, :] < hi[:, None])
        bands.append(band.astype(np.float32))
        invs.append(np.broadcast_to((1.0 / (hi - lo))[:, None], (tm, 128)).astype(np.float32))
    return jnp.asarray(np.stack(bands), BF16), jnp.asarray(np.stack(invs), F32)


def _even_layer(x, mod, norm_g, w_in, w_out, pool_w, pool_scale, sconv_w, *, ctx, line):
    b, t, d = x.shape
    tm = min(2 * TILE_TOKENS, t)
    assert t % tm == 0 and tm % line == 0
    db = d // 2
    ts = max(PREP_TOKENS, line)
    assert tm % ts == 0 and ts % line == 0
    band, invc = _pool_constants(ts, line)
    full = lambda a: pl.BlockSpec(a.shape, lambda bi, i: (0,) * a.ndim)
    args = (mod, norm_g.reshape(1, d), w_in.astype(BF16), w_out.astype(BF16), pool_w.astype(BF16),
            pool_scale.reshape(1, db), sconv_w, band, invc)
    return pl.pallas_call(
        functools.partial(_even_kernel, ctx=ctx, line=line),
        grid=(b, t // tm),
        in_specs=[pl.BlockSpec((None, tm, d), lambda bi, i: (bi, i, 0))] + [full(a) for a in args],
        out_specs=pl.BlockSpec((None, tm, d), lambda bi, i: (bi, i, 0)),
        out_shape=jax.ShapeDtypeStruct((b, t, d), F32),
        compiler_params=pltpu.CompilerParams(dimension_semantics=("parallel", "parallel"),
                                             vmem_limit_bytes=VMEM_LIMIT),
        name="even_ctx" if ctx else "even_latent",
    )(x, *args)


def _front_kernel(x_ref, xp_ref, xn_ref, mod_ref, ng_ref, win_ref,
                  mu_ref, w0_ref, a0_ref, w1_ref, w2_ref, a1_ref, a2_ref,
                  g1_ref, g2_ref, kk_ref, ka_ref, rk_ref, seg_ref, later_ref, whole_ref,
                  rest_ref, v_ref, g_ref, bonus_ref, at_ref, rt_ref, bk_ref, gc_ref, *, ctx):
    tm, d = x_ref.shape
    db = v_ref.shape[-1]
    ts = PREP_TOKENS
    te = ts + 2 * SUBLANES
    i, n = pl.program_id(1), pl.num_programs(1)
    mrow = _mod_row(mod_ref, ctx)
    h_prev = _modulated(xp_ref[...], ng_ref[...], mrow, d) * jnp.where(i > 0, 1.0, 0.0)
    h_next = _modulated(xn_ref[...], ng_ref[...], mrow, d) * jnp.where(i < n - 1, 1.0, 0.0)
    h = _modulated(x_ref[...], ng_ref[...], mrow, d)
    h_b = h.astype(BF16)
    h_ext = jnp.concatenate([h_prev, h, h_next], axis=0).astype(BF16)
    mu = mu_ref[...]
    first_chunk = lax.broadcasted_iota(jnp.int32, (db, ts), 1) < CHUNK

    def rest(c, j):
        rows = c["rows"]
        if j == 1:
            p12 = _dot(h_b[rows], win_ref[:, 5 * db:7 * db], False)
            rest_ref[rows, db:2 * db] = (p12[:, 0:db] * _sigmoid(p12[:, db:])).astype(BF16)
        else:
            col = 4 if j == 0 else 7
            rest_ref[rows, j * db:(j + 1) * db] = _dot(h_b[rows], win_ref[:, col * db:(col + 1) * db],
                                                        False).astype(BF16)

    def project(c):
        c.update(z_ext=_dot(h_ext[c["s"] * ts:c["s"] * ts + te], win_ref[:, 0:4 * db], False))

    def shift(c):
        z_ext = c["z_ext"]
        nbr = (pltpu.roll(z_ext, 1, 0) + pltpu.roll(z_ext, te - 1, 0))[SUBLANES:ts + SUBLANES]
        z = z_ext[SUBLANES:ts + SUBLANES]
        dz = 0.5 * nbr - z
        u, du = z[:, 0:db], dz[:, 0:db]
        v = z[:, 3 * db:4 * db] + dz[:, 3 * db:4 * db] * mu[2:3, :]
        c.update(r=z[:, db:2 * db] + dz[:, db:2 * db] * mu[0:1, :],
                 k=z[:, 2 * db:3 * db] + dz[:, 2 * db:3 * db] * mu[1:2, :], v=v,
                 uw=u + du * mu[3:4, :], ua=u + du * mu[4:5, :], ug=u + du * mu[5:6, :])
        v_ref[c["rows"], :] = v.astype(BF16)
        rest(c, 0)

    def loras(c):
        g_ref[c["rows"], :] = _mm(_sigmoid(_mm(c["ug"], g1_ref[...])), g2_ref[...]).astype(BF16)
        kk = c["k"] * kk_ref[...]
        inv_norm = lax.rsqrt(jnp.maximum(_mm(kk * kk, seg_ref[...]), 1e-24))
        c.update(lw_pre=w0_ref[...] + _mm(jnp.tanh(_mm(c["uw"], w1_ref[...])), w2_ref[...]),
                 a_pre=a0_ref[...] + _mm(_mm(c["ua"], a1_ref[...]), a2_ref[...]),
                 kk=kk * inv_norm, ksum=None)
        rest(c, 1)

    def direction(dr):
        def stage(c):
            rows, k, kk = c["rows"], c["k"], c["kk"]
            lw = -DECAY_SCALE * _sigmoid(c["lw_pre"][:, dr * db:(dr + 1) * db])
            a = _sigmoid(c["a_pre"][:, dr * db:(dr + 1) * db])
            k_d = k * (1.0 + (a - 1.0) * ka_ref[...])
            c["ksum"] = k_d if c["ksum"] is None else c["ksum"] + k_d
            lw_parts = _split(lw, 2)
            rel = -_mmp([later_ref[dr]], lw_parts)
            at_ref[dr, rows, :] = (-kk * jnp.exp(rel - lw)).astype(BF16)
            rt_ref[dr, rows, :] = (c["r"] * jnp.exp(rel)).astype(BF16)
            gout = jnp.exp(-rel)
            bh_t = (kk * a * gout).T
            kh_t = (k_d * gout).T
            bh_r, kh_r = pltpu.roll(bh_t, CHUNK, 1), pltpu.roll(kh_t, CHUNK, 1)
            gam = jnp.exp(_mmp([whole_ref[...]], lw_parts))
            for q in range(ts // CHUNK):
                own = first_chunk if q == 0 else jnp.logical_not(first_chunk)
                chunk = c["s"] * (ts // CHUNK) + q
                bk_ref[dr, chunk] = jnp.concatenate([jnp.where(own, bh_t, bh_r), jnp.where(own, kh_t, kh_r)],
                                                    axis=1).astype(BF16)
                gc_ref[dr, chunk] = gam[q * SUBLANES:(q + 1) * SUBLANES, :]
            if dr == 0:
                rest(c, 2)
        return stage

    def bonus(c):
        bonus_ref[c["rows"], :] = (_mm(c["r"] * c["ksum"] * rk_ref[...], seg_ref[...]) * c["v"]).astype(BF16)

    subs = [dict(s=s, rows=slice(s * ts, (s + 1) * ts)) for s in range(tm // ts)]
    for stage in (project, shift, loras, direction(0), direction(1), bonus):
        for c in subs:
            stage(c)


def _prep_constants(tm):
    t = np.arange(tm)
    same = (t[:, None] // CHUNK) == (t[None, :] // CHUNK)
    later = np.stack([same & (t[None, :] > t[:, None]), same & (t[None, :] < t[:, None])]).astype(np.float32)
    whole = (np.arange(tm // CHUNK * SUBLANES)[:, None] // SUBLANES) == (t[None, :] // CHUNK)
    return jnp.asarray(later, BF16), jnp.asarray(whole.astype(np.float32), BF16)


def _pad_to(a, axis, size):
    pad = [(0, 0)] * a.ndim
    pad[axis] = (0, size - a.shape[axis])
    return jnp.pad(a, pad)


def _odd_front(x, mod, norm_g, w_in, p, *, ctx):
    b, t, d = x.shape
    db = p["k_k"].shape[-1]
    tm = min(2 * TILE_TOKENS, t)
    assert t % tm == 0 and tm % PREP_TOKENS == 0 and PREP_TOKENS == 2 * CHUNK
    nc = t // CHUNK
    nb = t // SUBLANES
    per = tm // SUBLANES
    later, whole = _prep_constants(PREP_TOKENS)
    heads = db // RWKV_HEAD
    seg = jnp.asarray(np.kron(np.eye(heads), np.ones((RWKV_HEAD, RWKV_HEAD))), BF16)

    def lora_pair(w_a, w_b):
        rank = w_a.shape[-1]
        down = _pad_to(jnp.concatenate([w_a[0], w_a[1]], axis=-1), 1, LORA_PAD)
        up = jnp.zeros((LORA_PAD, 2 * db), F32)
        up = up.at[0:rank, 0:db].set(w_b[0]).at[rank:2 * rank, db:2 * db].set(w_b[1])
        return down.astype(BF16), up.astype(BF16)

    w1c, w2c = lora_pair(p["w1"], p["w2"])
    a1c, a2c = lora_pair(p["a1"], p["a2"])
    consts = (mod, norm_g.reshape(1, d), w_in.astype(BF16),
              _pad_to(p["rwkv_mu"], 0, SUBLANES), p["w0"].reshape(1, 2 * db), p["a0"].reshape(1, 2 * db),
              w1c, w2c, a1c, a2c,
              _pad_to(p["g1"], 1, LORA_PAD).astype(BF16), _pad_to(p["g2"], 0, LORA_PAD).astype(BF16),
              p["k_k"].reshape(1, db), p["k_a"].reshape(1, db), p["r_k"].reshape(1, db), seg, later, whole)
    full = lambda a: pl.BlockSpec(a.shape, lambda bi, i: (0,) * a.ndim)
    tok = pl.BlockSpec((None, tm, db), lambda bi, i: (bi, i, 0))
    dirtok = pl.BlockSpec((None, 2, tm, db), lambda bi, i: (bi, 0, i, 0))
    tok_shape = jax.ShapeDtypeStruct((b, t, db), F32)
    dir_shape = jax.ShapeDtypeStruct((b, 2, t, db), BF16)
    nrest = 3 * db
    assert w_in.shape[1] == 8 * db
    tok_b = jax.ShapeDtypeStruct((b, t, db), BF16)
    return pl.pallas_call(
        functools.partial(_front_kernel, ctx=ctx),
        grid=(b, t // tm),
        in_specs=[pl.BlockSpec((None, tm, d), lambda bi, i: (bi, i, 0)),
                  pl.BlockSpec((None, SUBLANES, d), lambda bi, i: (bi, jnp.maximum(i * per - 1, 0), 0)),
                  pl.BlockSpec((None, SUBLANES, d), lambda bi, i: (bi, jnp.minimum((i + 1) * per, nb - 1), 0))]
        + [full(a) for a in consts],
        out_specs=[pl.BlockSpec((None, tm, nrest), lambda bi, i: (bi, i, 0)), tok, tok, tok, dirtok, dirtok,
                   pl.BlockSpec((None, 2, tm // CHUNK, db, 4 * CHUNK), lambda bi, i: (bi, 0, i, 0, 0)),
                   pl.BlockSpec((None, 2, tm // CHUNK, SUBLANES, db), lambda bi, i: (bi, 0, i, 0, 0))],
        out_shape=[jax.ShapeDtypeStruct((b, t, nrest), BF16), tok_b, tok_b, tok_b, dir_shape, dir_shape,
                   jax.ShapeDtypeStruct((b, 2, nc, db, 4 * CHUNK), BF16),
                   jax.ShapeDtypeStruct((b, 2, nc, SUBLANES, db), F32)],
        compiler_params=pltpu.CompilerParams(dimension_semantics=("parallel", "parallel"),
                                             vmem_limit_bytes=VMEM_LIMIT),
        name="front_ctx" if ctx else "front_latent",
    )(x, x, x, *consts)


SCAN_PARTS = 1
SCAN_CHUNKS_PER_STEP = 4


def _scan_kernel(at0, rt0, bk0, v0, gc0, at1, rt1, bk1, v1, gc1, s0_ref,
                 hm_ref, ms_ref, mi_ref, idw_ref, bdm_ref, eye_ref,
                 of_ref, or_ref, st_ref):
    @pl.when(pl.program_id(0) == 0)
    def _():
        st_ref[...] = s0_ref[...]

    npart = SCAN_PARTS
    pair = GROUP // 2
    bdm_b = bdm_ref[...]
    bd2 = bdm_b[0:pair, 0:pair].astype(F32)
    eye2 = eye_ref[0:pair, 0:pair]

    def bd(x):
        return [jnp.concatenate([part * hm_ref[h] for h in range(HEADS_PER_GROUP)], axis=0)
                for part in _split(x, npart)]

    def cat_parts(plists):
        return [jnp.concatenate(ps, axis=-1) for ps in zip(*plists)]

    steps = int(np.log2(CHUNK)) - 1

    def load(c):
        at_r, rt_r, bk_r, v_r, gc_r = c["refs"]
        bi, g, sl, q, rows = c["bi"], c["g"], c["sl"], c["q"], c["rows"]
        c.update(at=at_r[bi, rows, sl], rt=rt_r[bi, rows, sl], v=v_r[bi, rows, sl], gc=gc_r[bi, q, 0:1, sl],
                 bk=bk_r[bi, q, g * GROUP:(g + 1) * GROUP, :])

    def interactions(c):
        ms, mi = ms_ref[c["d"]], mi_ref[c["d"]]
        bk = c["bk"]
        wts = jnp.concatenate([jnp.concatenate([bk[:, 0:pair]] * 2, axis=1) * bdm_b,
                               jnp.concatenate([bk[:, pair:]] * 2, axis=1) * bdm_b], axis=1)
        gmat = _dot(jnp.concatenate([c["at"], c["rt"]], axis=0), wts, False)
        lab = gmat[0:CHUNK, 0:GROUP] * ms
        c.update(lak=gmat[0:CHUNK, GROUP:] * ms, mrb=gmat[CHUNK:, 0:GROUP] * mi,
                 mrk=gmat[CHUNK:, GROUP:] * mi, pw=lab, tinv=idw_ref[...] + lab)

    def invert(s):
        def stage(c):
            pw, tinv = c["pw"], c["tinv"]
            wts = bd(pw)
            if s == 0:
                pw = _mmp(_split(pw, npart), wts)
            elif s < steps:
                res = _mmp(_split(jnp.concatenate([pw, tinv], axis=0), npart), wts)
                pw, tinv = res[0:CHUNK], tinv + res[CHUNK:]
            else:
                tinv = tinv + _mmp(_split(tinv, npart), wts)
            c.update(pw=pw, tinv=tinv)
        return stage

    def values(c):
        lv_mv = _mmp(_split(jnp.concatenate([c["lak"], c["mrk"]], axis=0), npart), bd(c["v"]))
        c.update(lv=lv_mv[0:CHUNK], mv=lv_mv[CHUNK:])

    def solve(c):
        aw = _mmp(_split(c["tinv"], npart), cat_parts([bd(c["at"]), bd(c["lv"])]))
        c.update(ahat=aw[:, 0:GROUP], wv=aw[:, GROUP:])

    def outputs(c):
        ro = _mmp(_split(c["mrb"], npart), cat_parts([bd(c["ahat"]), bd(c["wv"])]))
        c.update(rhat=(c["rt"].astype(F32) + ro[:, 0:GROUP]) * c["gc"], ointra=ro[:, GROUP:] + c["mv"])

    def transition(c):
        zero_rows = jnp.zeros((CHUNK, GROUP), BF16)
        zero_half = jnp.zeros((CHUNK, pair), BF16)
        zero_blk = jnp.zeros((pair, pair), F32)
        pms, qms = [], []
        for p in range(2):
            ps = slice(p * pair, (p + 1) * pair)
            top = jnp.concatenate([c["ahat"][:, ps], c["wv"][:, ps]], axis=1).astype(BF16)
            mid = jnp.concatenate([zero_half, c["v"][:, ps]], axis=1)
            rhs = jnp.concatenate([top, zero_rows, mid, zero_rows], axis=0)
            pq = _dot(c["bk"][ps, :], rhs, False)
            pblk = (pq[:, 0:pair] * bd2 + eye2) * c["gc"][:, ps]
            qblk = pq[:, pair:] * bd2
            pms.append(jnp.concatenate([pblk, zero_blk] if p == 0 else [zero_blk, pblk], axis=1))
            qms.append(jnp.concatenate([qblk, zero_blk] if p == 0 else [zero_blk, qblk], axis=1))
        c.update(pm=jnp.concatenate(pms, axis=0), qm=jnp.concatenate(qms, axis=0))

    def advance(c, st):
        rs = _dot(jnp.concatenate([c["rhat"], c["pm"]], axis=0).astype(BF16), st.astype(BF16), False)
        c["out"][c["bi"], c["rows"], c["sl"]] = (rs[0:CHUNK] + c["ointra"]).astype(BF16)
        return rs[CHUNK:] + c["qm"]

    nq = at0.shape[1] // CHUNK
    dirs = ((at0, rt0, bk0, v0, gc0, of_ref), (at1, rt1, bk1, v1, gc1, or_ref))
    groups = {}
    for bi in range(at0.shape[0]):
        for d, refs in enumerate(dirs):
            for g in range(at0.shape[-1] // GROUP):
                order = range(nq) if d == 0 else reversed(range(nq))
                groups[(bi, d, g)] = [
                    dict(q=q, rows=slice(q * CHUNK, (q + 1) * CHUNK), bi=bi, d=d, g=g,
                         sl=slice(g * GROUP, (g + 1) * GROUP), refs=refs[:5], out=refs[5]) for q in order]
    chains = [c for k in range(nq) for cs in groups.values() for c in [cs[k]]]
    stages = ([load, interactions] + [invert(s) for s in range(steps + 1)]
              + [values, solve, outputs, transition])
    for stage in stages:
        for c in chains:
            stage(c)
    states = {key: st_ref[key[0], key[1], key[2]] for key in groups}
    for k in range(nq):
        for key, cs in groups.items():
            states[key] = advance(cs[k], states[key])
    for key in groups:
        st_ref[key[0], key[1], key[2]] = states[key]


def _scan_constants():
    t = np.arange(CHUNK)[:, None]
    lane = np.arange(GROUP)[None, :]
    s = lane % CHUNK
    strict = np.stack([s < t, s > t]).astype(np.float32)
    incl = np.stack([s <= t, s >= t]).astype(np.float32)
    idw = (s == t).astype(np.float32)
    hm = np.stack([np.broadcast_to((lane // RWKV_HEAD) == h, (CHUNK, GROUP)) for h in range(HEADS_PER_GROUP)])
    bdm = np.kron(np.eye(HEADS_PER_GROUP), np.ones((RWKV_HEAD, RWKV_HEAD)))
    return (jnp.asarray(hm.astype(np.float32), BF16), jnp.asarray(strict), jnp.asarray(incl), jnp.asarray(idw),
            jnp.asarray(bdm, BF16), jnp.eye(GROUP, dtype=F32))


def _rwkv_scan(at, rt, bk, v, gc, s0):
    b, _, t, db = at.shape
    nc = t // CHUNK
    consts = _scan_constants()
    full = lambda a: pl.BlockSpec(a.shape, lambda i: (0,) * a.ndim)
    nq = min(SCAN_CHUNKS_PER_STEP, nc)
    ns = nc // nq
    assert nc % nq == 0
    fwd, rev = (lambda i: i), (lambda i: ns - 1 - i)
    dspec = lambda d, ci: pl.BlockSpec((b, None, nq * CHUNK, db), lambda i: (0, d, ci(i), 0))
    kspec = lambda d, ci: pl.BlockSpec((b, None, nq, db, 4 * CHUNK), lambda i: (0, d, ci(i), 0, 0))
    vspec = lambda ci: pl.BlockSpec((b, nq * CHUNK, db), lambda i: (0, ci(i), 0))
    gspec = lambda d, ci: pl.BlockSpec((b, None, nq, SUBLANES, db), lambda i: (0, d, ci(i), 0, 0))
    in_specs = ([dspec(0, fwd), dspec(0, fwd), kspec(0, fwd), vspec(fwd), gspec(0, fwd),
                 dspec(1, rev), dspec(1, rev), kspec(1, rev), vspec(rev), gspec(1, rev), full(s0)]
                + [full(a) for a in consts])
    return pl.pallas_call(
        _scan_kernel,
        grid=(ns,),
        in_specs=in_specs,
        out_specs=[vspec(fwd), vspec(rev), full(s0)],
        out_shape=[jax.ShapeDtypeStruct((b, t, db), BF16), jax.ShapeDtypeStruct((b, t, db), BF16),
                   jax.ShapeDtypeStruct(s0.shape, F32)],
        compiler_params=pltpu.CompilerParams(dimension_semantics=("arbitrary",),
                                             vmem_limit_bytes=VMEM_LIMIT),
        name="rwkv_scan",
    )(at, rt, bk, v, gc, at, rt, bk, v, gc, s0, *consts)


def _conv_kernel(glu_ref, w_ref, b_ref, o_ref, h_ref, *, stride):
    t, c = glu_ref.shape
    halo = (CONF_CONV_W // 2) * stride
    nsteps = t // CONV_ROWS
    zeros = jnp.zeros((halo, c), F32)
    h_ref[0:halo, :] = zeros
    h_ref[halo + t:halo + t + halo, :] = zeros

    def widen(j, carry):
        r0 = pl.multiple_of(j * CONV_ROWS, CONV_ROWS)
        h_ref[pl.ds(halo + r0, CONV_ROWS), :] = glu_ref[pl.ds(r0, CONV_ROWS), :].astype(F32)
        return carry

    lax.fori_loop(0, nsteps, widen, 0)
    w = w_ref[...]
    bias = b_ref[...]

    def conv(j, carry):
        r0 = pl.multiple_of(j * CONV_ROWS, CONV_ROWS)
        acc = jnp.broadcast_to(bias, (CONV_ROWS, c))
        for tap in range(CONF_CONV_W):
            acc = acc + h_ref[pl.ds(r0 + tap * stride, CONV_ROWS), :] * w[tap:tap + 1, :]
        o_ref[pl.ds(r0, CONV_ROWS), :] = acc.astype(BF16)
        return carry

    lax.fori_loop(0, nsteps, conv, 0)


def _conformer_conv(proj, dw_w, dw_b, *, stride):
    b, t, n = proj.shape
    db = dw_w.shape[-1]
    lanes = 128
    assert t % CONV_ROWS == 0 and stride % SUBLANES == 0
    halo = (CONF_CONV_W // 2) * stride
    c1 = db // lanes
    return pl.pallas_call(
        functools.partial(_conv_kernel, stride=stride),
        grid=(b, db // lanes),
        in_specs=[pl.BlockSpec((None, t, lanes), lambda bi, j: (bi, 0, c1 + j)),
                  pl.BlockSpec((_round_up(CONF_CONV_W, SUBLANES), lanes), lambda bi, j: (0, j)),
                  pl.BlockSpec((1, lanes), lambda bi, j: (0, j))],
        out_specs=pl.BlockSpec((None, t, lanes), lambda bi, j: (bi, 0, j)),
        out_shape=jax.ShapeDtypeStruct((b, t, db), BF16),
        scratch_shapes=[pltpu.VMEM((t + 2 * halo, lanes), F32)],
        compiler_params=pltpu.CompilerParams(dimension_semantics=("parallel", "parallel"),
                                             vmem_limit_bytes=VMEM_LIMIT),
        name="conformer_conv",
    )(proj, _pad_to(dw_w, 0, _round_up(CONF_CONV_W, SUBLANES)), dw_b.reshape(1, db))


def _round_up(x, m):
    return (x + m - 1) // m * m


def _merge_kernel(x_ref, of_ref, or_ref, g_ref, bonus_ref, conv_ref, zc_ref, zd_ref, mod_ref,
                  lnxg_ref, lnxb_ref, clng_ref, clnb_ref, wout_ref, fg_ref, seg_ref, o_ref):
    tm, d = x_ref.shape
    mrow = _mod_row(mod_ref, False)
    seg = seg_ref[...]

    def centre(c):
        o = of_ref[c["rows"], :].astype(F32) + or_ref[c["rows"], :].astype(F32)
        c.update(dev=o - _mm(o, seg))

    def normalise(c):
        rows, dev = c["rows"], c["dev"]
        var = _mm(dev * dev, seg)
        on = dev * lax.rsqrt(var + LNX_EPS) * lnxg_ref[...] + lnxb_ref[...]
        y_c = ((on + bonus_ref[rows, :].astype(F32)) * g_ref[rows, :].astype(F32)
               * _silu(zc_ref[rows, :].astype(F32)))
        cv = conv_ref[rows, :].astype(F32)
        cdev = cv - jnp.mean(cv, axis=-1, keepdims=True)
        cvar = jnp.mean(cdev * cdev, axis=-1, keepdims=True)
        ln = cdev * lax.rsqrt(cvar + CONF_LN_EPS) * clng_ref[...] + clnb_ref[...]
        y_d = _silu(ln) * _silu(zd_ref[rows, :].astype(F32))
        c.update(y=jnp.concatenate([y_c, y_d], axis=-1))

    def emit(c):
        y = _mm(c["y"], wout_ref[...])
        o_ref[c["rows"], :] = _rmsnorm(x_ref[c["rows"], :] + mrow[:, 2 * d:3 * d] * y, fg_ref[...])

    ts = tm // MERGE_SUBTILES
    subs = [dict(rows=slice(s * ts, (s + 1) * ts)) for s in range(MERGE_SUBTILES)]
    for stage in (centre, normalise, emit):
        for c in subs:
            stage(c)


def _merge(x, o_f, o_r, g, bonus, conv, proj, mod, p, final_g):
    b, t, d = x.shape
    db = d // 2
    tm = min(2 * TILE_TOKENS, t)
    heads = db // RWKV_HEAD
    seg = jnp.asarray(np.kron(np.eye(heads), np.full((RWKV_HEAD, RWKV_HEAD), 1.0 / RWKV_HEAD)), BF16)
    consts = (mod, p["lnx_g"].reshape(1, db), p["lnx_b"].reshape(1, db), p["conf_ln_g"].reshape(1, db),
              p["conf_ln_b"].reshape(1, db), p["out_o"].astype(BF16), final_g.reshape(1, d), seg)
    full = lambda a: pl.BlockSpec(a.shape, lambda bi, i: (0,) * a.ndim)
    tok = pl.BlockSpec((None, tm, db), lambda bi, i: (bi, i, 0))
    col = lambda cblk: pl.BlockSpec((None, tm, db), lambda bi, i: (bi, i, cblk))
    return pl.pallas_call(
        _merge_kernel,
        grid=(b, t // tm),
        in_specs=[pl.BlockSpec((None, tm, d), lambda bi, i: (bi, i, 0)), tok, tok, tok, tok, tok,
                  col(0), col(2)] + [full(a) for a in consts],
        out_specs=pl.BlockSpec((None, tm, d), lambda bi, i: (bi, i, 0)),
        out_shape=jax.ShapeDtypeStruct((b, t, d), F32),
        compiler_params=pltpu.CompilerParams(dimension_semantics=("parallel", "parallel"),
                                             vmem_limit_bytes=VMEM_LIMIT),
        name="merge",
    )(x, o_f, o_r, g, bonus, conv, proj, proj, *consts)


def kernel(x, c, ctx, c_ctx, ada_w_e, ada_b_e, norm_e, in_e, out_e, pool_w, pool_scale, sconv_w,
           ada_w_o, ada_b_o, norm_o, in_o, out_o, rwkv_mu, w0, w1, w2, a0, a1, a2, g1, g2,
           k_k, k_a, r_k, lnx_g, lnx_b, conf_dw_w, conf_dw_b, conf_ln_g, conf_ln_b, final_g):
    b, t, d = x.shape
    db = d // 2
    assert b == 2
    cvec = _pad_to(jnp.concatenate([c, c_ctx[None, :]], axis=0), 0, SUBLANES)
    mod_e = _adaln(cvec, ada_w_e[0], ada_b_e[0])
    mod_o = _adaln(cvec, ada_w_o[0], ada_b_o[0])

    even = dict(norm_g=norm_e[0], w_in=in_e[0], w_out=out_e[0], pool_w=pool_w[0],
                pool_scale=pool_scale[0], sconv_w=sconv_w[0])
    x1 = _even_layer(x, mod_e, ctx=False, line=GRID_W, **even)
    xc1 = _even_layer(ctx, mod_e, ctx=True, line=ctx.shape[1], **even)

    p = dict(rwkv_mu=rwkv_mu[0], w0=w0[0], w1=w1[0], w2=w2[0], a0=a0[0], a1=a1[0], a2=a2[0],
             g1=g1[0], g2=g2[0], k_k=k_k[0], k_a=k_a[0], r_k=r_k[0], lnx_g=lnx_g[0], lnx_b=lnx_b[0],
             conf_ln_g=conf_ln_g[0], conf_ln_b=conf_ln_b[0], out_o=out_o[0])

    _, v_c, _, _, at_c, rt_c, bk_c, gc_c = _odd_front(xc1, mod_o, norm_o[0], in_o[0], p, ctx=True)
    zero_state = jnp.zeros((b, 2, db // GROUP, GROUP, GROUP), F32)
    _, _, ctx_state = _rwkv_scan(at_c, rt_c, bk_c, v_c, gc_c, zero_state)

    proj, v_l, g_l, bonus, at_l, rt_l, bk_l, gc_l = _odd_front(x1, mod_o, norm_o[0], in_o[0], p, ctx=False)
    o_f, o_r, _ = _rwkv_scan(at_l, rt_l, bk_l, v_l, gc_l, ctx_state)
    conv = _conformer_conv(proj, conf_dw_w[0], conf_dw_b[0], stride=GRID_W)
    return _merge(x1, o_f, o_r, g_l, bonus, conv, proj, mod_o, p, final_g)
```

```python
import functools

import numpy as np
import jax
import jax.numpy as jnp
from jax import lax
from jax.experimental import pallas as pl
from jax.experimental.pallas import tpu as pltpu

F32 = jnp.float32
BF16 = jnp.bfloat16

GRID_W = 64
POOL_WINDOWS = (2, 4, 8, 16)
SHORT_CONV_W = 3
CONF_CONV_W = 31
RWKV_HEAD = 64
NORM_EPS = 1e-6
LNX_EPS = 64e-5
CONF_LN_EPS = 1e-5
DECAY_SCALE = float(np.exp(-0.5))

CHUNK = 64
HEADS_PER_GROUP = 4
GROUP = HEADS_PER_GROUP * RWKV_HEAD
LORA_PAD = 128
SUBLANES = 8
VMEM_LIMIT = 52 * 1024 * 1024

TILE_TOKENS = 256
PREP_TOKENS = 128
CONV_ROWS = 128
CONV_TAP_GROUP = 4
BF16_ROWS = 16
MERGE_SUBTILES = 8


def _sigmoid(x):
    return 0.5 * jnp.tanh(0.5 * x) + 0.5


def _silu(x):
    return x * _sigmoid(x)


def _split(x, n):
    if x.dtype == BF16:
        return [x]
    parts, r = [], x
    for i in range(n):
        p = r.astype(BF16)
        parts.append(p)
        if i + 1 < n:
            r = r - p.astype(F32)
    return parts


def _dot(a, b, nt):
    dims = (((1,), (1,)), ((), ())) if nt else (((1,), (0,)), ((), ()))
    return lax.dot_general(a, b, dims, preferred_element_type=F32)


def _mmp(ap, bp, nt=False):
    order = max(len(ap), len(bp))
    acc = None
    for i in reversed(range(len(ap))):
        for j in reversed(range(len(bp))):
            if i + j < order:
                t = _dot(ap[i], bp[j], nt)
                acc = t if acc is None else acc + t
    return acc


def _mm(a, b, na=1, nb=1, nt=False):
    return _mmp(_split(a, na), _split(b, nb), nt)


def _rmsnorm(x, g):
    ms = jnp.mean(x * x, axis=-1, keepdims=True)
    return x * lax.rsqrt(ms + NORM_EPS) * g


def _modulated(x, g, mod_row, d):
    shift, scale = mod_row[:, 0:d], mod_row[:, d:2 * d]
    return _rmsnorm(x, g) * (1.0 + scale) + shift


def _mod_row(mod_ref, ctx):
    row = 2 if ctx else pl.program_id(0)
    return mod_ref[pl.ds(row, 1), :]


def _adaln_kernel(c_ref, w_ref, b_ref, o_ref):
    o_ref[...] = _mm(_silu(c_ref[...]), w_ref[...], 2, 2) + b_ref[...]


def _adaln(cvec, w, b):
    d, n = w.shape
    tn = 768
    return pl.pallas_call(
        _adaln_kernel,
        grid=(n // tn,),
        in_specs=[pl.BlockSpec((SUBLANES, d), lambda j: (0, 0)),
                  pl.BlockSpec((d, tn), lambda j: (0, j)),
                  pl.BlockSpec((1, tn), lambda j: (0, j))],
        out_specs=pl.BlockSpec((SUBLANES, tn), lambda j: (0, j)),
        out_shape=jax.ShapeDtypeStruct((SUBLANES, n), F32),
        compiler_params=pltpu.CompilerParams(dimension_semantics=("parallel",),
                                             vmem_limit_bytes=VMEM_LIMIT),
        name="adaln",
    )(cvec, w, b.reshape(1, n))


def _even_kernel(x_ref, mod_ref, g_ref, win_ref, wout_ref, poolw_ref, pscale_ref, sconv_ref,
                 band_ref, invc_ref, o_ref, *, ctx, line):
    tm, d = x_ref.shape
    ts = band_ref.shape[-1]
    db = d // 2
    grp = db // len(POOL_WINDOWS)
    mrow = _mod_row(mod_ref, ctx)

    def project(c):
        x = x_ref[c["rows"], :]
        c.update(x=x, proj=_mm(_modulated(x, g_ref[...], mrow, d), win_ref[...]))

    def pool(c):
        u_a = c["proj"][:, 0:db]
        pooled = []
        for i in range(len(POOL_WINDOWS)):
            ug = u_a[:, i * grp:(i + 1) * grp]
            pooled.append(_mm(band_ref[i], ug) * invc_ref[i] - ug)
        c.update(pooled=pooled)

    def mix(c):
        proj = c["proj"]
        z_a = proj[:, db:2 * db]
        v_b, g_b = proj[:, 2 * db:3 * db], proj[:, 3 * db:4 * db]
        g_c, z_b = proj[:, 4 * db:5 * db], proj[:, 5 * db:6 * db]
        ya = [_mm(c["pooled"][i], poolw_ref[i]) for i in range(len(POOL_WINDOWS))]
        y_a = jnp.concatenate(ya, axis=-1) * pscale_ref[...] * _silu(z_a)
        q = g_c * v_b
        pos = lax.broadcasted_iota(jnp.int32, q.shape, 0) % line
        q_prev = jnp.where(pos >= 1, pltpu.roll(q, 1, 0), 0.0)
        q_next = jnp.where(pos <= line - 2, pltpu.roll(q, ts - 1, 0), 0.0)
        w = sconv_ref[...]
        conv = q_prev * w[0:1, :] + q * w[1:2, :] + q_next * w[2:3, :]
        y_b = g_b * conv * _silu(z_b)
        c.update(y=jnp.concatenate([y_a, y_b], axis=-1))

    def emit(c):
        o_ref[c["rows"], :] = c["x"] + mrow[:, 2 * d:3 * d] * _mm(c["y"], wout_ref[...])

    subs = [dict(rows=slice(s * ts, (s + 1) * ts)) for s in range(tm // ts)]
    for stage in (project, pool, mix, emit):
        for c in subs:
            stage(c)


def _pool_constants(tm, line):
    pos = np.arange(tm) % line
    lid = np.arange(tm) // line
    bands, invs = [], []
    for w in POOL_WINDOWS:
        lo = np.clip(pos - w // 2, 0, line)
        hi = np.clip(pos - w // 2 + w, 0, line)
        same = lid[:, None] == lid[None, :]
        band = same & (pos[None, :] >= lo[:, None]) & (pos[None, :] < hi[:, None])
        bands.append(band.astype(np.float32))
        invs.append(np.broadcast_to((1.0 / (hi - lo))[:, None], (tm, 128)).astype(np.float32))
    return jnp.asarray(np.stack(bands), BF16), jnp.asarray(np.stack(invs), F32)


def _even_layer(x, mod, norm_g, w_in, w_out, pool_w, pool_scale, sconv_w, *, ctx, line):
    b, t, d = x.shape
    tm = min(4 * TILE_TOKENS, t)
    assert t % tm == 0 and tm % line == 0
    db = d // 2
    ts = max(PREP_TOKENS, line)
    assert tm % ts == 0 and ts % line == 0
    band, invc = _pool_constants(ts, line)
    full = lambda a: pl.BlockSpec(a.shape, lambda bi, i: (0,) * a.ndim)
    args = (mod, norm_g.reshape(1, d), w_in.astype(BF16), w_out.astype(BF16), pool_w.astype(BF16),
            pool_scale.reshape(1, db), sconv_w, band, invc)
    return pl.pallas_call(
        functools.partial(_even_kernel, ctx=ctx, line=line),
        grid=(b, t // tm),
        in_specs=[pl.BlockSpec((None, tm, d), lambda bi, i: (bi, i, 0))] + [full(a) for a in args],
        out_specs=pl.BlockSpec((None, tm, d), lambda bi, i: (bi, i, 0)),
        out_shape=jax.ShapeDtypeStruct((b, t, d), F32),
        compiler_params=pltpu.CompilerParams(dimension_semantics=("parallel", "parallel"),
                                             vmem_limit_bytes=VMEM_LIMIT),
        name="even_ctx" if ctx else "even_latent",
    )(x, *args)


def _front_kernel(x_ref, xp_ref, xn_ref, mod_ref, ng_ref, win_ref,
                  mu_ref, w0_ref, a0_ref, w1_ref, w2_ref, a1_ref, a2_ref,
                  g1_ref, g2_ref, kk_ref, ka_ref, rk_ref, seg_ref, later_ref, whole_ref,
                  rest_ref, v_ref, g_ref, bonus_ref, at_ref, rt_ref, bk_ref, gc_ref, *, ctx):
    tm, d = x_ref.shape
    db = v_ref.shape[-1]
    ts = PREP_TOKENS
    te = ts + 2 * SUBLANES
    i, n = pl.program_id(1), pl.num_programs(1)
    mrow = _mod_row(mod_ref, ctx)
    h_prev = _modulated(xp_ref[...], ng_ref[...], mrow, d) * jnp.where(i > 0, 1.0, 0.0)
    h_next = _modulated(xn_ref[...], ng_ref[...], mrow, d) * jnp.where(i < n - 1, 1.0, 0.0)
    h = _modulated(x_ref[...], ng_ref[...], mrow, d)
    h_b = h.astype(BF16)
    h_ext = jnp.concatenate([h_prev, h, h_next], axis=0).astype(BF16)
    mu = mu_ref[...]
    first_chunk = lax.broadcasted_iota(jnp.int32, (db, ts), 1) < CHUNK

    def rest(c, j):
        rows = c["rows"]
        if j == 1:
            p12 = _dot(h_b[rows], win_ref[:, 5 * db:7 * db], False)
            rest_ref[rows, db:2 * db] = (p12[:, 0:db] * _sigmoid(p12[:, db:])).astype(BF16)
        else:
            col = 4 if j == 0 else 7
            rest_ref[rows, j * db:(j + 1) * db] = _dot(h_b[rows], win_ref[:, col * db:(col + 1) * db],
                                                        False).astype(BF16)

    def project(c):
        c.update(z_ext=_dot(h_ext[c["s"] * ts:c["s"] * ts + te], win_ref[:, 0:4 * db], False))

    def shift(c):
        z_ext = c["z_ext"]
        nbr = (pltpu.roll(z_ext, 1, 0) + pltpu.roll(z_ext, te - 1, 0))[SUBLANES:ts + SUBLANES]
        z = z_ext[SUBLANES:ts + SUBLANES]
        dz = 0.5 * nbr - z
        u, du = z[:, 0:db], dz[:, 0:db]
        v = z[:, 3 * db:4 * db] + dz[:, 3 * db:4 * db] * mu[2:3, :]
        c.update(r=z[:, db:2 * db] + dz[:, db:2 * db] * mu[0:1, :],
                 k=z[:, 2 * db:3 * db] + dz[:, 2 * db:3 * db] * mu[1:2, :], v=v,
                 uw=u + du * mu[3:4, :], ua=u + du * mu[4:5, :], ug=u + du * mu[5:6, :])
        v_ref[c["rows"], :] = v.astype(BF16)
        rest(c, 0)

    def loras(c):
        g_ref[c["rows"], :] = _mm(_sigmoid(_mm(c["ug"], g1_ref[...])), g2_ref[...]).astype(BF16)
        kk = c["k"] * kk_ref[...]
        inv_norm = lax.rsqrt(jnp.maximum(_mm(kk * kk, seg_ref[...]), 1e-24))
        c.update(lw_pre=w0_ref[...] + _mm(jnp.tanh(_mm(c["uw"], w1_ref[...])), w2_ref[...]),
                 a_pre=a0_ref[...] + _mm(_mm(c["ua"], a1_ref[...]), a2_ref[...]),
                 kk=kk * inv_norm, ksum=None)
        rest(c, 1)

    def direction(dr):
        def stage(c):
            rows, k, kk = c["rows"], c["k"], c["kk"]
            lw = -DECAY_SCALE * _sigmoid(c["lw_pre"][:, dr * db:(dr + 1) * db])
            a = _sigmoid(c["a_pre"][:, dr * db:(dr + 1) * db])
            k_d = k * (1.0 + (a - 1.0) * ka_ref[...])
            c["ksum"] = k_d if c["ksum"] is None else c["ksum"] + k_d
            lw_parts = _split(lw, 2)
            rel = -_mmp([later_ref[dr]], lw_parts)
            at_ref[dr, rows, :] = (-kk * jnp.exp(rel - lw)).astype(BF16)
            rt_ref[dr, rows, :] = (c["r"] * jnp.exp(rel)).astype(BF16)
            gout = jnp.exp(-rel)
            bh_t = (kk * a * gout).T
            kh_t = (k_d * gout).T
            bh_r, kh_r = pltpu.roll(bh_t, CHUNK, 1), pltpu.roll(kh_t, CHUNK, 1)
            gam = jnp.exp(_mmp([whole_ref[...]], lw_parts))
            for q in range(ts // CHUNK):
                own = first_chunk if q == 0 else jnp.logical_not(first_chunk)
                chunk = c["s"] * (ts // CHUNK) + q
                bk_ref[dr, chunk] = jnp.concatenate([jnp.where(own, bh_t, bh_r), jnp.where(own, kh_t, kh_r)],
                                                    axis=1).astype(BF16)
                gc_ref[dr, chunk] = gam[q * SUBLANES:(q + 1) * SUBLANES, :]
            if dr == 0:
                rest(c, 2)
        return stage

    def bonus(c):
        bonus_ref[c["rows"], :] = (_mm(c["r"] * c["ksum"] * rk_ref[...], seg_ref[...]) * c["v"]).astype(BF16)

    subs = [dict(s=s, rows=slice(s * ts, (s + 1) * ts)) for s in range(tm // ts)]
    for stage in (project, shift, loras, direction(0), direction(1), bonus):
        for c in subs:
            stage(c)


def _prep_constants(tm):
    t = np.arange(tm)
    same = (t[:, None] // CHUNK) == (t[None, :] // CHUNK)
    later = np.stack([same & (t[None, :] > t[:, None]), same & (t[None, :] < t[:, None])]).astype(np.float32)
    whole = (np.arange(tm // CHUNK * SUBLANES)[:, None] // SUBLANES) == (t[None, :] // CHUNK)
    return jnp.asarray(later, BF16), jnp.asarray(whole.astype(np.float32), BF16)


def _pad_to(a, axis, size):
    pad = [(0, 0)] * a.ndim
    pad[axis] = (0, size - a.shape[axis])
    return jnp.pad(a, pad)


def _odd_front(x, mod, norm_g, w_in, p, *, ctx):
    b, t, d = x.shape
    db = p["k_k"].shape[-1]
    tm = min(2 * TILE_TOKENS, t)
    assert t % tm == 0 and tm % PREP_TOKENS == 0 and PREP_TOKENS == 2 * CHUNK
    nc = t // CHUNK
    nb = t // SUBLANES
    per = tm // SUBLANES
    later, whole = _prep_constants(PREP_TOKENS)
    heads = db // RWKV_HEAD
    seg = jnp.asarray(np.kron(np.eye(heads), np.ones((RWKV_HEAD, RWKV_HEAD))), BF16)

    def lora_pair(w_a, w_b):
        rank = w_a.shape[-1]
        down = _pad_to(jnp.concatenate([w_a[0], w_a[1]], axis=-1), 1, LORA_PAD)
        up = jnp.zeros((LORA_PAD, 2 * db), F32)
        up = up.at[0:rank, 0:db].set(w_b[0]).at[rank:2 * rank, db:2 * db].set(w_b[1])
        return down.astype(BF16), up.astype(BF16)

    w1c, w2c = lora_pair(p["w1"], p["w2"])
    a1c, a2c = lora_pair(p["a1"], p["a2"])
    consts = (mod, norm_g.reshape(1, d), w_in.astype(BF16),
              _pad_to(p["rwkv_mu"], 0, SUBLANES), p["w0"].reshape(1, 2 * db), p["a0"].reshape(1, 2 * db),
              w1c, w2c, a1c, a2c,
              _pad_to(p["g1"], 1, LORA_PAD).astype(BF16), _pad_to(p["g2"], 0, LORA_PAD).astype(BF16),
              p["k_k"].reshape(1, db), p["k_a"].reshape(1, db), p["r_k"].reshape(1, db), seg, later, whole)
    full = lambda a: pl.BlockSpec(a.shape, lambda bi, i: (0,) * a.ndim)
    tok = pl.BlockSpec((None, tm, db), lambda bi, i: (bi, i, 0))
    dirtok = pl.BlockSpec((None, 2, tm, db), lambda bi, i: (bi, 0, i, 0))
    tok_shape = jax.ShapeDtypeStruct((b, t, db), F32)
    dir_shape = jax.ShapeDtypeStruct((b, 2, t, db), BF16)
    nrest = 3 * db
    assert w_in.shape[1] == 8 * db
    tok_b = jax.ShapeDtypeStruct((b, t, db), BF16)
    return pl.pallas_call(
        functools.partial(_front_kernel, ctx=ctx),
        grid=(b, t // tm),
        in_specs=[pl.BlockSpec((None, tm, d), lambda bi, i: (bi, i, 0)),
                  pl.BlockSpec((None, SUBLANES, d), lambda bi, i: (bi, jnp.maximum(i * per - 1, 0), 0)),
                  pl.BlockSpec((None, SUBLANES, d), lambda bi, i: (bi, jnp.minimum((i + 1) * per, nb - 1), 0))]
        + [full(a) for a in consts],
        out_specs=[pl.BlockSpec((None, tm, nrest), lambda bi, i: (bi, i, 0)), tok, tok, tok, dirtok, dirtok,
                   pl.BlockSpec((None, 2, tm // CHUNK, db, 4 * CHUNK), lambda bi, i: (bi, 0, i, 0, 0)),
                   pl.BlockSpec((None, 2, tm // CHUNK, SUBLANES, db), lambda bi, i: (bi, 0, i, 0, 0))],
        out_shape=[jax.ShapeDtypeStruct((b, t, nrest), BF16), tok_b, tok_b, tok_b, dir_shape, dir_shape,
                   jax.ShapeDtypeStruct((b, 2, nc, db, 4 * CHUNK), BF16),
                   jax.ShapeDtypeStruct((b, 2, nc, SUBLANES, db), F32)],
        compiler_params=pltpu.CompilerParams(dimension_semantics=("parallel", "parallel"),
                                             vmem_limit_bytes=VMEM_LIMIT),
        name="front_ctx" if ctx else "front_latent",
    )(x, x, x, *consts)


SCAN_PARTS = 1
SCAN_CHUNKS_PER_STEP = 4


def _scan_kernel(at0, rt0, bk0, v0, gc0, at1, rt1, bk1, v1, gc1, s0_ref,
                 hm_ref, ms_ref, mi_ref, idw_ref, bdm_ref, eye_ref,
                 of_ref, or_ref, st_ref):
    @pl.when(pl.program_id(0) == 0)
    def _():
        st_ref[...] = s0_ref[...]

    npart = SCAN_PARTS
    pair = GROUP // 2
    bdm_b = bdm_ref[...]
    bd2 = bdm_b[0:pair, 0:pair].astype(F32)
    eye2 = eye_ref[0:pair, 0:pair]

    def bd(x):
        return [jnp.concatenate([part * hm_ref[h] for h in range(HEADS_PER_GROUP)], axis=0)
                for part in _split(x, npart)]

    def cat_parts(plists):
        return [jnp.concatenate(ps, axis=-1) for ps in zip(*plists)]

    steps = int(np.log2(CHUNK)) - 1

    def load(c):
        at_r, rt_r, bk_r, v_r, gc_r = c["refs"]
        bi, g, sl, q, rows = c["bi"], c["g"], c["sl"], c["q"], c["rows"]
        c.update(at=at_r[bi, rows, sl], rt=rt_r[bi, rows, sl], v=v_r[bi, rows, sl], gc=gc_r[bi, q, 0:1, sl],
                 bk=bk_r[bi, q, g * GROUP:(g + 1) * GROUP, :])

    def interactions(c):
        ms, mi = ms_ref[c["d"]], mi_ref[c["d"]]
        bk = c["bk"]
        wts = jnp.concatenate([jnp.concatenate([bk[:, 0:pair]] * 2, axis=1) * bdm_b,
                               jnp.concatenate([bk[:, pair:]] * 2, axis=1) * bdm_b], axis=1)
        gmat = _dot(jnp.concatenate([c["at"], c["rt"]], axis=0), wts, False)
        lab = gmat[0:CHUNK, 0:GROUP] * ms
        c.update(lak=gmat[0:CHUNK, GROUP:] * ms, mrb=gmat[CHUNK:, 0:GROUP] * mi,
                 mrk=gmat[CHUNK:, GROUP:] * mi, pw=lab, tinv=idw_ref[...] + lab)

    def invert(s):
        def stage(c):
            pw, tinv = c["pw"], c["tinv"]
            wts = bd(pw)
            if s == 0:
                pw = _mmp(_split(pw, npart), wts)
            elif s < steps:
                res = _mmp(_split(jnp.concatenate([pw, tinv], axis=0), npart), wts)
                pw, tinv = res[0:CHUNK], tinv + res[CHUNK:]
            else:
                tinv = tinv + _mmp(_split(tinv, npart), wts)
            c.update(pw=pw, tinv=tinv)
        return stage

    def values(c):
        lv_mv = _mmp(_split(jnp.concatenate([c["lak"], c["mrk"]], axis=0), npart), bd(c["v"]))
        c.update(lv=lv_mv[0:CHUNK], mv=lv_mv[CHUNK:])

    def solve(c):
        aw = _mmp(_split(c["tinv"], npart), cat_parts([bd(c["at"]), bd(c["lv"])]))
        c.update(ahat=aw[:, 0:GROUP], wv=aw[:, GROUP:])

    def outputs(c):
        ro = _mmp(_split(c["mrb"], npart), cat_parts([bd(c["ahat"]), bd(c["wv"])]))
        c.update(rhat=(c["rt"].astype(F32) + ro[:, 0:GROUP]) * c["gc"], ointra=ro[:, GROUP:] + c["mv"])

    def transition(c):
        zero_rows = jnp.zeros((CHUNK, GROUP), BF16)
        zero_half = jnp.zeros((CHUNK, pair), BF16)
        zero_blk = jnp.zeros((pair, pair), F32)
        pms, qms = [], []
        for p in range(2):
            ps = slice(p * pair, (p + 1) * pair)
            top = jnp.concatenate([c["ahat"][:, ps], c["wv"][:, ps]], axis=1).astype(BF16)
            mid = jnp.concatenate([zero_half, c["v"][:, ps]], axis=1)
            rhs = jnp.concatenate([top, zero_rows, mid, zero_rows], axis=0)
            pq = _dot(c["bk"][ps, :], rhs, False)
            pblk = (pq[:, 0:pair] * bd2 + eye2) * c["gc"][:, ps]
            qblk = pq[:, pair:] * bd2
            pms.append(jnp.concatenate([pblk, zero_blk] if p == 0 else [zero_blk, pblk], axis=1))
            qms.append(jnp.concatenate([qblk, zero_blk] if p == 0 else [zero_blk, qblk], axis=1))
        c.update(pm=jnp.concatenate(pms, axis=0), qm=jnp.concatenate(qms, axis=0))

    def advance(c, st):
        rs = _dot(jnp.concatenate([c["rhat"], c["pm"]], axis=0).astype(BF16), st.astype(BF16), False)
        c["out"][c["bi"], c["rows"], c["sl"]] = (rs[0:CHUNK] + c["ointra"]).astype(BF16)
        return rs[CHUNK:] + c["qm"]

    nq = at0.shape[1] // CHUNK
    dirs = ((at0, rt0, bk0, v0, gc0, of_ref), (at1, rt1, bk1, v1, gc1, or_ref))
    groups = {}
    for bi in range(at0.shape[0]):
        for d, refs in enumerate(dirs):
            for g in range(at0.shape[-1] // GROUP):
                order = range(nq) if d == 0 else reversed(range(nq))
                groups[(bi, d, g)] = [
                    dict(q=q, rows=slice(q * CHUNK, (q + 1) * CHUNK), bi=bi, d=d, g=g,
                         sl=slice(g * GROUP, (g + 1) * GROUP), refs=refs[:5], out=refs[5]) for q in order]
    chains = [c for k in range(nq) for cs in groups.values() for c in [cs[k]]]
    stages = ([load, interactions] + [invert(s) for s in range(steps + 1)]
              + [values, solve, outputs, transition])
    for stage in stages:
        for c in chains:
            stage(c)
    states = {key: st_ref[key[0], key[1], key[2]] for key in groups}
    for k in range(nq):
        for key, cs in groups.items():
            states[key] = advance(cs[k], states[key])
    for key in groups:
        st_ref[key[0], key[1], key[2]] = states[key]


def _scan_constants():
    t = np.arange(CHUNK)[:, None]
    lane = np.arange(GROUP)[None, :]
    s = lane % CHUNK
    strict = np.stack([s < t, s > t]).astype(np.float32)
    incl = np.stack([s <= t, s >= t]).astype(np.float32)
    idw = (s == t).astype(np.float32)
    hm = np.stack([np.broadcast_to((lane // RWKV_HEAD) == h, (CHUNK, GROUP)) for h in range(HEADS_PER_GROUP)])
    bdm = np.kron(np.eye(HEADS_PER_GROUP), np.ones((RWKV_HEAD, RWKV_HEAD)))
    return (jnp.asarray(hm.astype(np.float32), BF16), jnp.asarray(strict), jnp.asarray(incl), jnp.asarray(idw),
            jnp.asarray(bdm, BF16), jnp.eye(GROUP, dtype=F32))


def _rwkv_scan(at, rt, bk, v, gc, s0):
    b, _, t, db = at.shape
    nc = t // CHUNK
    consts = _scan_constants()
    full = lambda a: pl.BlockSpec(a.shape, lambda i: (0,) * a.ndim)
    nq = min(SCAN_CHUNKS_PER_STEP, nc)
    ns = nc // nq
    assert nc % nq == 0
    fwd, rev = (lambda i: i), (lambda i: ns - 1 - i)
    dspec = lambda d, ci: pl.BlockSpec((b, None, nq * CHUNK, db), lambda i: (0, d, ci(i), 0))
    kspec = lambda d, ci: pl.BlockSpec((b, None, nq, db, 4 * CHUNK), lambda i: (0, d, ci(i), 0, 0))
    vspec = lambda ci: pl.BlockSpec((b, nq * CHUNK, db), lambda i: (0, ci(i), 0))
    gspec = lambda d, ci: pl.BlockSpec((b, None, nq, SUBLANES, db), lambda i: (0, d, ci(i), 0, 0))
    in_specs = ([dspec(0, fwd), dspec(0, fwd), kspec(0, fwd), vspec(fwd), gspec(0, fwd),
                 dspec(1, rev), dspec(1, rev), kspec(1, rev), vspec(rev), gspec(1, rev), full(s0)]
                + [full(a) for a in consts])
    return pl.pallas_call(
        _scan_kernel,
        grid=(ns,),
        in_specs=in_specs,
        out_specs=[vspec(fwd), vspec(rev), full(s0)],
        out_shape=[jax.ShapeDtypeStruct((b, t, db), BF16), jax.ShapeDtypeStruct((b, t, db), BF16),
                   jax.ShapeDtypeStruct(s0.shape, F32)],
        compiler_params=pltpu.CompilerParams(dimension_semantics=("arbitrary",),
                                             vmem_limit_bytes=VMEM_LIMIT),
        name="rwkv_scan",
    )(at, rt, bk, v, gc, at, rt, bk, v, gc, s0, *consts)


def _conv_kernel(glu_ref, w_ref, b_ref, o_ref, h_ref, *, stride):
    t, c = glu_ref.shape
    halo = (CONF_CONV_W // 2) * stride
    nsteps = t // CONV_ROWS
    zeros = jnp.zeros((halo, c), BF16)
    h_ref[0:halo, :] = zeros
    h_ref[halo + t:halo + t + halo, :] = zeros

    def pad(j, carry):
        r0 = pl.multiple_of(j * CONV_ROWS, CONV_ROWS)
        h_ref[pl.ds(halo + r0, CONV_ROWS), :] = glu_ref[pl.ds(r0, CONV_ROWS), :]
        return carry

    lax.fori_loop(0, nsteps, pad, 0)
    bias = b_ref[...]

    def conv(j, carry):
        r0 = pl.multiple_of(j * CONV_ROWS, CONV_ROWS)
        acc = jnp.broadcast_to(bias, (CONV_ROWS, c))
        for t0 in range(0, CONF_CONV_W, CONV_TAP_GROUP):
            part = None
            for tap in range(t0, min(t0 + CONV_TAP_GROUP, CONF_CONV_W)):
                wt = jnp.concatenate([w_ref[tap * BF16_ROWS:(tap + 1) * BF16_ROWS, :]] * (CONV_ROWS // BF16_ROWS),
                                     axis=0)
                term = h_ref[pl.ds(r0 + tap * stride, CONV_ROWS), :] * wt
                part = term if part is None else part + term
            acc = acc + part.astype(F32)
        o_ref[pl.ds(r0, CONV_ROWS), :] = acc.astype(BF16)
        return carry

    lax.fori_loop(0, nsteps, conv, 0)


def _conformer_conv(proj, dw_w, dw_b, *, stride):
    b, t, n = proj.shape
    db = dw_w.shape[-1]
    lanes = 128
    assert t % CONV_ROWS == 0 and stride % SUBLANES == 0
    halo = (CONF_CONV_W // 2) * stride
    c1 = db // lanes
    w_rows = jnp.repeat(dw_w.astype(BF16), BF16_ROWS, axis=0)
    return pl.pallas_call(
        functools.partial(_conv_kernel, stride=stride),
        grid=(b, db // lanes),
        in_specs=[pl.BlockSpec((None, t, lanes), lambda bi, j: (bi, 0, c1 + j)),
                  pl.BlockSpec((CONF_CONV_W * BF16_ROWS, lanes), lambda bi, j: (0, j)),
                  pl.BlockSpec((1, lanes), lambda bi, j: (0, j))],
        out_specs=pl.BlockSpec((None, t, lanes), lambda bi, j: (bi, 0, j)),
        out_shape=jax.ShapeDtypeStruct((b, t, db), BF16),
        scratch_shapes=[pltpu.VMEM((t + 2 * halo, lanes), BF16)],
        compiler_params=pltpu.CompilerParams(dimension_semantics=("parallel", "parallel"),
                                             vmem_limit_bytes=VMEM_LIMIT),
        name="conformer_conv",
    )(proj, w_rows, dw_b.reshape(1, db))


def _round_up(x, m):
    return (x + m - 1) // m * m


def _merge_kernel(x_ref, of_ref, or_ref, g_ref, bonus_ref, conv_ref, zc_ref, zd_ref, mod_ref,
                  lnxg_ref, lnxb_ref, clng_ref, clnb_ref, wout_ref, fg_ref, seg_ref, o_ref):
    tm, d = x_ref.shape
    mrow = _mod_row(mod_ref, False)
    seg = seg_ref[...]

    def centre(c):
        o = of_ref[c["rows"], :].astype(F32) + or_ref[c["rows"], :].astype(F32)
        c.update(dev=o - _mm(o, seg))

    def normalise(c):
        rows, dev = c["rows"], c["dev"]
        var = _mm(dev * dev, seg)
        on = dev * lax.rsqrt(var + LNX_EPS) * lnxg_ref[...] + lnxb_ref[...]
        y_c = ((on + bonus_ref[rows, :].astype(F32)) * g_ref[rows, :].astype(F32)
               * _silu(zc_ref[rows, :].astype(F32)))
        cv = conv_ref[rows, :].astype(F32)
        cdev = cv - jnp.mean(cv, axis=-1, keepdims=True)
        cvar = jnp.mean(cdev * cdev, axis=-1, keepdims=True)
        ln = cdev * lax.rsqrt(cvar + CONF_LN_EPS) * clng_ref[...] + clnb_ref[...]
        y_d = _silu(ln) * _silu(zd_ref[rows, :].astype(F32))
        c.update(y=jnp.concatenate([y_c, y_d], axis=-1))

    def emit(c):
        y = _mm(c["y"], wout_ref[...])
        o_ref[c["rows"], :] = _rmsnorm(x_ref[c["rows"], :] + mrow[:, 2 * d:3 * d] * y, fg_ref[...])

    ts = tm // MERGE_SUBTILES
    subs = [dict(rows=slice(s * ts, (s + 1) * ts)) for s in range(MERGE_SUBTILES)]
    for stage in (centre, normalise, emit):
        for c in subs:
            stage(c)


def _merge(x, o_f, o_r, g, bonus, conv, proj, mod, p, final_g):
    b, t, d = x.shape
    db = d // 2
    tm = min(4 * TILE_TOKENS, t)
    heads = db // RWKV_HEAD
    seg = jnp.asarray(np.kron(np.eye(heads), np.full((RWKV_HEAD, RWKV_HEAD), 1.0 / RWKV_HEAD)), BF16)
    consts = (mod, p["lnx_g"].reshape(1, db), p["lnx_b"].reshape(1, db), p["conf_ln_g"].reshape(1, db),
              p["conf_ln_b"].reshape(1, db), p["out_o"].astype(BF16), final_g.reshape(1, d), seg)
    full = lambda a: pl.BlockSpec(a.shape, lambda bi, i: (0,) * a.ndim)
    tok = pl.BlockSpec((None, tm, db), lambda bi, i: (bi, i, 0))
    col = lambda cblk: pl.BlockSpec((None, tm, db), lambda bi, i: (bi, i, cblk))
    return pl.pallas_call(
        _merge_kernel,
        grid=(b, t // tm),
        in_specs=[pl.BlockSpec((None, tm, d), lambda bi, i: (bi, i, 0)), tok, tok, tok, tok, tok,
                  col(0), col(2)] + [full(a) for a in consts],
        out_specs=pl.BlockSpec((None, tm, d), lambda bi, i: (bi, i, 0)),
        out_shape=jax.ShapeDtypeStruct((b, t, d), F32),
        compiler_params=pltpu.CompilerParams(dimension_semantics=("parallel", "parallel"),
                                             vmem_limit_bytes=VMEM_LIMIT),
        name="merge",
    )(x, o_f, o_r, g, bonus, conv, proj, proj, *consts)


def kernel(x, c, ctx, c_ctx, ada_w_e, ada_b_e, norm_e, in_e, out_e, pool_w, pool_scale, sconv_w,
           ada_w_o, ada_b_o, norm_o, in_o, out_o, rwkv_mu, w0, w1, w2, a0, a1, a2, g1, g2,
           k_k, k_a, r_k, lnx_g, lnx_b, conf_dw_w, conf_dw_b, conf_ln_g, conf_ln_b, final_g):
    b, t, d = x.shape
    db = d // 2
    assert b == 2
    cvec = _pad_to(jnp.concatenate([c, c_ctx[None, :]], axis=0), 0, SUBLANES)
    mod_e = _adaln(cvec, ada_w_e[0], ada_b_e[0])
    mod_o = _adaln(cvec, ada_w_o[0], ada_b_o[0])

    even = dict(norm_g=norm_e[0], w_in=in_e[0], w_out=out_e[0], pool_w=pool_w[0],
                pool_scale=pool_scale[0], sconv_w=sconv_w[0])
    x1 = _even_layer(x, mod_e, ctx=False, line=GRID_W, **even)
    xc1 = _even_layer(ctx, mod_e, ctx=True, line=ctx.shape[1], **even)

    p = dict(rwkv_mu=rwkv_mu[0], w0=w0[0], w1=w1[0], w2=w2[0], a0=a0[0], a1=a1[0], a2=a2[0],
             g1=g1[0], g2=g2[0], k_k=k_k[0], k_a=k_a[0], r_k=r_k[0], lnx_g=lnx_g[0], lnx_b=lnx_b[0],
             conf_ln_g=conf_ln_g[0], conf_ln_b=conf_ln_b[0], out_o=out_o[0])

    _, v_c, _, _, at_c, rt_c, bk_c, gc_c = _odd_front(xc1, mod_o, norm_o[0], in_o[0], p, ctx=True)
    zero_state = jnp.zeros((b, 2, db // GROUP, GROUP, GROUP), F32)
    _, _, ctx_state = _rwkv_scan(at_c, rt_c, bk_c, v_c, gc_c, zero_state)

    proj, v_l, g_l, bonus, at_l, rt_l, bk_l, gc_l = _odd_front(x1, mod_o, norm_o[0], in_o[0], p, ctx=False)
    o_f, o_r, _ = _rwkv_scan(at_l, rt_l, bk_l, v_l, gc_l, ctx_state)
    conv = _conformer_conv(proj, conf_dw_w[0], conf_dw_b[0], stride=GRID_W)
    return _merge(x1, o_f, o_r, g_l, bonus, conv, proj, mod_o, p, final_g)
```

```python
import functools

import numpy as np
import jax
import jax.numpy as jnp
from jax import lax
from jax.experimental import pallas as pl
from jax.experimental.pallas import tpu as pltpu

F32 = jnp.float32
BF16 = jnp.bfloat16

GRID_W = 64
POOL_WINDOWS = (2, 4, 8, 16)
SHORT_CONV_W = 3
CONF_CONV_W = 31
RWKV_HEAD = 64
NORM_EPS = 1e-6
LNX_EPS = 64e-5
CONF_LN_EPS = 1e-5
DECAY_SCALE = float(np.exp(-0.5))

CHUNK = 64
HEADS_PER_GROUP = 4
GROUP = HEADS_PER_GROUP * RWKV_HEAD
LORA_PAD = 128
SUBLANES = 8
VMEM_LIMIT = 52 * 1024 * 1024

TILE_TOKENS = 256
PREP_TOKENS = 128
CONV_ROWS = 128
MERGE_SUBTILES = 8


def _sigmoid(x):
    return 0.5 * jnp.tanh(0.5 * x) + 0.5


def _silu(x):
    return x * _sigmoid(x)


def _split(x, n):
    if x.dtype == BF16:
        return [x]
    parts, r = [], x
    for i in range(n):
        p = r.astype(BF16)
        parts.append(p)
        if i + 1 < n:
            r = r - p.astype(F32)
    return parts


def _dot(a, b, nt):
    dims = (((1,), (1,)), ((), ())) if nt else (((1,), (0,)), ((), ()))
    return lax.dot_general(a, b, dims, preferred_element_type=F32)


def _mmp(ap, bp, nt=False):
    order = max(len(ap), len(bp))
    acc = None
    for i in reversed(range(len(ap))):
        for j in reversed(range(len(bp))):
            if i + j < order:
                t = _dot(ap[i], bp[j], nt)
                acc = t if acc is None else acc + t
    return acc


def _mm(a, b, na=1, nb=1, nt=False):
    return _mmp(_split(a, na), _split(b, nb), nt)


def _rmsnorm(x, g):
    ms = jnp.mean(x * x, axis=-1, keepdims=True)
    return x * lax.rsqrt(ms + NORM_EPS) * g


def _modulated(x, g, mod_row, d):
    shift, scale = mod_row[:, 0:d], mod_row[:, d:2 * d]
    return _rmsnorm(x, g) * (1.0 + scale) + shift


def _mod_row(mod_ref, ctx):
    row = 2 if ctx else pl.program_id(0)
    return mod_ref[pl.ds(row, 1), :]


def _adaln_kernel(c_ref, w_ref, b_ref, o_ref):
    o_ref[...] = _mm(_silu(c_ref[...]), w_ref[...], 2, 2) + b_ref[...]


def _adaln(cvec, w, b):
    d, n = w.shape
    tn = 768
    return pl.pallas_call(
        _adaln_kernel,
        grid=(n // tn,),
        in_specs=[pl.BlockSpec((SUBLANES, d), lambda j: (0, 0)),
                  pl.BlockSpec((d, tn), lambda j: (0, j)),
                  pl.BlockSpec((1, tn), lambda j: (0, j))],
        out_specs=pl.BlockSpec((SUBLANES, tn), lambda j: (0, j)),
        out_shape=jax.ShapeDtypeStruct((SUBLANES, n), F32),
        compiler_params=pltpu.CompilerParams(dimension_semantics=("parallel",),
                                             vmem_limit_bytes=VMEM_LIMIT),
        name="adaln",
    )(cvec, w, b.reshape(1, n))


def _even_kernel(x_ref, mod_ref, g_ref, win_ref, wout_ref, poolw_ref, pscale_ref, sconv_ref,
                 band_ref, invc_ref, o_ref, *, ctx, line):
    tm, d = x_ref.shape
    ts = band_ref.shape[-1]
    db = d // 2
    grp = db // len(POOL_WINDOWS)
    mrow = _mod_row(mod_ref, ctx)

    def project(c):
        x = x_ref[c["rows"], :]
        c.update(x=x, proj=_mm(_modulated(x, g_ref[...], mrow, d), win_ref[...]))

    def pool(c):
        u_a = c["proj"][:, 0:db]
        pooled = []
        for i in range(len(POOL_WINDOWS)):
            ug = u_a[:, i * grp:(i + 1) * grp]
            pooled.append(_mm(band_ref[i], ug) * invc_ref[i] - ug)
        c.update(pooled=pooled)

    def mix(c):
        proj = c["proj"]
        z_a = proj[:, db:2 * db]
        v_b, g_b = proj[:, 2 * db:3 * db], proj[:, 3 * db:4 * db]
        g_c, z_b = proj[:, 4 * db:5 * db], proj[:, 5 * db:6 * db]
        ya = [_mm(c["pooled"][i], poolw_ref[i]) for i in range(len(POOL_WINDOWS))]
        y_a = jnp.concatenate(ya, axis=-1) * pscale_ref[...] * _silu(z_a)
        q = g_c * v_b
        pos = lax.broadcasted_iota(jnp.int32, q.shape, 0) % line
        q_prev = jnp.where(pos >= 1, pltpu.roll(q, 1, 0), 0.0)
        q_next = jnp.where(pos <= line - 2, pltpu.roll(q, ts - 1, 0), 0.0)
        w = sconv_ref[...]
        conv = q_prev * w[0:1, :] + q * w[1:2, :] + q_next * w[2:3, :]
        y_b = g_b * conv * _silu(z_b)
        c.update(y=jnp.concatenate([y_a, y_b], axis=-1))

    def emit(c):
        o_ref[c["rows"], :] = c["x"] + mrow[:, 2 * d:3 * d] * _mm(c["y"], wout_ref[...])

    subs = [dict(rows=slice(s * ts, (s + 1) * ts)) for s in range(tm // ts)]
    for stage in (project, pool, mix, emit):
        for c in subs:
            stage(c)


def _pool_constants(tm, line):
    pos = np.arange(tm) % line
    lid = np.arange(tm) // line
    bands, invs = [], []
    for w in POOL_WINDOWS:
        lo = np.clip(pos - w // 2, 0, line)
        hi = np.clip(pos - w // 2 + w, 0, line)
        same = lid[:, None] == lid[None, :]
        band = same & (pos[None, :] >= lo[:, None]) & (pos[None, :] < hi[:, None])
        bands.append(band.astype(np.float32))
        invs.append(np.broadcast_to((1.0 / (hi - lo))[:, None], (tm, 128)).astype(np.float32))
    return jnp.asarray(np.stack(bands), BF16), jnp.asarray(np.stack(invs), F32)


def _even_layer(x, mod, norm_g, w_in, w_out, pool_w, pool_scale, sconv_w, *, ctx, line):
    b, t, d = x.shape
    tm = min(4 * TILE_TOKENS, t)
    assert t % tm == 0 and tm % line == 0
    db = d // 2
    ts = max(PREP_TOKENS, line)
    assert tm % ts == 0 and ts % line == 0
    band, invc = _pool_constants(ts, line)
    full = lambda a: pl.BlockSpec(a.shape, lambda bi, i: (0,) * a.ndim)
    args = (mod, norm_g.reshape(1, d), w_in.astype(BF16), w_out.astype(BF16), pool_w.astype(BF16),
            pool_scale.reshape(1, db), sconv_w, band, invc)
    return pl.pallas_call(
        functools.partial(_even_kernel, ctx=ctx, line=line),
        grid=(b, t // tm),
        in_specs=[pl.BlockSpec((None, tm, d), lambda bi, i: (bi, i, 0))] + [full(a) for a in args],
        out_specs=pl.BlockSpec((None, tm, d), lambda bi, i: (bi, i, 0)),
        out_shape=jax.ShapeDtypeStruct((b, t, d), F32),
        compiler_params=pltpu.CompilerParams(dimension_semantics=("parallel", "parallel"),
                                             vmem_limit_bytes=VMEM_LIMIT),
        name="even_ctx" if ctx else "even_latent",
    )(x, *args)


def _front_kernel(x_ref, xp_ref, xn_ref, mod_ref, ng_ref, win_ref,
                  mu_ref, w0_ref, a0_ref, w1_ref, w2_ref, a1_ref, a2_ref,
                  g1_ref, g2_ref, kk_ref, ka_ref, rk_ref, seg_ref, later_ref, whole_ref,
                  rest_ref, v_ref, g_ref, bonus_ref, at_ref, rt_ref, bk_ref, gc_ref, *, ctx):
    tm, d = x_ref.shape
    db = v_ref.shape[-1]
    ts = PREP_TOKENS
    te = ts + 2 * SUBLANES
    i, n = pl.program_id(1), pl.num_programs(1)
    mrow = _mod_row(mod_ref, ctx)
    h_prev = _modulated(xp_ref[...], ng_ref[...], mrow, d) * jnp.where(i > 0, 1.0, 0.0)
    h_next = _modulated(xn_ref[...], ng_ref[...], mrow, d) * jnp.where(i < n - 1, 1.0, 0.0)
    h = _modulated(x_ref[...], ng_ref[...], mrow, d)
    h_b = h.astype(BF16)
    h_ext = jnp.concatenate([h_prev, h, h_next], axis=0).astype(BF16)
    mu = mu_ref[...]
    first_chunk = lax.broadcasted_iota(jnp.int32, (db, ts), 1) < CHUNK

    def rest(c, j):
        rows = c["rows"]
        if j == 1:
            p12 = _dot(h_b[rows], win_ref[:, 5 * db:7 * db], False)
            rest_ref[rows, db:2 * db] = (p12[:, 0:db] * _sigmoid(p12[:, db:])).astype(BF16)
        else:
            col = 4 if j == 0 else 7
            rest_ref[rows, j * db:(j + 1) * db] = _dot(h_b[rows], win_ref[:, col * db:(col + 1) * db],
                                                        False).astype(BF16)

    def project(c):
        c.update(z_ext=_dot(h_ext[c["s"] * ts:c["s"] * ts + te], win_ref[:, 0:4 * db], False))

    def shift(c):
        z_ext = c["z_ext"]
        nbr = (pltpu.roll(z_ext, 1, 0) + pltpu.roll(z_ext, te - 1, 0))[SUBLANES:ts + SUBLANES]
        z = z_ext[SUBLANES:ts + SUBLANES]
        dz = 0.5 * nbr - z
        u, du = z[:, 0:db], dz[:, 0:db]
        v = z[:, 3 * db:4 * db] + dz[:, 3 * db:4 * db] * mu[2:3, :]
        c.update(r=z[:, db:2 * db] + dz[:, db:2 * db] * mu[0:1, :],
                 k=z[:, 2 * db:3 * db] + dz[:, 2 * db:3 * db] * mu[1:2, :], v=v,
                 uw=u + du * mu[3:4, :], ua=u + du * mu[4:5, :], ug=u + du * mu[5:6, :])
        v_ref[c["rows"], :] = v.astype(BF16)
        rest(c, 0)

    def loras(c):
        g_ref[c["rows"], :] = _mm(_sigmoid(_mm(c["ug"], g1_ref[...])), g2_ref[...]).astype(BF16)
        kk = c["k"] * kk_ref[...]
        inv_norm = lax.rsqrt(jnp.maximum(_mm(kk * kk, seg_ref[...]), 1e-24))
        c.update(lw_pre=w0_ref[...] + _mm(jnp.tanh(_mm(c["uw"], w1_ref[...])), w2_ref[...]),
                 a_pre=a0_ref[...] + _mm(_mm(c["ua"], a1_ref[...]), a2_ref[...]),
                 kk=kk * inv_norm, ksum=None)
        rest(c, 1)

    def direction(dr):
        def stage(c):
            rows, k, kk = c["rows"], c["k"], c["kk"]
            lw = -DECAY_SCALE * _sigmoid(c["lw_pre"][:, dr * db:(dr + 1) * db])
            a = _sigmoid(c["a_pre"][:, dr * db:(dr + 1) * db])
            k_d = k * (1.0 + (a - 1.0) * ka_ref[...])
            c["ksum"] = k_d if c["ksum"] is None else c["ksum"] + k_d
            lw_parts = _split(lw, 2)
            rel = -_mmp([later_ref[dr]], lw_parts)
            at_ref[dr, rows, :] = (-kk * jnp.exp(rel - lw)).astype(BF16)
            rt_ref[dr, rows, :] = (c["r"] * jnp.exp(rel)).astype(BF16)
            gout = jnp.exp(-rel)
            bh_t = (kk * a * gout).T
            kh_t = (k_d * gout).T
            bh_r, kh_r = pltpu.roll(bh_t, CHUNK, 1), pltpu.roll(kh_t, CHUNK, 1)
            gam = jnp.exp(_mmp([whole_ref[...]], lw_parts))
            for q in range(ts // CHUNK):
                own = first_chunk if q == 0 else jnp.logical_not(first_chunk)
                chunk = c["s"] * (ts // CHUNK) + q
                bk_ref[dr, chunk] = jnp.concatenate([jnp.where(own, bh_t, bh_r), jnp.where(own, kh_t, kh_r)],
                                                    axis=1).astype(BF16)
                gc_ref[dr, chunk] = gam[q * SUBLANES:(q + 1) * SUBLANES, :]
            if dr == 0:
                rest(c, 2)
        return stage

    def bonus(c):
        bonus_ref[c["rows"], :] = (_mm(c["r"] * c["ksum"] * rk_ref[...], seg_ref[...]) * c["v"]).astype(BF16)

    subs = [dict(s=s, rows=slice(s * ts, (s + 1) * ts)) for s in range(tm // ts)]
    for stage in (project, shift, loras, direction(0), direction(1), bonus):
        for c in subs:
            stage(c)


def _prep_constants(tm):
    t = np.arange(tm)
    same = (t[:, None] // CHUNK) == (t[None, :] // CHUNK)
    later = np.stack([same & (t[None, :] > t[:, None]), same & (t[None, :] < t[:, None])]).astype(np.float32)
    whole = (np.arange(tm // CHUNK * SUBLANES)[:, None] // SUBLANES) == (t[None, :] // CHUNK)
    return jnp.asarray(later, BF16), jnp.asarray(whole.astype(np.float32), BF16)


def _pad_to(a, axis, size):
    pad = [(0, 0)] * a.ndim
    pad[axis] = (0, size - a.shape[axis])
    return jnp.pad(a, pad)


def _odd_front(x, mod, norm_g, w_in, p, *, ctx):
    b, t, d = x.shape
    db = p["k_k"].shape[-1]
    tm = min(2 * TILE_TOKENS, t)
    assert t % tm == 0 and tm % PREP_TOKENS == 0 and PREP_TOKENS == 2 * CHUNK
    nc = t // CHUNK
    nb = t // SUBLANES
    per = tm // SUBLANES
    later, whole = _prep_constants(PREP_TOKENS)
    heads = db // RWKV_HEAD
    seg = jnp.asarray(np.kron(np.eye(heads), np.ones((RWKV_HEAD, RWKV_HEAD))), BF16)

    def lora_pair(w_a, w_b):
        rank = w_a.shape[-1]
        down = _pad_to(jnp.concatenate([w_a[0], w_a[1]], axis=-1), 1, LORA_PAD)
        up = jnp.zeros((LORA_PAD, 2 * db), F32)
        up = up.at[0:rank, 0:db].set(w_b[0]).at[rank:2 * rank, db:2 * db].set(w_b[1])
        return down.astype(BF16), up.astype(BF16)

    w1c, w2c = lora_pair(p["w1"], p["w2"])
    a1c, a2c = lora_pair(p["a1"], p["a2"])
    consts = (mod, norm_g.reshape(1, d), w_in.astype(BF16),
              _pad_to(p["rwkv_mu"], 0, SUBLANES), p["w0"].reshape(1, 2 * db), p["a0"].reshape(1, 2 * db),
              w1c, w2c, a1c, a2c,
              _pad_to(p["g1"], 1, LORA_PAD).astype(BF16), _pad_to(p["g2"], 0, LORA_PAD).astype(BF16),
              p["k_k"].reshape(1, db), p["k_a"].reshape(1, db), p["r_k"].reshape(1, db), seg, later, whole)
    full = lambda a: pl.BlockSpec(a.shape, lambda bi, i: (0,) * a.ndim)
    tok = pl.BlockSpec((None, tm, db), lambda bi, i: (bi, i, 0))
    dirtok = pl.BlockSpec((None, 2, tm, db), lambda bi, i: (bi, 0, i, 0))
    tok_shape = jax.ShapeDtypeStruct((b, t, db), F32)
    dir_shape = jax.ShapeDtypeStruct((b, 2, t, db), BF16)
    nrest = 3 * db
    assert w_in.shape[1] == 8 * db
    tok_b = jax.ShapeDtypeStruct((b, t, db), BF16)
    return pl.pallas_call(
        functools.partial(_front_kernel, ctx=ctx),
        grid=(b, t // tm),
        in_specs=[pl.BlockSpec((None, tm, d), lambda bi, i: (bi, i, 0)),
                  pl.BlockSpec((None, SUBLANES, d), lambda bi, i: (bi, jnp.maximum(i * per - 1, 0), 0)),
                  pl.BlockSpec((None, SUBLANES, d), lambda bi, i: (bi, jnp.minimum((i + 1) * per, nb - 1), 0))]
        + [full(a) for a in consts],
        out_specs=[pl.BlockSpec((None, tm, nrest), lambda bi, i: (bi, i, 0)), tok, tok, tok, dirtok, dirtok,
                   pl.BlockSpec((None, 2, tm // CHUNK, db, 4 * CHUNK), lambda bi, i: (bi, 0, i, 0, 0)),
                   pl.BlockSpec((None, 2, tm // CHUNK, SUBLANES, db), lambda bi, i: (bi, 0, i, 0, 0))],
        out_shape=[jax.ShapeDtypeStruct((b, t, nrest), BF16), tok_b, tok_b, tok_b, dir_shape, dir_shape,
                   jax.ShapeDtypeStruct((b, 2, nc, db, 4 * CHUNK), BF16),
                   jax.ShapeDtypeStruct((b, 2, nc, SUBLANES, db), F32)],
        compiler_params=pltpu.CompilerParams(dimension_semantics=("parallel", "parallel"),
                                             vmem_limit_bytes=VMEM_LIMIT),
        name="front_ctx" if ctx else "front_latent",
    )(x, x, x, *consts)


SCAN_PARTS = 1
SCAN_CHUNKS_PER_STEP = 4


def _scan_kernel(at0, rt0, bk0, v0, gc0, at1, rt1, bk1, v1, gc1, s0_ref,
                 hm_ref, ms_ref, mi_ref, idw_ref, bdm_ref, eye_ref,
                 of_ref, or_ref, st_ref):
    @pl.when(pl.program_id(0) == 0)
    def _():
        st_ref[...] = s0_ref[...]

    npart = SCAN_PARTS
    pair = GROUP // 2
    bdm_b = bdm_ref[...]
    bd2 = bdm_b[0:pair, 0:pair].astype(F32)
    eye2 = eye_ref[0:pair, 0:pair]

    def bd(x):
        return [jnp.concatenate([part * hm_ref[h] for h in range(HEADS_PER_GROUP)], axis=0)
                for part in _split(x, npart)]

    def cat_parts(plists):
        return [jnp.concatenate(ps, axis=-1) for ps in zip(*plists)]

    steps = int(np.log2(CHUNK)) - 1

    def load(c):
        at_r, rt_r, bk_r, v_r, gc_r = c["refs"]
        bi, g, sl, q, rows = c["bi"], c["g"], c["sl"], c["q"], c["rows"]
        c.update(at=at_r[bi, rows, sl], rt=rt_r[bi, rows, sl], v=v_r[bi, rows, sl], gc=gc_r[bi, q, 0:1, sl],
                 bk=bk_r[bi, q, g * GROUP:(g + 1) * GROUP, :])

    def interactions(c):
        ms, mi = ms_ref[c["d"]], mi_ref[c["d"]]
        bk = c["bk"]
        wts = jnp.concatenate([jnp.concatenate([bk[:, 0:pair]] * 2, axis=1) * bdm_b,
                               jnp.concatenate([bk[:, pair:]] * 2, axis=1) * bdm_b], axis=1)
        gmat = _dot(jnp.concatenate([c["at"], c["rt"]], axis=0), wts, False)
        lab = gmat[0:CHUNK, 0:GROUP] * ms
        c.update(lak=gmat[0:CHUNK, GROUP:] * ms, mrb=gmat[CHUNK:, 0:GROUP] * mi,
                 mrk=gmat[CHUNK:, GROUP:] * mi, pw=lab, tinv=idw_ref[...] + lab)

    def invert(s):
        def stage(c):
            pw, tinv = c["pw"], c["tinv"]
            wts = bd(pw)
            if s == 0:
                pw = _mmp(_split(pw, npart), wts)
            elif s < steps:
                res = _mmp(_split(jnp.concatenate([pw, tinv], axis=0), npart), wts)
                pw, tinv = res[0:CHUNK], tinv + res[CHUNK:]
            else:
                tinv = tinv + _mmp(_split(tinv, npart), wts)
            c.update(pw=pw, tinv=tinv)
        return stage

    def values(c):
        lv_mv = _mmp(_split(jnp.concatenate([c["lak"], c["mrk"]], axis=0), npart), bd(c["v"]))
        c.update(lv=lv_mv[0:CHUNK], mv=lv_mv[CHUNK:])

    def solve(c):
        aw = _mmp(_split(c["tinv"], npart), cat_parts([bd(c["at"]), bd(c["lv"])]))
        c.update(ahat=aw[:, 0:GROUP], wv=aw[:, GROUP:])

    def outputs(c):
        ro = _mmp(_split(c["mrb"], npart), cat_parts([bd(c["ahat"]), bd(c["wv"])]))
        c.update(rhat=(c["rt"].astype(F32) + ro[:, 0:GROUP]) * c["gc"], ointra=ro[:, GROUP:] + c["mv"])

    def transition(c):
        zero_rows = jnp.zeros((CHUNK, GROUP), BF16)
        zero_half = jnp.zeros((CHUNK, pair), BF16)
        zero_blk = jnp.zeros((pair, pair), F32)
        pms, qms = [], []
        for p in range(2):
            ps = slice(p * pair, (p + 1) * pair)
            top = jnp.concatenate([c["ahat"][:, ps], c["wv"][:, ps]], axis=1).astype(BF16)
            mid = jnp.concatenate([zero_half, c["v"][:, ps]], axis=1)
            rhs = jnp.concatenate([top, zero_rows, mid, zero_rows], axis=0)
            pq = _dot(c["bk"][ps, :], rhs, False)
            pblk = (pq[:, 0:pair] * bd2 + eye2) * c["gc"][:, ps]
            qblk = pq[:, pair:] * bd2
            pms.append(jnp.concatenate([pblk, zero_blk] if p == 0 else [zero_blk, pblk], axis=1))
            qms.append(jnp.concatenate([qblk, zero_blk] if p == 0 else [zero_blk, qblk], axis=1))
        c.update(pm=jnp.concatenate(pms, axis=0), qm=jnp.concatenate(qms, axis=0))

    def advance(c, st):
        rs = _dot(jnp.concatenate([c["rhat"], c["pm"]], axis=0).astype(BF16), st.astype(BF16), False)
        c["out"][c["bi"], c["rows"], c["sl"]] = (rs[0:CHUNK] + c["ointra"]).astype(BF16)
        return rs[CHUNK:] + c["qm"]

    nq = at0.shape[1] // CHUNK
    dirs = ((at0, rt0, bk0, v0, gc0, of_ref), (at1, rt1, bk1, v1, gc1, or_ref))
    groups = {}
    for bi in range(at0.shape[0]):
        for d, refs in enumerate(dirs):
            for g in range(at0.shape[-1] // GROUP):
                order = range(nq) if d == 0 else reversed(range(nq))
                groups[(bi, d, g)] = [
                    dict(q=q, rows=slice(q * CHUNK, (q + 1) * CHUNK), bi=bi, d=d, g=g,
                         sl=slice(g * GROUP, (g + 1) * GROUP), refs=refs[:5], out=refs[5]) for q in order]
    chains = [c for k in range(nq) for cs in groups.values() for c in [cs[k]]]
    stages = ([load, interactions] + [invert(s) for s in range(steps + 1)]
              + [values, solve, outputs, transition])
    for stage in stages:
        for c in chains:
            stage(c)
    states = {key: st_ref[key[0], key[1], key[2]] for key in groups}
    for k in range(nq):
        for key, cs in groups.items():
            states[key] = advance(cs[k], states[key])
    for key in groups:
        st_ref[key[0], key[1], key[2]] = states[key]


def _scan_constants():
    t = np.arange(CHUNK)[:, None]
    lane = np.arange(GROUP)[None, :]
    s = lane % CHUNK
    strict = np.stack([s < t, s > t]).astype(np.float32)
    incl = np.stack([s <= t, s >= t]).astype(np.float32)
    idw = (s == t).astype(np.float32)
    hm = np.stack([np.broadcast_to((lane // RWKV_HEAD) == h, (CHUNK, GROUP)) for h in range(HEADS_PER_GROUP)])
    bdm = np.kron(np.eye(HEADS_PER_GROUP), np.ones((RWKV_HEAD, RWKV_HEAD)))
    return (jnp.asarray(hm.astype(np.float32), BF16), jnp.asarray(strict), jnp.asarray(incl), jnp.asarray(idw),
            jnp.asarray(bdm, BF16), jnp.eye(GROUP, dtype=F32))


def _rwkv_scan(at, rt, bk, v, gc, s0):
    b, _, t, db = at.shape
    nc = t // CHUNK
    consts = _scan_constants()
    full = lambda a: pl.BlockSpec(a.shape, lambda i: (0,) * a.ndim)
    nq = min(SCAN_CHUNKS_PER_STEP, nc)
    ns = nc // nq
    assert nc % nq == 0
    fwd, rev = (lambda i: i), (lambda i: ns - 1 - i)
    dspec = lambda d, ci: pl.BlockSpec((b, None, nq * CHUNK, db), lambda i: (0, d, ci(i), 0))
    kspec = lambda d, ci: pl.BlockSpec((b, None, nq, db, 4 * CHUNK), lambda i: (0, d, ci(i), 0, 0))
    vspec = lambda ci: pl.BlockSpec((b, nq * CHUNK, db), lambda i: (0, ci(i), 0))
    gspec = lambda d, ci: pl.BlockSpec((b, None, nq, SUBLANES, db), lambda i: (0, d, ci(i), 0, 0))
    in_specs = ([dspec(0, fwd), dspec(0, fwd), kspec(0, fwd), vspec(fwd), gspec(0, fwd),
                 dspec(1, rev), dspec(1, rev), kspec(1, rev), vspec(rev), gspec(1, rev), full(s0)]
                + [full(a) for a in consts])
    return pl.pallas_call(
        _scan_kernel,
        grid=(ns,),
        in_specs=in_specs,
        out_specs=[vspec(fwd), vspec(rev), full(s0)],
        out_shape=[jax.ShapeDtypeStruct((b, t, db), BF16), jax.ShapeDtypeStruct((b, t, db), BF16),
                   jax.ShapeDtypeStruct(s0.shape, F32)],
        compiler_params=pltpu.CompilerParams(dimension_semantics=("arbitrary",),
                                             vmem_limit_bytes=VMEM_LIMIT),
        name="rwkv_scan",
    )(at, rt, bk, v, gc, at, rt, bk, v, gc, s0, *consts)


def _conv_kernel(glu_ref, w_ref, b_ref, o_ref, h_ref, *, stride):
    t, c = glu_ref.shape
    halo = (CONF_CONV_W // 2) * stride
    nsteps = t // CONV_ROWS
    zeros = jnp.zeros((halo, c), F32)
    h_ref[0:halo, :] = zeros
    h_ref[halo + t:halo + t + halo, :] = zeros

    def widen(j, carry):
        r0 = pl.multiple_of(j * CONV_ROWS, CONV_ROWS)
        h_ref[pl.ds(halo + r0, CONV_ROWS), :] = glu_ref[pl.ds(r0, CONV_ROWS), :].astype(F32)
        return carry

    lax.fori_loop(0, nsteps, widen, 0)
    w = w_ref[...]
    bias = b_ref[...]

    def conv(j, carry):
        r0 = pl.multiple_of(j * CONV_ROWS, CONV_ROWS)
        acc = jnp.broadcast_to(bias, (CONV_ROWS, c))
        for tap in range(CONF_CONV_W):
            acc = acc + h_ref[pl.ds(r0 + tap * stride, CONV_ROWS), :] * w[tap:tap + 1, :]
        o_ref[pl.ds(r0, CONV_ROWS), :] = acc.astype(BF16)
        return carry

    lax.fori_loop(0, nsteps, conv, 0)


def _conformer_conv(proj, dw_w, dw_b, *, stride):
    b, t, n = proj.shape
    db = dw_w.shape[-1]
    lanes = 128
    assert t % CONV_ROWS == 0 and stride % SUBLANES == 0
    halo = (CONF_CONV_W // 2) * stride
    c1 = db // lanes
    return pl.pallas_call(
        functools.partial(_conv_kernel, stride=stride),
        grid=(b, db // lanes),
        in_specs=[pl.BlockSpec((None, t, lanes), lambda bi, j: (bi, 0, c1 + j)),
                  pl.BlockSpec((_round_up(CONF_CONV_W, SUBLANES), lanes), lambda bi, j: (0, j)),
                  pl.BlockSpec((1, lanes), lambda bi, j: (0, j))],
        out_specs=pl.BlockSpec((None, t, lanes), lambda bi, j: (bi, 0, j)),
        out_shape=jax.ShapeDtypeStruct((b, t, db), BF16),
        scratch_shapes=[pltpu.VMEM((t + 2 * halo, lanes), F32)],
        compiler_params=pltpu.CompilerParams(dimension_semantics=("parallel", "parallel"),
                                             vmem_limit_bytes=VMEM_LIMIT),
        name="conformer_conv",
    )(proj, _pad_to(dw_w, 0, _round_up(CONF_CONV_W, SUBLANES)), dw_b.reshape(1, db))


def _round_up(x, m):
    return (x + m - 1) // m * m


def _merge_kernel(x_ref, of_ref, or_ref, g_ref, bonus_ref, conv_ref, zc_ref, zd_ref, mod_ref,
                  lnxg_ref, lnxb_ref, clng_ref, clnb_ref, wout_ref, fg_ref, seg_ref, o_ref):
    tm, d = x_ref.shape
    mrow = _mod_row(mod_ref, False)
    seg = seg_ref[...]

    def centre(c):
        o = of_ref[c["rows"], :].astype(F32) + or_ref[c["rows"], :].astype(F32)
        c.update(dev=o - _mm(o, seg))

    def normalise(c):
        rows, dev = c["rows"], c["dev"]
        var = _mm(dev * dev, seg)
        on = dev * lax.rsqrt(var + LNX_EPS) * lnxg_ref[...] + lnxb_ref[...]
        y_c = ((on + bonus_ref[rows, :].astype(F32)) * g_ref[rows, :].astype(F32)
               * _silu(zc_ref[rows, :].astype(F32)))
        cv = conv_ref[rows, :].astype(F32)
        cdev = cv - jnp.mean(cv, axis=-1, keepdims=True)
        cvar = jnp.mean(cdev * cdev, axis=-1, keepdims=True)
        ln = cdev * lax.rsqrt(cvar + CONF_LN_EPS) * clng_ref[...] + clnb_ref[...]
        y_d = _silu(ln) * _silu(zd_ref[rows, :].astype(F32))
        c.update(y=jnp.concatenate([y_c, y_d], axis=-1))

    def emit(c):
        y = _mm(c["y"], wout_ref[...])
        o_ref[c["rows"], :] = _rmsnorm(x_ref[c["rows"], :] + mrow[:, 2 * d:3 * d] * y, fg_ref[...])

    ts = tm // MERGE_SUBTILES
    subs = [dict(rows=slice(s * ts, (s + 1) * ts)) for s in range(MERGE_SUBTILES)]
    for stage in (centre, normalise, emit):
        for c in subs:
            stage(c)


def _merge(x, o_f, o_r, g, bonus, conv, proj, mod, p, final_g):
    b, t, d = x.shape
    db = d // 2
    tm = min(4 * TILE_TOKENS, t)
    heads = db // RWKV_HEAD
    seg = jnp.asarray(np.kron(np.eye(heads), np.full((RWKV_HEAD, RWKV_HEAD), 1.0 / RWKV_HEAD)), BF16)
    consts = (mod, p["lnx_g"].reshape(1, db), p["lnx_b"].reshape(1, db), p["conf_ln_g"].reshape(1, db),
              p["conf_ln_b"].reshape(1, db), p["out_o"].astype(BF16), final_g.reshape(1, d), seg)
    full = lambda a: pl.BlockSpec(a.shape, lambda bi, i: (0,) * a.ndim)
    tok = pl.BlockSpec((None, tm, db), lambda bi, i: (bi, i, 0))
    col = lambda cblk: pl.BlockSpec((None, tm, db), lambda bi, i: (bi, i, cblk))
    return pl.pallas_call(
        _merge_kernel,
        grid=(b, t // tm),
        in_specs=[pl.BlockSpec((None, tm, d), lambda bi, i: (bi, i, 0)), tok, tok, tok, tok, tok,
                  col(0), col(2)] + [full(a) for a in consts],
        out_specs=pl.BlockSpec((None, tm, d), lambda bi, i: (bi, i, 0)),
        out_shape=jax.ShapeDtypeStruct((b, t, d), F32),
        compiler_params=pltpu.CompilerParams(dimension_semantics=("parallel", "parallel"),
                                             vmem_limit_bytes=VMEM_LIMIT),
        name="merge",
    )(x, o_f, o_r, g, bonus, conv, proj, proj, *consts)


def kernel(x, c, ctx, c_ctx, ada_w_e, ada_b_e, norm_e, in_e, out_e, pool_w, pool_scale, sconv_w,
           ada_w_o, ada_b_o, norm_o, in_o, out_o, rwkv_mu, w0, w1, w2, a0, a1, a2, g1, g2,
           k_k, k_a, r_k, lnx_g, lnx_b, conf_dw_w, conf_dw_b, conf_ln_g, conf_ln_b, final_g):
    b, t, d = x.shape
    db = d // 2
    assert b == 2
    cvec = _pad_to(jnp.concatenate([c, c_ctx[None, :]], axis=0), 0, SUBLANES)
    mod_e = _adaln(cvec, ada_w_e[0], ada_b_e[0])
    mod_o = _adaln(cvec, ada_w_o[0], ada_b_o[0])

    even = dict(norm_g=norm_e[0], w_in=in_e[0], w_out=out_e[0], pool_w=pool_w[0],
                pool_scale=pool_scale[0], sconv_w=sconv_w[0])
    x1 = _even_layer(x, mod_e, ctx=False, line=GRID_W, **even)
    xc1 = _even_layer(ctx, mod_e, ctx=True, line=ctx.shape[1], **even)

    p = dict(rwkv_mu=rwkv_mu[0], w0=w0[0], w1=w1[0], w2=w2[0], a0=a0[0], a1=a1[0], a2=a2[0],
             g1=g1[0], g2=g2[0], k_k=k_k[0], k_a=k_a[0], r_k=r_k[0], lnx_g=lnx_g[0], lnx_b=lnx_b[0],
             conf_ln_g=conf_ln_g[0], conf_ln_b=conf_ln_b[0], out_o=out_o[0])

    _, v_c, _, _, at_c, rt_c, bk_c, gc_c = _odd_front(xc1, mod_o, norm_o[0], in_o[0], p, ctx=True)
    zero_state = jnp.zeros((b, 2, db // GROUP, GROUP, GROUP), F32)
    _, _, ctx_state = _rwkv_scan(at_c, rt_c, bk_c, v_c, gc_c, zero_state)

    proj, v_l, g_l, bonus, at_l, rt_l, bk_l, gc_l = _odd_front(x1, mod_o, norm_o[0], in_o[0], p, ctx=False)
    o_f, o_r, _ = _rwkv_scan(at_l, rt_l, bk_l, v_l, gc_l, ctx_state)
    conv = _conformer_conv(proj, conf_dw_w[0], conf_dw_b[0], stride=GRID_W)
    return _merge(x1, o_f, o_r, g_l, bonus, conv, proj, mod_o, p, final_g)
```

```python
import functools

import numpy as np
import jax
import jax.numpy as jnp
from jax import lax
from jax.experimental import pallas as pl
from jax.experimental.pallas import tpu as pltpu

F32 = jnp.float32
BF16 = jnp.bfloat16

GRID_W = 64
POOL_WINDOWS = (2, 4, 8, 16)
SHORT_CONV_W = 3
CONF_CONV_W = 31
RWKV_HEAD = 64
NORM_EPS = 1e-6
LNX_EPS = 64e-5
CONF_LN_EPS = 1e-5
DECAY_SCALE = float(np.exp(-0.5))

CHUNK = 64
HEADS_PER_GROUP = 4
GROUP = HEADS_PER_GROUP * RWKV_HEAD
LANES = 128
SUBLANES = 8
LORA_PAD = LANES
VMEM_LIMIT = 52 * 1024 * 1024

TILE_TOKENS = 256
PREP_TOKENS = 128
ADALN_COLS = 768
CONV_ROWS = 128
MERGE_SUBTILES = 8


def _sigmoid(x):
    return 0.5 * jnp.tanh(0.5 * x) + 0.5


def _silu(x):
    return x * _sigmoid(x)


def _split(x, n):
    if x.dtype == BF16:
        return [x]
    parts, r = [], x
    for i in range(n):
        p = r.astype(BF16)
        parts.append(p)
        if i + 1 < n:
            r = r - p.astype(F32)
    return parts


def _dot(a, b, nt):
    dims = (((1,), (1,)), ((), ())) if nt else (((1,), (0,)), ((), ()))
    return lax.dot_general(a, b, dims, preferred_element_type=F32)


def _mmp(ap, bp, nt=False):
    order = max(len(ap), len(bp))
    acc = None
    for i in reversed(range(len(ap))):
        for j in reversed(range(len(bp))):
            if i + j < order:
                t = _dot(ap[i], bp[j], nt)
                acc = t if acc is None else acc + t
    return acc


def _mm(a, b, na=1, nb=1, nt=False):
    return _mmp(_split(a, na), _split(b, nb), nt)


def _rmsnorm(x, g):
    ms = jnp.mean(x * x, axis=-1, keepdims=True)
    return x * lax.rsqrt(ms + NORM_EPS) * g


def _modulated(x, g, mod_row, d):
    shift, scale = mod_row[:, 0:d], mod_row[:, d:2 * d]
    return _rmsnorm(x, g) * (1.0 + scale) + shift


def _mod_row(mod_ref, ctx):
    row = 2 if ctx else pl.program_id(0)
    return mod_ref[pl.ds(row, 1), :]


def _adaln_kernel(c_ref, w_ref, b_ref, o_ref):
    o_ref[...] = _mm(_silu(c_ref[...]), w_ref[...], 2, 2) + b_ref[...]


def _adaln(cvec, w, b):
    d, n = w.shape
    tn = ADALN_COLS
    assert n % tn == 0
    return pl.pallas_call(
        _adaln_kernel,
        grid=(n // tn,),
        in_specs=[pl.BlockSpec((SUBLANES, d), lambda j: (0, 0)),
                  pl.BlockSpec((d, tn), lambda j: (0, j)),
                  pl.BlockSpec((1, tn), lambda j: (0, j))],
        out_specs=pl.BlockSpec((SUBLANES, tn), lambda j: (0, j)),
        out_shape=jax.ShapeDtypeStruct((SUBLANES, n), F32),
        compiler_params=pltpu.CompilerParams(dimension_semantics=("parallel",),
                                             vmem_limit_bytes=VMEM_LIMIT),
        name="adaln",
    )(cvec, w, b.reshape(1, n))


def _even_kernel(x_ref, mod_ref, g_ref, win_ref, wout_ref, poolw_ref, pscale_ref, sconv_ref,
                 band_ref, invc_ref, o_ref, *, ctx, line):
    tm, d = x_ref.shape
    ts = band_ref.shape[-1]
    db = d // 2
    grp = db // len(POOL_WINDOWS)
    mrow = _mod_row(mod_ref, ctx)

    def project(c):
        x = x_ref[c["rows"], :]
        c.update(x=x, proj=_mm(_modulated(x, g_ref[...], mrow, d), win_ref[...]))

    def pool(c):
        u_a = c["proj"][:, 0:db]
        pooled = []
        for i in range(len(POOL_WINDOWS)):
            ug = u_a[:, i * grp:(i + 1) * grp]
            pooled.append(_mm(band_ref[i], ug) * invc_ref[i] - ug)
        c.update(pooled=pooled)

    def mix(c):
        proj = c["proj"]
        z_a = proj[:, db:2 * db]
        v_b, g_b = proj[:, 2 * db:3 * db], proj[:, 3 * db:4 * db]
        g_c, z_b = proj[:, 4 * db:5 * db], proj[:, 5 * db:6 * db]
        ya = [_mm(c["pooled"][i], poolw_ref[i]) for i in range(len(POOL_WINDOWS))]
        y_a = jnp.concatenate(ya, axis=-1) * pscale_ref[...] * _silu(z_a)
        q = g_c * v_b
        pos = lax.broadcasted_iota(jnp.int32, q.shape, 0) % line
        q_prev = jnp.where(pos >= 1, pltpu.roll(q, 1, 0), 0.0)
        q_next = jnp.where(pos <= line - 2, pltpu.roll(q, ts - 1, 0), 0.0)
        w = sconv_ref[...]
        conv = q_prev * w[0:1, :] + q * w[1:2, :] + q_next * w[2:3, :]
        y_b = g_b * conv * _silu(z_b)
        c.update(y=jnp.concatenate([y_a, y_b], axis=-1))

    def emit(c):
        o_ref[c["rows"], :] = c["x"] + mrow[:, 2 * d:3 * d] * _mm(c["y"], wout_ref[...])

    subs = [dict(rows=slice(s * ts, (s + 1) * ts)) for s in range(tm // ts)]
    for stage in (project, pool, mix, emit):
        for c in subs:
            stage(c)


def _pool_constants(tm, line):
    pos = np.arange(tm) % line
    lid = np.arange(tm) // line
    bands, invs = [], []
    for w in POOL_WINDOWS:
        lo = np.clip(pos - w // 2, 0, line)
        hi = np.clip(pos - w // 2 + w, 0, line)
        same = lid[:, None] == lid[None, :]
        band = same & (pos[None, :] >= lo[:, None]) & (pos[None, :] < hi[:, None])
        bands.append(band.astype(np.float32))
        invs.append(np.broadcast_to((1.0 / (hi - lo))[:, None], (tm, LANES)).astype(np.float32))
    return jnp.asarray(np.stack(bands), BF16), jnp.asarray(np.stack(invs), F32)


def _even_layer(x, mod, norm_g, w_in, w_out, pool_w, pool_scale, sconv_w, *, ctx, line):
    b, t, d = x.shape
    tm = min(4 * TILE_TOKENS, t)
    assert t % tm == 0 and tm % line == 0
    db = d // 2
    ts = max(PREP_TOKENS, line)
    assert tm % ts == 0 and ts % line == 0
    band, invc = _pool_constants(ts, line)
    full = lambda a: pl.BlockSpec(a.shape, lambda bi, i: (0,) * a.ndim)
    args = (mod, norm_g.reshape(1, d), w_in.astype(BF16), w_out.astype(BF16), pool_w.astype(BF16),
            pool_scale.reshape(1, db), sconv_w, band, invc)
    return pl.pallas_call(
        functools.partial(_even_kernel, ctx=ctx, line=line),
        grid=(b, t // tm),
        in_specs=[pl.BlockSpec((None, tm, d), lambda bi, i: (bi, i, 0))] + [full(a) for a in args],
        out_specs=pl.BlockSpec((None, tm, d), lambda bi, i: (bi, i, 0)),
        out_shape=jax.ShapeDtypeStruct((b, t, d), F32),
        compiler_params=pltpu.CompilerParams(dimension_semantics=("parallel", "parallel"),
                                             vmem_limit_bytes=VMEM_LIMIT),
        name="even_ctx" if ctx else "even_latent",
    )(x, *args)


def _front_kernel(x_ref, xp_ref, xn_ref, mod_ref, ng_ref, win_ref,
                  mu_ref, w0_ref, a0_ref, w1_ref, w2_ref, a1_ref, a2_ref,
                  g1_ref, g2_ref, kk_ref, ka_ref, rk_ref, seg_ref, later_ref, whole_ref,
                  tok_ref, art_ref, bk_ref, gc_ref, *, ctx):
    tm, d = x_ref.shape
    db = kk_ref.shape[-1]
    ts = PREP_TOKENS
    te = ts + 2 * SUBLANES
    i, n = pl.program_id(1), pl.num_programs(1)
    mrow = _mod_row(mod_ref, ctx)
    h_prev = _modulated(xp_ref[...], ng_ref[...], mrow, d) * jnp.where(i > 0, 1.0, 0.0)
    h_next = _modulated(xn_ref[...], ng_ref[...], mrow, d) * jnp.where(i < n - 1, 1.0, 0.0)
    h = _modulated(x_ref[...], ng_ref[...], mrow, d)
    h_b = h.astype(BF16)
    h_ext = jnp.concatenate([h_prev, h, h_next], axis=0).astype(BF16)
    mu = mu_ref[...]
    first_chunk = lax.broadcasted_iota(jnp.int32, (db, ts), 1) < CHUNK

    def rest(c, j):
        rows = c["rows"]
        if j == 1:
            p12 = _dot(h_b[rows], win_ref[:, 5 * db:7 * db], False)
            tok_ref[rows, db:2 * db] = (p12[:, 0:db] * _sigmoid(p12[:, db:])).astype(BF16)
        else:
            col = 4 if j == 0 else 7
            tok_ref[rows, j * db:(j + 1) * db] = _dot(h_b[rows], win_ref[:, col * db:(col + 1) * db],
                                                       False).astype(BF16)

    def project(c):
        c.update(z_ext=_dot(h_ext[c["s"] * ts:c["s"] * ts + te], win_ref[:, 0:4 * db], False))

    def shift(c):
        z_ext = c["z_ext"]
        nbr = (pltpu.roll(z_ext, 1, 0) + pltpu.roll(z_ext, te - 1, 0))[SUBLANES:ts + SUBLANES]
        z = z_ext[SUBLANES:ts + SUBLANES]
        dz = 0.5 * nbr - z
        u, du = z[:, 0:db], dz[:, 0:db]
        v = z[:, 3 * db:4 * db] + dz[:, 3 * db:4 * db] * mu[2:3, :]
        c.update(r=z[:, db:2 * db] + dz[:, db:2 * db] * mu[0:1, :],
                 k=z[:, 2 * db:3 * db] + dz[:, 2 * db:3 * db] * mu[1:2, :], v=v,
                 uw=u + du * mu[3:4, :], ua=u + du * mu[4:5, :], ug=u + du * mu[5:6, :])
        tok_ref[c["rows"], 3 * db:4 * db] = v.astype(BF16)
        rest(c, 0)

    def loras(c):
        tok_ref[c["rows"], 4 * db:5 * db] = _mm(_sigmoid(_mm(c["ug"], g1_ref[...])), g2_ref[...]).astype(BF16)
        kk = c["k"] * kk_ref[...]
        inv_norm = lax.rsqrt(jnp.maximum(_mm(kk * kk, seg_ref[...]), 1e-24))
        c.update(lw_pre=w0_ref[...] + _mm(jnp.tanh(_mm(c["uw"], w1_ref[...])), w2_ref[...]),
                 a_pre=a0_ref[...] + _mm(_mm(c["ua"], a1_ref[...]), a2_ref[...]),
                 kk=kk * inv_norm, ksum=None)
        rest(c, 1)

    def direction(dr):
        def stage(c):
            rows, k, kk = c["rows"], c["k"], c["kk"]
            lw = -DECAY_SCALE * _sigmoid(c["lw_pre"][:, dr * db:(dr + 1) * db])
            a = _sigmoid(c["a_pre"][:, dr * db:(dr + 1) * db])
            k_d = k * (1.0 + (a - 1.0) * ka_ref[...])
            c["ksum"] = k_d if c["ksum"] is None else c["ksum"] + k_d
            lw_parts = _split(lw, 1)
            rel = -_mmp([later_ref[dr]], lw_parts)
            art_ref[dr, 0, rows, :] = (-kk * jnp.exp(rel - lw)).astype(BF16)
            art_ref[dr, 1, rows, :] = (c["r"] * jnp.exp(rel)).astype(BF16)
            gout = jnp.exp(-rel)
            bh_t = (kk * a * gout).T
            kh_t = (k_d * gout).T
            bh_r, kh_r = pltpu.roll(bh_t, CHUNK, 1), pltpu.roll(kh_t, CHUNK, 1)
            gam = jnp.exp(_mmp([whole_ref[...]], lw_parts))
            for q in range(ts // CHUNK):
                own = first_chunk if q == 0 else jnp.logical_not(first_chunk)
                chunk = c["s"] * (ts // CHUNK) + q
                bk_ref[dr, chunk] = jnp.concatenate([jnp.where(own, bh_t, bh_r), jnp.where(own, kh_t, kh_r)],
                                                    axis=1).astype(BF16)
                gc_ref[dr, chunk] = gam[q * SUBLANES:(q + 1) * SUBLANES, :]
            if dr == 0:
                rest(c, 2)
        return stage

    def bonus(c):
        tok_ref[c["rows"], 5 * db:6 * db] = (_mm(c["r"] * c["ksum"] * rk_ref[...], seg_ref[...])
                                             * c["v"]).astype(BF16)

    subs = [dict(s=s, rows=slice(s * ts, (s + 1) * ts)) for s in range(tm // ts)]
    for stage in (project, shift, loras, direction(0), direction(1), bonus):
        for c in subs:
            stage(c)


def _prep_constants(tm):
    t = np.arange(tm)
    same = (t[:, None] // CHUNK) == (t[None, :] // CHUNK)
    later = np.stack([same & (t[None, :] > t[:, None]), same & (t[None, :] < t[:, None])]).astype(np.float32)
    whole = (np.arange(tm // CHUNK * SUBLANES)[:, None] // SUBLANES) == (t[None, :] // CHUNK)
    return jnp.asarray(later, BF16), jnp.asarray(whole.astype(np.float32), BF16)


def _pad_to(a, axis, size):
    pad = [(0, 0)] * a.ndim
    pad[axis] = (0, size - a.shape[axis])
    return jnp.pad(a, pad)


def _odd_front(x, mod, norm_g, w_in, p, *, ctx):
    b, t, d = x.shape
    db = p["k_k"].shape[-1]
    tm = min(2 * TILE_TOKENS, t)
    assert t % tm == 0 and tm % PREP_TOKENS == 0 and PREP_TOKENS == 2 * CHUNK
    nc = t // CHUNK
    nb = t // SUBLANES
    per = tm // SUBLANES
    later, whole = _prep_constants(PREP_TOKENS)
    heads = db // RWKV_HEAD
    seg = jnp.asarray(np.kron(np.eye(heads), np.ones((RWKV_HEAD, RWKV_HEAD))), BF16)

    def lora_pair(w_a, w_b):
        rank = w_a.shape[-1]
        down = _pad_to(jnp.concatenate([w_a[0], w_a[1]], axis=-1), 1, LORA_PAD)
        up = jnp.zeros((LORA_PAD, 2 * db), F32)
        up = up.at[0:rank, 0:db].set(w_b[0]).at[rank:2 * rank, db:2 * db].set(w_b[1])
        return down.astype(BF16), up.astype(BF16)

    w1c, w2c = lora_pair(p["w1"], p["w2"])
    a1c, a2c = lora_pair(p["a1"], p["a2"])
    consts = (mod, norm_g.reshape(1, d), w_in.astype(BF16),
              _pad_to(p["rwkv_mu"], 0, SUBLANES), p["w0"].reshape(1, 2 * db), p["a0"].reshape(1, 2 * db),
              w1c, w2c, a1c, a2c,
              _pad_to(p["g1"], 1, LORA_PAD).astype(BF16), _pad_to(p["g2"], 0, LORA_PAD).astype(BF16),
              p["k_k"].reshape(1, db), p["k_a"].reshape(1, db), p["r_k"].reshape(1, db), seg, later, whole)
    full = lambda a: pl.BlockSpec(a.shape, lambda bi, i: (0,) * a.ndim)
    ntok = 6 * db
    assert w_in.shape[1] == 8 * db
    return pl.pallas_call(
        functools.partial(_front_kernel, ctx=ctx),
        grid=(b, t // tm),
        in_specs=[pl.BlockSpec((None, tm, d), lambda bi, i: (bi, i, 0)),
                  pl.BlockSpec((None, SUBLANES, d), lambda bi, i: (bi, jnp.maximum(i * per - 1, 0), 0)),
                  pl.BlockSpec((None, SUBLANES, d), lambda bi, i: (bi, jnp.minimum((i + 1) * per, nb - 1), 0))]
        + [full(a) for a in consts],
        out_specs=[pl.BlockSpec((None, tm, ntok), lambda bi, i: (bi, i, 0)),
                   pl.BlockSpec((None, 2, 2, tm, db), lambda bi, i: (bi, 0, 0, i, 0)),
                   pl.BlockSpec((None, 2, tm // CHUNK, db, 4 * CHUNK), lambda bi, i: (bi, 0, i, 0, 0)),
                   pl.BlockSpec((None, 2, tm // CHUNK, SUBLANES, db), lambda bi, i: (bi, 0, i, 0, 0))],
        out_shape=[jax.ShapeDtypeStruct((b, t, ntok), BF16), jax.ShapeDtypeStruct((b, 2, 2, t, db), BF16),
                   jax.ShapeDtypeStruct((b, 2, nc, db, 4 * CHUNK), BF16),
                   jax.ShapeDtypeStruct((b, 2, nc, SUBLANES, db), F32)],
        compiler_params=pltpu.CompilerParams(dimension_semantics=("parallel", "parallel"),
                                             vmem_limit_bytes=VMEM_LIMIT),
        name="front_ctx" if ctx else "front_latent",
    )(x, x, x, *consts)


SCAN_PARTS = 1
SCAN_CHUNKS_PER_STEP = 4


def _scan_kernel(ar0, bk0, v0, gc0, ar1, bk1, v1, gc1, s0_ref,
                 hm_ref, ms_ref, mi_ref, idw_ref, bdm_ref, eye_ref,
                 of_ref, or_ref, st_ref):
    @pl.when(pl.program_id(0) == 0)
    def _():
        st_ref[...] = s0_ref[...]

    npart = SCAN_PARTS
    pair = GROUP // 2
    bdm_b = bdm_ref[...]
    bd2 = bdm_b[0:pair, 0:pair].astype(F32)
    eye2 = eye_ref[0:pair, 0:pair]

    def bd(x):
        return [jnp.concatenate([part * hm_ref[h] for h in range(HEADS_PER_GROUP)], axis=0)
                for part in _split(x, npart)]

    def cat_parts(plists):
        return [jnp.concatenate(ps, axis=-1) for ps in zip(*plists)]

    steps = int(np.log2(CHUNK)) - 1

    def load(c):
        ar_r, bk_r, v_r, gc_r = c["refs"]
        bi, g, sl, q, rows = c["bi"], c["g"], c["sl"], c["q"], c["rows"]
        c.update(at=ar_r[bi, 0, rows, sl], rt=ar_r[bi, 1, rows, sl], v=v_r[bi, rows, sl], gc=gc_r[bi, q, 0:1, sl],
                 bk=bk_r[bi, q, g * GROUP:(g + 1) * GROUP, :])

    def interactions(c):
        ms, mi = ms_ref[c["d"]], mi_ref[c["d"]]
        bk = c["bk"]
        wts = jnp.concatenate([jnp.concatenate([bk[:, 0:pair]] * 2, axis=1) * bdm_b,
                               jnp.concatenate([bk[:, pair:]] * 2, axis=1) * bdm_b], axis=1)
        gmat = _dot(jnp.concatenate([c["at"], c["rt"]], axis=0), wts, False)
        lab = gmat[0:CHUNK, 0:GROUP] * ms
        c.update(lak=gmat[0:CHUNK, GROUP:] * ms, mrb=gmat[CHUNK:, 0:GROUP] * mi,
                 mrk=gmat[CHUNK:, GROUP:] * mi, pw=lab, tinv=idw_ref[...] + lab)

    def invert(s):
        def stage(c):
            pw, tinv = c["pw"], c["tinv"]
            wts = bd(pw)
            if s == 0:
                pw = _mmp(_split(pw, npart), wts)
            elif s < steps:
                res = _mmp(_split(jnp.concatenate([pw, tinv], axis=0), npart), wts)
                pw, tinv = res[0:CHUNK], tinv + res[CHUNK:]
            else:
                tinv = tinv + _mmp(_split(tinv, npart), wts)
            c.update(pw=pw, tinv=tinv)
        return stage

    def values(c):
        lv_mv = _mmp(_split(jnp.concatenate([c["lak"], c["mrk"]], axis=0), npart), bd(c["v"]))
        c.update(lv=lv_mv[0:CHUNK], mv=lv_mv[CHUNK:])

    def solve(c):
        aw = _mmp(_split(c["tinv"], npart), cat_parts([bd(c["at"]), bd(c["lv"])]))
        c.update(ahat=aw[:, 0:GROUP], wv=aw[:, GROUP:])

    def outputs(c):
        ro = _mmp(_split(c["mrb"], npart), cat_parts([bd(c["ahat"]), bd(c["wv"])]))
        c.update(rhat=(c["rt"].astype(F32) + ro[:, 0:GROUP]) * c["gc"], ointra=ro[:, GROUP:] + c["mv"])

    def transition(c):
        zero_rows = jnp.zeros((CHUNK, GROUP), BF16)
        zero_half = jnp.zeros((CHUNK, pair), BF16)
        zero_blk = jnp.zeros((pair, pair), F32)
        pms, qms = [], []
        for p in range(2):
            ps = slice(p * pair, (p + 1) * pair)
            top = jnp.concatenate([c["ahat"][:, ps], c["wv"][:, ps]], axis=1).astype(BF16)
            mid = jnp.concatenate([zero_half, c["v"][:, ps]], axis=1)
            rhs = jnp.concatenate([top, zero_rows, mid, zero_rows], axis=0)
            pq = _dot(c["bk"][ps, :], rhs, False)
            pblk = (pq[:, 0:pair] * bd2 + eye2) * c["gc"][:, ps]
            qblk = pq[:, pair:] * bd2
            pms.append(jnp.concatenate([pblk, zero_blk] if p == 0 else [zero_blk, pblk], axis=1))
            qms.append(jnp.concatenate([qblk, zero_blk] if p == 0 else [zero_blk, qblk], axis=1))
        c.update(pm=jnp.concatenate(pms, axis=0), qm=jnp.concatenate(qms, axis=0))

    def advance(c, st):
        rs = _dot(jnp.concatenate([c["rhat"], c["pm"]], axis=0).astype(BF16), st.astype(BF16), False)
        c["out"][c["bi"], c["rows"], c["sl"]] = (rs[0:CHUNK] + c["ointra"]).astype(BF16)
        return rs[CHUNK:] + c["qm"]

    nq = v0.shape[1] // CHUNK
    dirs = ((ar0, bk0, v0, gc0, of_ref), (ar1, bk1, v1, gc1, or_ref))
    groups = {}
    for bi in range(v0.shape[0]):
        for d, refs in enumerate(dirs):
            for g in range(v0.shape[-1] // GROUP):
                order = range(nq) if d == 0 else reversed(range(nq))
                groups[(bi, d, g)] = [
                    dict(q=q, rows=slice(q * CHUNK, (q + 1) * CHUNK), bi=bi, d=d, g=g,
                         sl=slice(g * GROUP, (g + 1) * GROUP), refs=refs[:4], out=refs[4]) for q in order]
    chains = [c for k in range(nq) for cs in groups.values() for c in [cs[k]]]
    stages = ([load, interactions] + [invert(s) for s in range(steps + 1)]
              + [values, solve, outputs, transition])
    for stage in stages:
        for c in chains:
            stage(c)
    states = {key: st_ref[key[0], key[1], key[2]] for key in groups}
    for k in range(nq):
        for key, cs in groups.items():
            states[key] = advance(cs[k], states[key])
    for key in groups:
        st_ref[key[0], key[1], key[2]] = states[key]


def _scan_constants():
    t = np.arange(CHUNK)[:, None]
    lane = np.arange(GROUP)[None, :]
    s = lane % CHUNK
    strict = np.stack([s < t, s > t]).astype(np.float32)
    incl = np.stack([s <= t, s >= t]).astype(np.float32)
    idw = (s == t).astype(np.float32)
    hm = np.stack([np.broadcast_to((lane // RWKV_HEAD) == h, (CHUNK, GROUP)) for h in range(HEADS_PER_GROUP)])
    bdm = np.kron(np.eye(HEADS_PER_GROUP), np.ones((RWKV_HEAD, RWKV_HEAD)))
    return (jnp.asarray(hm.astype(np.float32), BF16), jnp.asarray(strict), jnp.asarray(incl), jnp.asarray(idw),
            jnp.asarray(bdm, BF16), jnp.eye(GROUP, dtype=F32))


def _rwkv_scan(art, bk, tok, gc, s0):
    b, _, _, t, db = art.shape
    nc = t // CHUNK
    consts = _scan_constants()
    full = lambda a: pl.BlockSpec(a.shape, lambda i: (0,) * a.ndim)
    nq = min(SCAN_CHUNKS_PER_STEP, nc)
    ns = nc // nq
    assert nc % nq == 0
    fwd, rev = (lambda i: i), (lambda i: ns - 1 - i)
    dspec = lambda d, ci: pl.BlockSpec((b, None, 2, nq * CHUNK, db), lambda i: (0, d, 0, ci(i), 0))
    kspec = lambda d, ci: pl.BlockSpec((b, None, nq, db, 4 * CHUNK), lambda i: (0, d, ci(i), 0, 0))
    vspec = lambda ci, col=0: pl.BlockSpec((b, nq * CHUNK, db), lambda i: (0, ci(i), col))
    gspec = lambda d, ci: pl.BlockSpec((b, None, nq, SUBLANES, db), lambda i: (0, d, ci(i), 0, 0))
    in_specs = ([dspec(0, fwd), kspec(0, fwd), vspec(fwd, 3), gspec(0, fwd),
                 dspec(1, rev), kspec(1, rev), vspec(rev, 3), gspec(1, rev), full(s0)]
                + [full(a) for a in consts])
    return pl.pallas_call(
        _scan_kernel,
        grid=(ns,),
        in_specs=in_specs,
        out_specs=[vspec(fwd), vspec(rev), full(s0)],
        out_shape=[jax.ShapeDtypeStruct((b, t, db), BF16), jax.ShapeDtypeStruct((b, t, db), BF16),
                   jax.ShapeDtypeStruct(s0.shape, F32)],
        compiler_params=pltpu.CompilerParams(dimension_semantics=("arbitrary",),
                                             vmem_limit_bytes=VMEM_LIMIT),
        name="rwkv_scan",
    )(art, bk, tok, gc, art, bk, tok, gc, s0, *consts)


def _conv_kernel(glu_ref, w_ref, b_ref, o_ref, h_ref, *, stride):
    t, c = glu_ref.shape
    halo = (CONF_CONV_W // 2) * stride
    nsteps = t // CONV_ROWS
    zeros = jnp.zeros((halo, c), F32)
    h_ref[0:halo, :] = zeros
    h_ref[halo + t:halo + t + halo, :] = zeros

    def widen(j, carry):
        r0 = pl.multiple_of(j * CONV_ROWS, CONV_ROWS)
        h_ref[pl.ds(halo + r0, CONV_ROWS), :] = glu_ref[pl.ds(r0, CONV_ROWS), :].astype(F32)
        return carry

    lax.fori_loop(0, nsteps, widen, 0)
    w = w_ref[...]
    bias = b_ref[...]

    def conv(j, carry):
        r0 = pl.multiple_of(j * CONV_ROWS, CONV_ROWS)
        acc = jnp.broadcast_to(bias, (CONV_ROWS, c))
        for tap in range(CONF_CONV_W):
            acc = acc + h_ref[pl.ds(r0 + tap * stride, CONV_ROWS), :] * w[tap:tap + 1, :]
        o_ref[pl.ds(r0, CONV_ROWS), :] = acc.astype(BF16)
        return carry

    lax.fori_loop(0, nsteps, conv, 0)


def _conformer_conv(proj, dw_w, dw_b, *, stride):
    b, t, n = proj.shape
    db = dw_w.shape[-1]
    lanes = LANES
    assert t % CONV_ROWS == 0 and stride % SUBLANES == 0
    halo = (CONF_CONV_W // 2) * stride
    c1 = db // lanes
    return pl.pallas_call(
        functools.partial(_conv_kernel, stride=stride),
        grid=(b, db // lanes),
        in_specs=[pl.BlockSpec((None, t, lanes), lambda bi, j: (bi, 0, c1 + j)),
                  pl.BlockSpec((_round_up(CONF_CONV_W, SUBLANES), lanes), lambda bi, j: (0, j)),
                  pl.BlockSpec((1, lanes), lambda bi, j: (0, j))],
        out_specs=pl.BlockSpec((None, t, lanes), lambda bi, j: (bi, 0, j)),
        out_shape=jax.ShapeDtypeStruct((b, t, db), BF16),
        scratch_shapes=[pltpu.VMEM((t + 2 * halo, lanes), F32)],
        compiler_params=pltpu.CompilerParams(dimension_semantics=("parallel", "parallel"),
                                             vmem_limit_bytes=VMEM_LIMIT),
        name="conformer_conv",
    )(proj, _pad_to(dw_w, 0, _round_up(CONF_CONV_W, SUBLANES)), dw_b.reshape(1, db))


def _round_up(x, m):
    return (x + m - 1) // m * m


def _merge_kernel(x_ref, of_ref, or_ref, g_ref, bonus_ref, conv_ref, zc_ref, zd_ref, mod_ref,
                  lnxg_ref, lnxb_ref, clng_ref, clnb_ref, wout_ref, fg_ref, seg_ref, o_ref):
    tm, d = x_ref.shape
    mrow = _mod_row(mod_ref, False)
    seg = seg_ref[...]

    def centre(c):
        o = of_ref[c["rows"], :].astype(F32) + or_ref[c["rows"], :].astype(F32)
        c.update(dev=o - _mm(o, seg))

    def normalise(c):
        rows, dev = c["rows"], c["dev"]
        var = _mm(dev * dev, seg)
        on = dev * lax.rsqrt(var + LNX_EPS) * lnxg_ref[...] + lnxb_ref[...]
        y_c = ((on + bonus_ref[rows, :].astype(F32)) * g_ref[rows, :].astype(F32)
               * _silu(zc_ref[rows, :].astype(F32)))
        cv = conv_ref[rows, :].astype(F32)
        cdev = cv - jnp.mean(cv, axis=-1, keepdims=True)
        cvar = jnp.mean(cdev * cdev, axis=-1, keepdims=True)
        ln = cdev * lax.rsqrt(cvar + CONF_LN_EPS) * clng_ref[...] + clnb_ref[...]
        y_d = _silu(ln) * _silu(zd_ref[rows, :].astype(F32))
        c.update(y=jnp.concatenate([y_c, y_d], axis=-1))

    def emit(c):
        y = _mm(c["y"], wout_ref[...])
        o_ref[c["rows"], :] = _rmsnorm(x_ref[c["rows"], :] + mrow[:, 2 * d:3 * d] * y, fg_ref[...])

    ts = tm // MERGE_SUBTILES
    subs = [dict(rows=slice(s * ts, (s + 1) * ts)) for s in range(MERGE_SUBTILES)]
    for stage in (centre, normalise, emit):
        for c in subs:
            stage(c)


def _merge(x, o_f, o_r, conv, tok_all, mod, p, final_g):
    b, t, d = x.shape
    db = d // 2
    tm = min(4 * TILE_TOKENS, t)
    heads = db // RWKV_HEAD
    seg = jnp.asarray(np.kron(np.eye(heads), np.full((RWKV_HEAD, RWKV_HEAD), 1.0 / RWKV_HEAD)), BF16)
    consts = (mod, p["lnx_g"].reshape(1, db), p["lnx_b"].reshape(1, db), p["conf_ln_g"].reshape(1, db),
              p["conf_ln_b"].reshape(1, db), p["out_o"].astype(BF16), final_g.reshape(1, d), seg)
    full = lambda a: pl.BlockSpec(a.shape, lambda bi, i: (0,) * a.ndim)
    tok = pl.BlockSpec((None, tm, db), lambda bi, i: (bi, i, 0))
    col = lambda cblk: pl.BlockSpec((None, tm, db), lambda bi, i: (bi, i, cblk))
    return pl.pallas_call(
        _merge_kernel,
        grid=(b, t // tm),
        in_specs=[pl.BlockSpec((None, tm, d), lambda bi, i: (bi, i, 0)), tok, tok, col(4), col(5), tok,
                  col(0), col(2)] + [full(a) for a in consts],
        out_specs=pl.BlockSpec((None, tm, d), lambda bi, i: (bi, i, 0)),
        out_shape=jax.ShapeDtypeStruct((b, t, d), F32),
        compiler_params=pltpu.CompilerParams(dimension_semantics=("parallel", "parallel"),
                                             vmem_limit_bytes=VMEM_LIMIT),
        name="merge",
    )(x, o_f, o_r, tok_all, tok_all, conv, tok_all, tok_all, *consts)


def kernel(x, c, ctx, c_ctx, ada_w_e, ada_b_e, norm_e, in_e, out_e, pool_w, pool_scale, sconv_w,
           ada_w_o, ada_b_o, norm_o, in_o, out_o, rwkv_mu, w0, w1, w2, a0, a1, a2, g1, g2,
           k_k, k_a, r_k, lnx_g, lnx_b, conf_dw_w, conf_dw_b, conf_ln_g, conf_ln_b, final_g):
    b, t, d = x.shape
    db = d // 2
    assert b == 2
    cvec = _pad_to(jnp.concatenate([c, c_ctx[None, :]], axis=0), 0, SUBLANES)
    mod_e = _adaln(cvec, ada_w_e[0], ada_b_e[0])
    mod_o = _adaln(cvec, ada_w_o[0], ada_b_o[0])

    even = dict(norm_g=norm_e[0], w_in=in_e[0], w_out=out_e[0], pool_w=pool_w[0],
                pool_scale=pool_scale[0], sconv_w=sconv_w[0])
    x1 = _even_layer(x, mod_e, ctx=False, line=GRID_W, **even)
    xc1 = _even_layer(ctx, mod_e, ctx=True, line=ctx.shape[1], **even)

    p = dict(rwkv_mu=rwkv_mu[0], w0=w0[0], w1=w1[0], w2=w2[0], a0=a0[0], a1=a1[0], a2=a2[0],
             g1=g1[0], g2=g2[0], k_k=k_k[0], k_a=k_a[0], r_k=r_k[0], lnx_g=lnx_g[0], lnx_b=lnx_b[0],
             conf_ln_g=conf_ln_g[0], conf_ln_b=conf_ln_b[0], out_o=out_o[0])

    tok_c, art_c, bk_c, gc_c = _odd_front(xc1, mod_o, norm_o[0], in_o[0], p, ctx=True)
    zero_state = jnp.zeros((b, 2, db // GROUP, GROUP, GROUP), F32)
    _, _, ctx_state = _rwkv_scan(art_c, bk_c, tok_c, gc_c, zero_state)

    tok_l, art_l, bk_l, gc_l = _odd_front(x1, mod_o, norm_o[0], in_o[0], p, ctx=False)
    o_f, o_r, _ = _rwkv_scan(art_l, bk_l, tok_l, gc_l, ctx_state)
    conv = _conformer_conv(tok_l, conf_dw_w[0], conf_dw_b[0], stride=GRID_W)
    return _merge(x1, o_f, o_r, conv, tok_l, mod_o, p, final_g)
```

```python
import functools

import numpy as np
import jax
import jax.numpy as jnp
from jax import lax
from jax.experimental import pallas as pl
from jax.experimental.pallas import tpu as pltpu

F32 = jnp.float32
BF16 = jnp.bfloat16

GRID_W = 64
POOL_WINDOWS = (2, 4, 8, 16)
SHORT_CONV_W = 3
CONF_CONV_W = 31
RWKV_HEAD = 64
NORM_EPS = 1e-6
LNX_EPS = 64e-5
CONF_LN_EPS = 1e-5
DECAY_SCALE = float(np.exp(-0.5))

CHUNK = 64
HEADS_PER_GROUP = 4
GROUP = HEADS_PER_GROUP * RWKV_HEAD
LANES = 128
SUBLANES = 8
LORA_PAD = LANES
VMEM_LIMIT = 52 * 1024 * 1024

TILE_TOKENS = 256
PREP_TOKENS = 128
ADALN_COLS = 768
CONV_ROWS = 128
MERGE_SUBTILES = 8


def _sigmoid(x):
    return 0.5 * jnp.tanh(0.5 * x) + 0.5


def _silu(x):
    return x * _sigmoid(x)


def _split(x, n):
    if x.dtype == BF16:
        return [x]
    parts, r = [], x
    for i in range(n):
        p = r.astype(BF16)
        parts.append(p)
        if i + 1 < n:
            r = r - p.astype(F32)
    return parts


def _dot(a, b, nt):
    dims = (((1,), (1,)), ((), ())) if nt else (((1,), (0,)), ((), ()))
    return lax.dot_general(a, b, dims, preferred_element_type=F32)


def _mmp(ap, bp, nt=False):
    order = max(len(ap), len(bp))
    acc = None
    for i in reversed(range(len(ap))):
        for j in reversed(range(len(bp))):
            if i + j < order:
                t = _dot(ap[i], bp[j], nt)
                acc = t if acc is None else acc + t
    return acc


def _mm(a, b, na=1, nb=1, nt=False):
    return _mmp(_split(a, na), _split(b, nb), nt)


def _rmsnorm(x, g):
    ms = jnp.mean(x * x, axis=-1, keepdims=True)
    return x * lax.rsqrt(ms + NORM_EPS) * g


def _modulated(x, g, mod_row, d):
    shift, scale = mod_row[:, 0:d], mod_row[:, d:2 * d]
    return _rmsnorm(x, g) * (1.0 + scale) + shift


def _mod_row(mod_ref, ctx):
    row = 2 if ctx else pl.program_id(0)
    return mod_ref[pl.ds(row, 1), :]


def _adaln_kernel(c_ref, w_ref, b_ref, o_ref):
    o_ref[...] = _mm(_silu(c_ref[...]), w_ref[...], 2, 2) + b_ref[...]


def _adaln(cvec, w, b):
    d, n = w.shape
    tn = ADALN_COLS
    assert n % tn == 0
    return pl.pallas_call(
        _adaln_kernel,
        grid=(n // tn,),
        in_specs=[pl.BlockSpec((SUBLANES, d), lambda j: (0, 0)),
                  pl.BlockSpec((d, tn), lambda j: (0, j)),
                  pl.BlockSpec((1, tn), lambda j: (0, j))],
        out_specs=pl.BlockSpec((SUBLANES, tn), lambda j: (0, j)),
        out_shape=jax.ShapeDtypeStruct((SUBLANES, n), F32),
        compiler_params=pltpu.CompilerParams(dimension_semantics=("parallel",),
                                             vmem_limit_bytes=VMEM_LIMIT),
        name="adaln",
    )(cvec, w, b.reshape(1, n))


def _even_kernel(x_ref, mod_ref, g_ref, win_ref, wout_ref, poolw_ref, pscale_ref, sconv_ref,
                 band_ref, invc_ref, o_ref, *, ctx, line):
    tm, d = x_ref.shape
    ts = band_ref.shape[-1]
    db = d // 2
    grp = db // len(POOL_WINDOWS)
    mrow = _mod_row(mod_ref, ctx)

    def project(c):
        x = x_ref[c["rows"], :]
        c.update(x=x, proj=_mm(_modulated(x, g_ref[...], mrow, d), win_ref[...]))

    def pool(c):
        u_a = c["proj"][:, 0:db]
        pooled = []
        for i in range(len(POOL_WINDOWS)):
            ug = u_a[:, i * grp:(i + 1) * grp]
            pooled.append(_mm(band_ref[i], ug) * invc_ref[i] - ug)
        c.update(pooled=pooled)

    def mix(c):
        proj = c["proj"]
        z_a = proj[:, db:2 * db]
        v_b, g_b = proj[:, 2 * db:3 * db], proj[:, 3 * db:4 * db]
        g_c, z_b = proj[:, 4 * db:5 * db], proj[:, 5 * db:6 * db]
        ya = [_mm(c["pooled"][i], poolw_ref[i]) for i in range(len(POOL_WINDOWS))]
        y_a = jnp.concatenate(ya, axis=-1) * pscale_ref[...] * _silu(z_a)
        q = g_c * v_b
        pos = lax.broadcasted_iota(jnp.int32, q.shape, 0) % line
        q_prev = jnp.where(pos >= 1, pltpu.roll(q, 1, 0), 0.0)
        q_next = jnp.where(pos <= line - 2, pltpu.roll(q, ts - 1, 0), 0.0)
        w = sconv_ref[...]
        conv = q_prev * w[0:1, :] + q * w[1:2, :] + q_next * w[2:3, :]
        y_b = g_b * conv * _silu(z_b)
        c.update(y=jnp.concatenate([y_a, y_b], axis=-1))

    def emit(c):
        o_ref[c["rows"], :] = c["x"] + mrow[:, 2 * d:3 * d] * _mm(c["y"], wout_ref[...])

    subs = [dict(rows=slice(s * ts, (s + 1) * ts)) for s in range(tm // ts)]
    for stage in (project, pool, mix, emit):
        for c in subs:
            stage(c)


def _pool_constants(tm, line):
    pos = np.arange(tm) % line
    lid = np.arange(tm) // line
    bands, invs = [], []
    for w in POOL_WINDOWS:
        lo = np.clip(pos - w // 2, 0, line)
        hi = np.clip(pos - w // 2 + w, 0, line)
        same = lid[:, None] == lid[None, :]
        band = same & (pos[None, :] >= lo[:, None]) & (pos[None, :] < hi[:, None])
        bands.append(band.astype(np.float32))
        invs.append(np.broadcast_to((1.0 / (hi - lo))[:, None], (tm, LANES)).astype(np.float32))
    return jnp.asarray(np.stack(bands), BF16), jnp.asarray(np.stack(invs), F32)


def _even_layer(x, mod, norm_g, w_in, w_out, pool_w, pool_scale, sconv_w, *, ctx, line):
    b, t, d = x.shape
    tm = min(4 * TILE_TOKENS, t)
    assert t % tm == 0 and tm % line == 0
    db = d // 2
    ts = max(PREP_TOKENS, line)
    assert tm % ts == 0 and ts % line == 0
    band, invc = _pool_constants(ts, line)
    full = lambda a: pl.BlockSpec(a.shape, lambda bi, i: (0,) * a.ndim)
    args = (mod, norm_g.reshape(1, d), w_in.astype(BF16), w_out.astype(BF16), pool_w.astype(BF16),
            pool_scale.reshape(1, db), sconv_w, band, invc)
    return pl.pallas_call(
        functools.partial(_even_kernel, ctx=ctx, line=line),
        grid=(b, t // tm),
        in_specs=[pl.BlockSpec((None, tm, d), lambda bi, i: (bi, i, 0))] + [full(a) for a in args],
        out_specs=pl.BlockSpec((None, tm, d), lambda bi, i: (bi, i, 0)),
        out_shape=jax.ShapeDtypeStruct((b, t, d), F32),
        compiler_params=pltpu.CompilerParams(dimension_semantics=("parallel", "parallel"),
                                             vmem_limit_bytes=VMEM_LIMIT),
        name="even_ctx" if ctx else "even_latent",
    )(x, *args)


def _front_kernel(x_ref, xp_ref, xn_ref, mod_ref, ng_ref, win_ref,
                  mu_ref, w0_ref, a0_ref, w1_ref, w2_ref, a1_ref, a2_ref,
                  g1_ref, g2_ref, kk_ref, ka_ref, rk_ref, seg_ref, later_ref, whole_ref,
                  rest_ref, v_ref, g_ref, bonus_ref, at_ref, rt_ref, bk_ref, gc_ref, *, ctx):
    tm, d = x_ref.shape
    db = v_ref.shape[-1]
    ts = PREP_TOKENS
    te = ts + 2 * SUBLANES
    i, n = pl.program_id(1), pl.num_programs(1)
    mrow = _mod_row(mod_ref, ctx)
    h_prev = _modulated(xp_ref[...], ng_ref[...], mrow, d) * jnp.where(i > 0, 1.0, 0.0)
    h_next = _modulated(xn_ref[...], ng_ref[...], mrow, d) * jnp.where(i < n - 1, 1.0, 0.0)
    h = _modulated(x_ref[...], ng_ref[...], mrow, d)
    h_b = h.astype(BF16)
    h_ext = jnp.concatenate([h_prev, h, h_next], axis=0).astype(BF16)
    mu = mu_ref[...]
    first_chunk = lax.broadcasted_iota(jnp.int32, (db, ts), 1) < CHUNK

    def rest(c, j):
        rows = c["rows"]
        if j == 1:
            p12 = _dot(h_b[rows], win_ref[:, 5 * db:7 * db], False)
            rest_ref[rows, db:2 * db] = (p12[:, 0:db] * _sigmoid(p12[:, db:])).astype(BF16)
        else:
            col = 4 if j == 0 else 7
            rest_ref[rows, j * db:(j + 1) * db] = _dot(h_b[rows], win_ref[:, col * db:(col + 1) * db],
                                                        False).astype(BF16)

    def project(c):
        c.update(z_ext=_dot(h_ext[c["s"] * ts:c["s"] * ts + te], win_ref[:, 0:4 * db], False))

    def shift(c):
        z_ext = c["z_ext"]
        nbr = (pltpu.roll(z_ext, 1, 0) + pltpu.roll(z_ext, te - 1, 0))[SUBLANES:ts + SUBLANES]
        z = z_ext[SUBLANES:ts + SUBLANES]
        dz = 0.5 * nbr - z
        u, du = z[:, 0:db], dz[:, 0:db]
        v = z[:, 3 * db:4 * db] + dz[:, 3 * db:4 * db] * mu[2:3, :]
        c.update(r=z[:, db:2 * db] + dz[:, db:2 * db] * mu[0:1, :],
                 k=z[:, 2 * db:3 * db] + dz[:, 2 * db:3 * db] * mu[1:2, :], v=v,
                 uw=u + du * mu[3:4, :], ua=u + du * mu[4:5, :], ug=u + du * mu[5:6, :])
        v_ref[c["rows"], :] = v.astype(BF16)
        rest(c, 0)

    def loras(c):
        g_ref[c["rows"], :] = _mm(_sigmoid(_mm(c["ug"], g1_ref[...])), g2_ref[...]).astype(BF16)
        kk = c["k"] * kk_ref[...]
        inv_norm = lax.rsqrt(jnp.maximum(_mm(kk * kk, seg_ref[...]), 1e-24))
        c.update(lw_pre=w0_ref[...] + _mm(jnp.tanh(_mm(c["uw"], w1_ref[...])), w2_ref[...]),
                 a_pre=a0_ref[...] + _mm(_mm(c["ua"], a1_ref[...]), a2_ref[...]),
                 kk=kk * inv_norm, ksum=None)
        rest(c, 1)

    def direction(dr):
        def stage(c):
            rows, k, kk = c["rows"], c["k"], c["kk"]
            lw = -DECAY_SCALE * _sigmoid(c["lw_pre"][:, dr * db:(dr + 1) * db])
            a = _sigmoid(c["a_pre"][:, dr * db:(dr + 1) * db])
            k_d = k * (1.0 + (a - 1.0) * ka_ref[...])
            c["ksum"] = k_d if c["ksum"] is None else c["ksum"] + k_d
            lw_parts = _split(lw, 1)
            rel = -_mmp([later_ref[dr]], lw_parts)
            at_ref[dr, rows, :] = (-kk * jnp.exp(rel - lw)).astype(BF16)
            rt_ref[dr, rows, :] = (c["r"] * jnp.exp(rel)).astype(BF16)
            gout = jnp.exp(-rel)
            bh_t = (kk * a * gout).T
            kh_t = (k_d * gout).T
            bh_r, kh_r = pltpu.roll(bh_t, CHUNK, 1), pltpu.roll(kh_t, CHUNK, 1)
            gam = jnp.exp(_mmp([whole_ref[...]], lw_parts))
            for q in range(ts // CHUNK):
                own = first_chunk if q == 0 else jnp.logical_not(first_chunk)
                chunk = c["s"] * (ts // CHUNK) + q
                bk_ref[dr, chunk] = jnp.concatenate([jnp.where(own, bh_t, bh_r), jnp.where(own, kh_t, kh_r)],
                                                    axis=1).astype(BF16)
                gc_ref[dr, chunk] = gam[q * SUBLANES:(q + 1) * SUBLANES, :]
            if dr == 0:
                rest(c, 2)
        return stage

    def bonus(c):
        bonus_ref[c["rows"], :] = (_mm(c["r"] * c["ksum"] * rk_ref[...], seg_ref[...]) * c["v"]).astype(BF16)

    subs = [dict(s=s, rows=slice(s * ts, (s + 1) * ts)) for s in range(tm // ts)]
    for stage in (project, shift, loras, direction(0), direction(1), bonus):
        for c in subs:
            stage(c)


def _prep_constants(tm):
    t = np.arange(tm)
    same = (t[:, None] // CHUNK) == (t[None, :] // CHUNK)
    later = np.stack([same & (t[None, :] > t[:, None]), same & (t[None, :] < t[:, None])]).astype(np.float32)
    whole = (np.arange(tm // CHUNK * SUBLANES)[:, None] // SUBLANES) == (t[None, :] // CHUNK)
    return jnp.asarray(later, BF16), jnp.asarray(whole.astype(np.float32), BF16)


def _pad_to(a, axis, size):
    pad = [(0, 0)] * a.ndim
    pad[axis] = (0, size - a.shape[axis])
    return jnp.pad(a, pad)


def _odd_front(x, mod, norm_g, w_in, p, *, ctx):
    b, t, d = x.shape
    db = p["k_k"].shape[-1]
    tm = min(2 * TILE_TOKENS, t)
    assert t % tm == 0 and tm % PREP_TOKENS == 0 and PREP_TOKENS == 2 * CHUNK
    nc = t // CHUNK
    nb = t // SUBLANES
    per = tm // SUBLANES
    later, whole = _prep_constants(PREP_TOKENS)
    heads = db // RWKV_HEAD
    seg = jnp.asarray(np.kron(np.eye(heads), np.ones((RWKV_HEAD, RWKV_HEAD))), BF16)

    def lora_pair(w_a, w_b):
        rank = w_a.shape[-1]
        down = _pad_to(jnp.concatenate([w_a[0], w_a[1]], axis=-1), 1, LORA_PAD)
        up = jnp.zeros((LORA_PAD, 2 * db), F32)
        up = up.at[0:rank, 0:db].set(w_b[0]).at[rank:2 * rank, db:2 * db].set(w_b[1])
        return down.astype(BF16), up.astype(BF16)

    w1c, w2c = lora_pair(p["w1"], p["w2"])
    a1c, a2c = lora_pair(p["a1"], p["a2"])
    consts = (mod, norm_g.reshape(1, d), w_in.astype(BF16),
              _pad_to(p["rwkv_mu"], 0, SUBLANES), p["w0"].reshape(1, 2 * db), p["a0"].reshape(1, 2 * db),
              w1c, w2c, a1c, a2c,
              _pad_to(p["g1"], 1, LORA_PAD).astype(BF16), _pad_to(p["g2"], 0, LORA_PAD).astype(BF16),
              p["k_k"].reshape(1, db), p["k_a"].reshape(1, db), p["r_k"].reshape(1, db), seg, later, whole)
    full = lambda a: pl.BlockSpec(a.shape, lambda bi, i: (0,) * a.ndim)
    tok = pl.BlockSpec((None, tm, db), lambda bi, i: (bi, i, 0))
    dirtok = pl.BlockSpec((None, 2, tm, db), lambda bi, i: (bi, 0, i, 0))
    dir_shape = jax.ShapeDtypeStruct((b, 2, t, db), BF16)
    nrest = 3 * db
    assert w_in.shape[1] == 8 * db
    tok_b = jax.ShapeDtypeStruct((b, t, db), BF16)
    return pl.pallas_call(
        functools.partial(_front_kernel, ctx=ctx),
        grid=(b, t // tm),
        in_specs=[pl.BlockSpec((None, tm, d), lambda bi, i: (bi, i, 0)),
                  pl.BlockSpec((None, SUBLANES, d), lambda bi, i: (bi, jnp.maximum(i * per - 1, 0), 0)),
                  pl.BlockSpec((None, SUBLANES, d), lambda bi, i: (bi, jnp.minimum((i + 1) * per, nb - 1), 0))]
        + [full(a) for a in consts],
        out_specs=[pl.BlockSpec((None, tm, nrest), lambda bi, i: (bi, i, 0)), tok, tok, tok, dirtok, dirtok,
                   pl.BlockSpec((None, 2, tm // CHUNK, db, 4 * CHUNK), lambda bi, i: (bi, 0, i, 0, 0)),
                   pl.BlockSpec((None, 2, tm // CHUNK, SUBLANES, db), lambda bi, i: (bi, 0, i, 0, 0))],
        out_shape=[jax.ShapeDtypeStruct((b, t, nrest), BF16), tok_b, tok_b, tok_b, dir_shape, dir_shape,
                   jax.ShapeDtypeStruct((b, 2, nc, db, 4 * CHUNK), BF16),
                   jax.ShapeDtypeStruct((b, 2, nc, SUBLANES, db), F32)],
        compiler_params=pltpu.CompilerParams(dimension_semantics=("parallel", "parallel"),
                                             vmem_limit_bytes=VMEM_LIMIT),
        name="front_ctx" if ctx else "front_latent",
    )(x, x, x, *consts)


SCAN_PARTS = 1
SCAN_CHUNKS_PER_STEP = 4


def _scan_kernel(at0, rt0, bk0, v0, gc0, at1, rt1, bk1, v1, gc1, s0_ref,
                 hm_ref, ms_ref, mi_ref, idw_ref, bdm_ref, eye_ref,
                 of_ref, or_ref, st_ref):
    @pl.when(pl.program_id(0) == 0)
    def _():
        st_ref[...] = s0_ref[...]

    npart = SCAN_PARTS
    pair = GROUP // 2
    bdm_b = bdm_ref[...]
    bd2 = bdm_b[0:pair, 0:pair].astype(F32)
    eye2 = eye_ref[0:pair, 0:pair]

    def bd(x):
        return [jnp.concatenate([part * hm_ref[h] for h in range(HEADS_PER_GROUP)], axis=0)
                for part in _split(x, npart)]

    def cat_parts(plists):
        return [jnp.concatenate(ps, axis=-1) for ps in zip(*plists)]

    steps = int(np.log2(CHUNK)) - 1

    def load(c):
        at_r, rt_r, bk_r, v_r, gc_r = c["refs"]
        bi, g, sl, q, rows = c["bi"], c["g"], c["sl"], c["q"], c["rows"]
        c.update(at=at_r[bi, rows, sl], rt=rt_r[bi, rows, sl], v=v_r[bi, rows, sl], gc=gc_r[bi, q, 0:1, sl],
                 bk=bk_r[bi, q, g * GROUP:(g + 1) * GROUP, :])

    def interactions(c):
        ms, mi = ms_ref[c["d"]], mi_ref[c["d"]]
        bk = c["bk"]
        wts = jnp.concatenate([jnp.concatenate([bk[:, 0:pair]] * 2, axis=1) * bdm_b,
                               jnp.concatenate([bk[:, pair:]] * 2, axis=1) * bdm_b], axis=1)
        gmat = _dot(jnp.concatenate([c["at"], c["rt"]], axis=0), wts, False)
        lab = gmat[0:CHUNK, 0:GROUP] * ms
        c.update(lak=gmat[0:CHUNK, GROUP:] * ms, mrb=gmat[CHUNK:, 0:GROUP] * mi,
                 mrk=gmat[CHUNK:, GROUP:] * mi, pw=lab, tinv=idw_ref[...] + lab)

    def invert(s):
        def stage(c):
            pw, tinv = c["pw"], c["tinv"]
            wts = bd(pw)
            if s == 0:
                pw = _mmp(_split(pw, npart), wts)
            elif s < steps:
                res = _mmp(_split(jnp.concatenate([pw, tinv], axis=0), npart), wts)
                pw, tinv = res[0:CHUNK], tinv + res[CHUNK:]
            else:
                tinv = tinv + _mmp(_split(tinv, npart), wts)
            c.update(pw=pw, tinv=tinv)
        return stage

    def values(c):
        lv_mv = _mmp(_split(jnp.concatenate([c["lak"], c["mrk"]], axis=0), npart), bd(c["v"]))
        c.update(lv=lv_mv[0:CHUNK], mv=lv_mv[CHUNK:])

    def solve(c):
        aw = _mmp(_split(c["tinv"], npart), cat_parts([bd(c["at"]), bd(c["lv"])]))
        c.update(ahat=aw[:, 0:GROUP], wv=aw[:, GROUP:])

    def outputs(c):
        ro = _mmp(_split(c["mrb"], npart), cat_parts([bd(c["ahat"]), bd(c["wv"])]))
        c.update(rhat=(c["rt"].astype(F32) + ro[:, 0:GROUP]) * c["gc"], ointra=ro[:, GROUP:] + c["mv"])

    def transition(c):
        zero_rows = jnp.zeros((CHUNK, GROUP), BF16)
        zero_half = jnp.zeros((CHUNK, pair), BF16)
        zero_blk = jnp.zeros((pair, pair), F32)
        pms, qms = [], []
        for p in range(2):
            ps = slice(p * pair, (p + 1) * pair)
            top = jnp.concatenate([c["ahat"][:, ps], c["wv"][:, ps]], axis=1).astype(BF16)
            mid = jnp.concatenate([zero_half, c["v"][:, ps]], axis=1)
            rhs = jnp.concatenate([top, zero_rows, mid, zero_rows], axis=0)
            pq = _dot(c["bk"][ps, :], rhs, False)
            pblk = (pq[:, 0:pair] * bd2 + eye2) * c["gc"][:, ps]
            qblk = pq[:, pair:] * bd2
            pms.append(jnp.concatenate([pblk, zero_blk] if p == 0 else [zero_blk, pblk], axis=1))
            qms.append(jnp.concatenate([qblk, zero_blk] if p == 0 else [zero_blk, qblk], axis=1))
        c.update(pm=jnp.concatenate(pms, axis=0), qm=jnp.concatenate(qms, axis=0))

    def advance(c, st):
        rs = _dot(jnp.concatenate([c["rhat"], c["pm"]], axis=0).astype(BF16), st.astype(BF16), False)
        c["out"][c["bi"], c["rows"], c["sl"]] = (rs[0:CHUNK] + c["ointra"]).astype(BF16)
        return rs[CHUNK:] + c["qm"]

    nq = at0.shape[1] // CHUNK
    dirs = ((at0, rt0, bk0, v0, gc0, of_ref), (at1, rt1, bk1, v1, gc1, or_ref))
    groups = {}
    for bi in range(at0.shape[0]):
        for d, refs in enumerate(dirs):
            for g in range(at0.shape[-1] // GROUP):
                order = range(nq) if d == 0 else reversed(range(nq))
                groups[(bi, d, g)] = [
                    dict(q=q, rows=slice(q * CHUNK, (q + 1) * CHUNK), bi=bi, d=d, g=g,
                         sl=slice(g * GROUP, (g + 1) * GROUP), refs=refs[:5], out=refs[5]) for q in order]
    chains = [c for k in range(nq) for cs in groups.values() for c in [cs[k]]]
    stages = ([load, interactions] + [invert(s) for s in range(steps + 1)]
              + [values, solve, outputs, transition])
    for stage in stages:
        for c in chains:
            stage(c)
    states = {key: st_ref[key[0], key[1], key[2]] for key in groups}
    for k in range(nq):
        for key, cs in groups.items():
            states[key] = advance(cs[k], states[key])
    for key in groups:
        st_ref[key[0], key[1], key[2]] = states[key]


def _scan_constants():
    t = np.arange(CHUNK)[:, None]
    lane = np.arange(GROUP)[None, :]
    s = lane % CHUNK
    strict = np.stack([s < t, s > t]).astype(np.float32)
    incl = np.stack([s <= t, s >= t]).astype(np.float32)
    idw = (s == t).astype(np.float32)
    hm = np.stack([np.broadcast_to((lane // RWKV_HEAD) == h, (CHUNK, GROUP)) for h in range(HEADS_PER_GROUP)])
    bdm = np.kron(np.eye(HEADS_PER_GROUP), np.ones((RWKV_HEAD, RWKV_HEAD)))
    return (jnp.asarray(hm.astype(np.float32), BF16), jnp.asarray(strict), jnp.asarray(incl), jnp.asarray(idw),
            jnp.asarray(bdm, BF16), jnp.eye(GROUP, dtype=F32))


def _rwkv_scan(at, rt, bk, v, gc, s0):
    b, _, t, db = at.shape
    nc = t // CHUNK
    consts = _scan_constants()
    full = lambda a: pl.BlockSpec(a.shape, lambda i: (0,) * a.ndim)
    nq = min(SCAN_CHUNKS_PER_STEP, nc)
    ns = nc // nq
    assert nc % nq == 0
    fwd, rev = (lambda i: i), (lambda i: ns - 1 - i)
    dspec = lambda d, ci: pl.BlockSpec((b, None, nq * CHUNK, db), lambda i: (0, d, ci(i), 0))
    kspec = lambda d, ci: pl.BlockSpec((b, None, nq, db, 4 * CHUNK), lambda i: (0, d, ci(i), 0, 0))
    vspec = lambda ci: pl.BlockSpec((b, nq * CHUNK, db), lambda i: (0, ci(i), 0))
    gspec = lambda d, ci: pl.BlockSpec((b, None, nq, SUBLANES, db), lambda i: (0, d, ci(i), 0, 0))
    in_specs = ([dspec(0, fwd), dspec(0, fwd), kspec(0, fwd), vspec(fwd), gspec(0, fwd),
                 dspec(1, rev), dspec(1, rev), kspec(1, rev), vspec(rev), gspec(1, rev), full(s0)]
                + [full(a) for a in consts])
    return pl.pallas_call(
        _scan_kernel,
        grid=(ns,),
        in_specs=in_specs,
        out_specs=[vspec(fwd), vspec(rev), full(s0)],
        out_shape=[jax.ShapeDtypeStruct((b, t, db), BF16), jax.ShapeDtypeStruct((b, t, db), BF16),
                   jax.ShapeDtypeStruct(s0.shape, F32)],
        compiler_params=pltpu.CompilerParams(dimension_semantics=("arbitrary",),
                                             vmem_limit_bytes=VMEM_LIMIT),
        name="rwkv_scan",
    )(at, rt, bk, v, gc, at, rt, bk, v, gc, s0, *consts)


def _conv_kernel(glu_ref, w_ref, b_ref, o_ref, h_ref, *, stride):
    t, c = glu_ref.shape
    halo = (CONF_CONV_W // 2) * stride
    nsteps = t // CONV_ROWS
    zeros = jnp.zeros((halo, c), F32)
    h_ref[0:halo, :] = zeros
    h_ref[halo + t:halo + t + halo, :] = zeros

    def widen(j, carry):
        r0 = pl.multiple_of(j * CONV_ROWS, CONV_ROWS)
        h_ref[pl.ds(halo + r0, CONV_ROWS), :] = glu_ref[pl.ds(r0, CONV_ROWS), :].astype(F32)
        return carry

    lax.fori_loop(0, nsteps, widen, 0)
    w = w_ref[...]
    bias = b_ref[...]

    def conv(j, carry):
        r0 = pl.multiple_of(j * CONV_ROWS, CONV_ROWS)
        acc = jnp.broadcast_to(bias, (CONV_ROWS, c))
        for tap in range(CONF_CONV_W):
            acc = acc + h_ref[pl.ds(r0 + tap * stride, CONV_ROWS), :] * w[tap:tap + 1, :]
        o_ref[pl.ds(r0, CONV_ROWS), :] = acc.astype(BF16)
        return carry

    lax.fori_loop(0, nsteps, conv, 0)


def _conformer_conv(proj, dw_w, dw_b, *, stride):
    b, t, n = proj.shape
    db = dw_w.shape[-1]
    lanes = LANES
    assert t % CONV_ROWS == 0 and stride % SUBLANES == 0
    halo = (CONF_CONV_W // 2) * stride
    c1 = db // lanes
    return pl.pallas_call(
        functools.partial(_conv_kernel, stride=stride),
        grid=(b, db // lanes),
        in_specs=[pl.BlockSpec((None, t, lanes), lambda bi, j: (bi, 0, c1 + j)),
                  pl.BlockSpec((_round_up(CONF_CONV_W, SUBLANES), lanes), lambda bi, j: (0, j)),
                  pl.BlockSpec((1, lanes), lambda bi, j: (0, j))],
        out_specs=pl.BlockSpec((None, t, lanes), lambda bi, j: (bi, 0, j)),
        out_shape=jax.ShapeDtypeStruct((b, t, db), BF16),
        scratch_shapes=[pltpu.VMEM((t + 2 * halo, lanes), F32)],
        compiler_params=pltpu.CompilerParams(dimension_semantics=("parallel", "parallel"),
                                             vmem_limit_bytes=VMEM_LIMIT),
        name="conformer_conv",
    )(proj, _pad_to(dw_w, 0, _round_up(CONF_CONV_W, SUBLANES)), dw_b.reshape(1, db))


def _round_up(x, m):
    return (x + m - 1) // m * m


def _merge_kernel(x_ref, of_ref, or_ref, g_ref, bonus_ref, conv_ref, zc_ref, zd_ref, mod_ref,
                  lnxg_ref, lnxb_ref, clng_ref, clnb_ref, wout_ref, fg_ref, seg_ref, o_ref):
    tm, d = x_ref.shape
    mrow = _mod_row(mod_ref, False)
    seg = seg_ref[...]

    def centre(c):
        o = of_ref[c["rows"], :].astype(F32) + or_ref[c["rows"], :].astype(F32)
        c.update(dev=o - _mm(o, seg))

    def normalise(c):
        rows, dev = c["rows"], c["dev"]
        var = _mm(dev * dev, seg)
        on = dev * lax.rsqrt(var + LNX_EPS) * lnxg_ref[...] + lnxb_ref[...]
        y_c = ((on + bonus_ref[rows, :].astype(F32)) * g_ref[rows, :].astype(F32)
               * _silu(zc_ref[rows, :].astype(F32)))
        cv = conv_ref[rows, :].astype(F32)
        cdev = cv - jnp.mean(cv, axis=-1, keepdims=True)
        cvar = jnp.mean(cdev * cdev, axis=-1, keepdims=True)
        ln = cdev * lax.rsqrt(cvar + CONF_LN_EPS) * clng_ref[...] + clnb_ref[...]
        y_d = _silu(ln) * _silu(zd_ref[rows, :].astype(F32))
        c.update(y=jnp.concatenate([y_c, y_d], axis=-1))

    def emit(c):
        y = _mm(c["y"], wout_ref[...])
        o_ref[c["rows"], :] = _rmsnorm(x_ref[c["rows"], :] + mrow[:, 2 * d:3 * d] * y, fg_ref[...])

    ts = tm // MERGE_SUBTILES
    subs = [dict(rows=slice(s * ts, (s + 1) * ts)) for s in range(MERGE_SUBTILES)]
    for stage in (centre, normalise, emit):
        for c in subs:
            stage(c)


def _merge(x, o_f, o_r, g, bonus, conv, proj, mod, p, final_g):
    b, t, d = x.shape
    db = d // 2
    tm = min(4 * TILE_TOKENS, t)
    heads = db // RWKV_HEAD
    seg = jnp.asarray(np.kron(np.eye(heads), np.full((RWKV_HEAD, RWKV_HEAD), 1.0 / RWKV_HEAD)), BF16)
    consts = (mod, p["lnx_g"].reshape(1, db), p["lnx_b"].reshape(1, db), p["conf_ln_g"].reshape(1, db),
              p["conf_ln_b"].reshape(1, db), p["out_o"].astype(BF16), final_g.reshape(1, d), seg)
    full = lambda a: pl.BlockSpec(a.shape, lambda bi, i: (0,) * a.ndim)
    tok = pl.BlockSpec((None, tm, db), lambda bi, i: (bi, i, 0))
    col = lambda cblk: pl.BlockSpec((None, tm, db), lambda bi, i: (bi, i, cblk))
    return pl.pallas_call(
        _merge_kernel,
        grid=(b, t // tm),
        in_specs=[pl.BlockSpec((None, tm, d), lambda bi, i: (bi, i, 0)), tok, tok, tok, tok, tok,
                  col(0), col(2)] + [full(a) for a in consts],
        out_specs=pl.BlockSpec((None, tm, d), lambda bi, i: (bi, i, 0)),
        out_shape=jax.ShapeDtypeStruct((b, t, d), F32),
        compiler_params=pltpu.CompilerParams(dimension_semantics=("parallel", "parallel"),
                                             vmem_limit_bytes=VMEM_LIMIT),
        name="merge",
    )(x, o_f, o_r, g, bonus, conv, proj, proj, *consts)


def kernel(x, c, ctx, c_ctx, ada_w_e, ada_b_e, norm_e, in_e, out_e, pool_w, pool_scale, sconv_w,
           ada_w_o, ada_b_o, norm_o, in_o, out_o, rwkv_mu, w0, w1, w2, a0, a1, a2, g1, g2,
           k_k, k_a, r_k, lnx_g, lnx_b, conf_dw_w, conf_dw_b, conf_ln_g, conf_ln_b, final_g):
    b, t, d = x.shape
    db = d // 2
    assert b == 2
    cvec = _pad_to(jnp.concatenate([c, c_ctx[None, :]], axis=0), 0, SUBLANES)
    mod_e = _adaln(cvec, ada_w_e[0], ada_b_e[0])
    mod_o = _adaln(cvec, ada_w_o[0], ada_b_o[0])

    even = dict(norm_g=norm_e[0], w_in=in_e[0], w_out=out_e[0], pool_w=pool_w[0],
                pool_scale=pool_scale[0], sconv_w=sconv_w[0])
    x1 = _even_layer(x, mod_e, ctx=False, line=GRID_W, **even)
    xc1 = _even_layer(ctx, mod_e, ctx=True, line=ctx.shape[1], **even)

    p = dict(rwkv_mu=rwkv_mu[0], w0=w0[0], w1=w1[0], w2=w2[0], a0=a0[0], a1=a1[0], a2=a2[0],
             g1=g1[0], g2=g2[0], k_k=k_k[0], k_a=k_a[0], r_k=r_k[0], lnx_g=lnx_g[0], lnx_b=lnx_b[0],
             conf_ln_g=conf_ln_g[0], conf_ln_b=conf_ln_b[0], out_o=out_o[0])

    _, v_c, _, _, at_c, rt_c, bk_c, gc_c = _odd_front(xc1, mod_o, norm_o[0], in_o[0], p, ctx=True)
    zero_state = jnp.zeros((b, 2, db // GROUP, GROUP, GROUP), F32)
    _, _, ctx_state = _rwkv_scan(at_c, rt_c, bk_c, v_c, gc_c, zero_state)

    proj, v_l, g_l, bonus, at_l, rt_l, bk_l, gc_l = _odd_front(x1, mod_o, norm_o[0], in_o[0], p, ctx=False)
    o_f, o_r, _ = _rwkv_scan(at_l, rt_l, bk_l, v_l, gc_l, ctx_state)
    conv = _conformer_conv(proj, conf_dw_w[0], conf_dw_b[0], stride=GRID_W)
    return _merge(x1, o_f, o_r, g_l, bonus, conv, proj, mod_o, p, final_g)
```

```python
import functools

import numpy as np
import jax
import jax.numpy as jnp
from jax import lax
from jax.experimental import pallas as pl
from jax.experimental.pallas import tpu as pltpu

F32 = jnp.float32
BF16 = jnp.bfloat16

GRID_W = 64
POOL_WINDOWS = (2, 4, 8, 16)
SHORT_CONV_W = 3
CONF_CONV_W = 31
RWKV_HEAD = 64
NORM_EPS = 1e-6
LNX_EPS = 64e-5
CONF_LN_EPS = 1e-5
DECAY_SCALE = float(np.exp(-0.5))

CHUNK = 64
HEADS_PER_GROUP = 4
GROUP = HEADS_PER_GROUP * RWKV_HEAD
LANES = 128
SUBLANES = 8
LORA_PAD = LANES
VMEM_LIMIT = 52 * 1024 * 1024

TILE_TOKENS = 256
PREP_TOKENS = 128
ADALN_COLS = 768
CONV_ROWS = 256
MERGE_SUBTILES = 4


def _sigmoid(x):
    return 0.5 * jnp.tanh(0.5 * x) + 0.5


def _silu(x):
    return x * _sigmoid(x)


def _split(x, n):
    if x.dtype == BF16:
        return [x]
    parts, r = [], x
    for i in range(n):
        p = r.astype(BF16)
        parts.append(p)
        if i + 1 < n:
            r = r - p.astype(F32)
    return parts


def _dot(a, b, nt):
    dims = (((1,), (1,)), ((), ())) if nt else (((1,), (0,)), ((), ()))
    return lax.dot_general(a, b, dims, preferred_element_type=F32)


def _mmp(ap, bp, nt=False):
    order = max(len(ap), len(bp))
    acc = None
    for i in reversed(range(len(ap))):
        for j in reversed(range(len(bp))):
            if i + j < order:
                t = _dot(ap[i], bp[j], nt)
                acc = t if acc is None else acc + t
    return acc


def _mm(a, b, na=1, nb=1, nt=False):
    return _mmp(_split(a, na), _split(b, nb), nt)


def _rmsnorm(x, g):
    ms = jnp.mean(x * x, axis=-1, keepdims=True)
    return x * lax.rsqrt(ms + NORM_EPS) * g


def _modulated(x, g, mod_row, d):
    shift, scale = mod_row[:, 0:d], mod_row[:, d:2 * d]
    return _rmsnorm(x, g) * (1.0 + scale) + shift


def _mod_row(mod_ref, ctx):
    row = 2 if ctx else pl.program_id(0)
    return mod_ref[pl.ds(row, 1), :]


def _adaln_kernel(c_ref, w_ref, b_ref, o_ref):
    o_ref[...] = _mm(_silu(c_ref[...]), w_ref[...], 2, 2) + b_ref[...]


def _adaln(cvec, w, b):
    d, n = w.shape
    tn = ADALN_COLS
    assert n % tn == 0
    return pl.pallas_call(
        _adaln_kernel,
        grid=(n // tn,),
        in_specs=[pl.BlockSpec((SUBLANES, d), lambda j: (0, 0)),
                  pl.BlockSpec((d, tn), lambda j: (0, j)),
                  pl.BlockSpec((1, tn), lambda j: (0, j))],
        out_specs=pl.BlockSpec((SUBLANES, tn), lambda j: (0, j)),
        out_shape=jax.ShapeDtypeStruct((SUBLANES, n), F32),
        compiler_params=pltpu.CompilerParams(dimension_semantics=("parallel",),
                                             vmem_limit_bytes=VMEM_LIMIT),
        name="adaln",
    )(cvec, w, b.reshape(1, n))


def _even_kernel(x_ref, mod_ref, g_ref, win_ref, wout_ref, poolw_ref, pscale_ref, sconv_ref,
                 band_ref, invc_ref, o_ref, *, ctx, line):
    tm, d = x_ref.shape
    ts = band_ref.shape[-1]
    db = d // 2
    grp = db // len(POOL_WINDOWS)
    mrow = _mod_row(mod_ref, ctx)

    def project(c):
        x = x_ref[c["rows"], :]
        c.update(x=x, proj=_mm(_modulated(x, g_ref[...], mrow, d), win_ref[...]))

    def pool(c):
        u_a = c["proj"][:, 0:db]
        pooled = []
        for i in range(len(POOL_WINDOWS)):
            ug = u_a[:, i * grp:(i + 1) * grp]
            pooled.append(_mm(band_ref[i], ug) * invc_ref[i] - ug)
        c.update(pooled=pooled)

    def mix(c):
        proj = c["proj"]
        z_a = proj[:, db:2 * db]
        v_b, g_b = proj[:, 2 * db:3 * db], proj[:, 3 * db:4 * db]
        g_c, z_b = proj[:, 4 * db:5 * db], proj[:, 5 * db:6 * db]
        ya = [_mm(c["pooled"][i], poolw_ref[i]) for i in range(len(POOL_WINDOWS))]
        y_a = jnp.concatenate(ya, axis=-1) * pscale_ref[...] * _silu(z_a)
        q = g_c * v_b
        pos = lax.broadcasted_iota(jnp.int32, q.shape, 0) % line
        q_prev = jnp.where(pos >= 1, pltpu.roll(q, 1, 0), 0.0)
        q_next = jnp.where(pos <= line - 2, pltpu.roll(q, ts - 1, 0), 0.0)
        w = sconv_ref[...]
        conv = q_prev * w[0:1, :] + q * w[1:2, :] + q_next * w[2:3, :]
        y_b = g_b * conv * _silu(z_b)
        c.update(y=jnp.concatenate([y_a, y_b], axis=-1))

    def emit(c):
        o_ref[c["rows"], :] = c["x"] + mrow[:, 2 * d:3 * d] * _mm(c["y"], wout_ref[...])

    subs = [dict(rows=slice(s * ts, (s + 1) * ts)) for s in range(tm // ts)]
    for stage in (project, pool, mix, emit):
        for c in subs:
            stage(c)


def _pool_constants(tm, line):
    pos = np.arange(tm) % line
    lid = np.arange(tm) // line
    bands, invs = [], []
    for w in POOL_WINDOWS:
        lo = np.clip(pos - w // 2, 0, line)
        hi = np.clip(pos - w // 2 + w, 0, line)
        same = lid[:, None] == lid[None, :]
        band = same & (pos[None, :] >= lo[:, None]) & (pos[None, :] < hi[:, None])
        bands.append(band.astype(np.float32))
        invs.append(np.broadcast_to((1.0 / (hi - lo))[:, None], (tm, LANES)).astype(np.float32))
    return jnp.asarray(np.stack(bands), BF16), jnp.asarray(np.stack(invs), F32)


def _even_layer(x, mod, norm_g, w_in, w_out, pool_w, pool_scale, sconv_w, *, ctx, line):
    b, t, d = x.shape
    tm = min(4 * TILE_TOKENS, t)
    assert t % tm == 0 and tm % line == 0
    db = d // 2
    ts = max(PREP_TOKENS, line)
    assert tm % ts == 0 and ts % line == 0
    band, invc = _pool_constants(ts, line)
    full = lambda a: pl.BlockSpec(a.shape, lambda bi, i: (0,) * a.ndim)
    args = (mod, norm_g.reshape(1, d), w_in.astype(BF16), w_out.astype(BF16), pool_w.astype(BF16),
            pool_scale.reshape(1, db), sconv_w, band, invc)
    return pl.pallas_call(
        functools.partial(_even_kernel, ctx=ctx, line=line),
        grid=(b, t // tm),
        in_specs=[pl.BlockSpec((None, tm, d), lambda bi, i: (bi, i, 0))] + [full(a) for a in args],
        out_specs=pl.BlockSpec((None, tm, d), lambda bi, i: (bi, i, 0)),
        out_shape=jax.ShapeDtypeStruct((b, t, d), F32),
        compiler_params=pltpu.CompilerParams(dimension_semantics=("parallel", "parallel"),
                                             vmem_limit_bytes=VMEM_LIMIT),
        name="even_ctx" if ctx else "even_latent",
    )(x, *args)


def _front_kernel(x_ref, xp_ref, xn_ref, mod_ref, ng_ref, win_ref,
                  mu_ref, w0_ref, a0_ref, w1_ref, w2_ref, a1_ref, a2_ref,
                  g1_ref, g2_ref, kk_ref, ka_ref, rk_ref, seg_ref, later_ref, whole_ref,
                  rest_ref, v_ref, g_ref, bonus_ref, at_ref, rt_ref, bk_ref, gc_ref, *, ctx):
    tm, d = x_ref.shape
    db = v_ref.shape[-1]
    ts = PREP_TOKENS
    te = ts + 2 * SUBLANES
    i, n = pl.program_id(1), pl.num_programs(1)
    mrow = _mod_row(mod_ref, ctx)
    h_prev = _modulated(xp_ref[...], ng_ref[...], mrow, d) * jnp.where(i > 0, 1.0, 0.0)
    h_next = _modulated(xn_ref[...], ng_ref[...], mrow, d) * jnp.where(i < n - 1, 1.0, 0.0)
    h = _modulated(x_ref[...], ng_ref[...], mrow, d)
    h_b = h.astype(BF16)
    h_ext = jnp.concatenate([h_prev, h, h_next], axis=0).astype(BF16)
    mu = mu_ref[...]
    first_chunk = lax.broadcasted_iota(jnp.int32, (db, ts), 1) < CHUNK

    def rest(c, j):
        rows = c["rows"]
        if j == 1:
            p12 = _dot(h_b[rows], win_ref[:, 5 * db:7 * db], False)
            rest_ref[rows, db:2 * db] = (p12[:, 0:db] * _sigmoid(p12[:, db:])).astype(BF16)
        else:
            col = 4 if j == 0 else 7
            rest_ref[rows, j * db:(j + 1) * db] = _dot(h_b[rows], win_ref[:, col * db:(col + 1) * db],
                                                        False).astype(BF16)

    def project(c):
        c.update(z_ext=_dot(h_ext[c["s"] * ts:c["s"] * ts + te], win_ref[:, 0:4 * db], False))

    def shift(c):
        z_ext = c["z_ext"]
        nbr = (pltpu.roll(z_ext, 1, 0) + pltpu.roll(z_ext, te - 1, 0))[SUBLANES:ts + SUBLANES]
        z = z_ext[SUBLANES:ts + SUBLANES]
        dz = 0.5 * nbr - z
        u, du = z[:, 0:db], dz[:, 0:db]
        v = z[:, 3 * db:4 * db] + dz[:, 3 * db:4 * db] * mu[2:3, :]
        c.update(r=z[:, db:2 * db] + dz[:, db:2 * db] * mu[0:1, :],
                 k=z[:, 2 * db:3 * db] + dz[:, 2 * db:3 * db] * mu[1:2, :], v=v,
                 uw=u + du * mu[3:4, :], ua=u + du * mu[4:5, :], ug=u + du * mu[5:6, :])
        v_ref[c["rows"], :] = v.astype(BF16)
        rest(c, 0)

    def loras(c):
        g_ref[c["rows"], :] = _mm(_sigmoid(_mm(c["ug"], g1_ref[...])), g2_ref[...]).astype(BF16)
        kk = c["k"] * kk_ref[...]
        inv_norm = lax.rsqrt(jnp.maximum(_mm(kk * kk, seg_ref[...]), 1e-24))
        c.update(lw_pre=w0_ref[...] + _mm(jnp.tanh(_mm(c["uw"], w1_ref[...])), w2_ref[...]),
                 a_pre=a0_ref[...] + _mm(_mm(c["ua"], a1_ref[...]), a2_ref[...]),
                 kk=kk * inv_norm, ksum=None)
        rest(c, 1)

    def direction(dr):
        def stage(c):
            rows, k, kk = c["rows"], c["k"], c["kk"]
            lw = -DECAY_SCALE * _sigmoid(c["lw_pre"][:, dr * db:(dr + 1) * db])
            a = _sigmoid(c["a_pre"][:, dr * db:(dr + 1) * db])
            k_d = k * (1.0 + (a - 1.0) * ka_ref[...])
            c["ksum"] = k_d if c["ksum"] is None else c["ksum"] + k_d
            lw_parts = _split(lw, 1)
            rel = -_mmp([later_ref[dr]], lw_parts)
            at_ref[dr, rows, :] = (-kk * jnp.exp(rel - lw)).astype(BF16)
            rt_ref[dr, rows, :] = (c["r"] * jnp.exp(rel)).astype(BF16)
            gout = jnp.exp(-rel)
            bh_t = (kk * a * gout).T
            kh_t = (k_d * gout).T
            bh_r, kh_r = pltpu.roll(bh_t, CHUNK, 1), pltpu.roll(kh_t, CHUNK, 1)
            gam = jnp.exp(_mmp([whole_ref[...]], lw_parts))
            for q in range(ts // CHUNK):
                own = first_chunk if q == 0 else jnp.logical_not(first_chunk)
                chunk = c["s"] * (ts // CHUNK) + q
                bk_ref[dr, chunk] = jnp.concatenate([jnp.where(own, bh_t, bh_r), jnp.where(own, kh_t, kh_r)],
                                                    axis=1).astype(BF16)
                gc_ref[dr, chunk] = gam[q * SUBLANES:(q + 1) * SUBLANES, :]
            if dr == 0:
                rest(c, 2)
        return stage

    def bonus(c):
        bonus_ref[c["rows"], :] = (_mm(c["r"] * c["ksum"] * rk_ref[...], seg_ref[...]) * c["v"]).astype(BF16)

    subs = [dict(s=s, rows=slice(s * ts, (s + 1) * ts)) for s in range(tm // ts)]
    for stage in (project, shift, loras, direction(0), direction(1), bonus):
        for c in subs:
            stage(c)


def _prep_constants(tm):
    t = np.arange(tm)
    same = (t[:, None] // CHUNK) == (t[None, :] // CHUNK)
    later = np.stack([same & (t[None, :] > t[:, None]), same & (t[None, :] < t[:, None])]).astype(np.float32)
    whole = (np.arange(tm // CHUNK * SUBLANES)[:, None] // SUBLANES) == (t[None, :] // CHUNK)
    return jnp.asarray(later, BF16), jnp.asarray(whole.astype(np.float32), BF16)


def _pad_to(a, axis, size):
    pad = [(0, 0)] * a.ndim
    pad[axis] = (0, size - a.shape[axis])
    return jnp.pad(a, pad)


def _odd_front(x, mod, norm_g, w_in, p, *, ctx):
    b, t, d = x.shape
    db = p["k_k"].shape[-1]
    tm = min(2 * TILE_TOKENS, t)
    assert t % tm == 0 and tm % PREP_TOKENS == 0 and PREP_TOKENS == 2 * CHUNK
    nc = t // CHUNK
    nb = t // SUBLANES
    per = tm // SUBLANES
    later, whole = _prep_constants(PREP_TOKENS)
    heads = db // RWKV_HEAD
    seg = jnp.asarray(np.kron(np.eye(heads), np.ones((RWKV_HEAD, RWKV_HEAD))), BF16)

    def lora_pair(w_a, w_b):
        rank = w_a.shape[-1]
        down = _pad_to(jnp.concatenate([w_a[0], w_a[1]], axis=-1), 1, LORA_PAD)
        up = jnp.zeros((LORA_PAD, 2 * db), F32)
        up = up.at[0:rank, 0:db].set(w_b[0]).at[rank:2 * rank, db:2 * db].set(w_b[1])
        return down.astype(BF16), up.astype(BF16)

    w1c, w2c = lora_pair(p["w1"], p["w2"])
    a1c, a2c = lora_pair(p["a1"], p["a2"])
    consts = (mod, norm_g.reshape(1, d), w_in.astype(BF16),
              _pad_to(p["rwkv_mu"], 0, SUBLANES), p["w0"].reshape(1, 2 * db), p["a0"].reshape(1, 2 * db),
              w1c, w2c, a1c, a2c,
              _pad_to(p["g1"], 1, LORA_PAD).astype(BF16), _pad_to(p["g2"], 0, LORA_PAD).astype(BF16),
              p["k_k"].reshape(1, db), p["k_a"].reshape(1, db), p["r_k"].reshape(1, db), seg, later, whole)
    full = lambda a: pl.BlockSpec(a.shape, lambda bi, i: (0,) * a.ndim)
    tok = pl.BlockSpec((None, tm, db), lambda bi, i: (bi, i, 0))
    dirtok = pl.BlockSpec((None, 2, tm, db), lambda bi, i: (bi, 0, i, 0))
    dir_shape = jax.ShapeDtypeStruct((b, 2, t, db), BF16)
    nrest = 3 * db
    assert w_in.shape[1] == 8 * db
    tok_b = jax.ShapeDtypeStruct((b, t, db), BF16)
    return pl.pallas_call(
        functools.partial(_front_kernel, ctx=ctx),
        grid=(b, t // tm),
        in_specs=[pl.BlockSpec((None, tm, d), lambda bi, i: (bi, i, 0)),
                  pl.BlockSpec((None, SUBLANES, d), lambda bi, i: (bi, jnp.maximum(i * per - 1, 0), 0)),
                  pl.BlockSpec((None, SUBLANES, d), lambda bi, i: (bi, jnp.minimum((i + 1) * per, nb - 1), 0))]
        + [full(a) for a in consts],
        out_specs=[pl.BlockSpec((None, tm, nrest), lambda bi, i: (bi, i, 0)), tok, tok, tok, dirtok, dirtok,
                   pl.BlockSpec((None, 2, tm // CHUNK, db, 4 * CHUNK), lambda bi, i: (bi, 0, i, 0, 0)),
                   pl.BlockSpec((None, 2, tm // CHUNK, SUBLANES, db), lambda bi, i: (bi, 0, i, 0, 0))],
        out_shape=[jax.ShapeDtypeStruct((b, t, nrest), BF16), tok_b, tok_b, tok_b, dir_shape, dir_shape,
                   jax.ShapeDtypeStruct((b, 2, nc, db, 4 * CHUNK), BF16),
                   jax.ShapeDtypeStruct((b, 2, nc, SUBLANES, db), F32)],
        compiler_params=pltpu.CompilerParams(dimension_semantics=("parallel", "parallel"),
                                             vmem_limit_bytes=VMEM_LIMIT),
        name="front_ctx" if ctx else "front_latent",
    )(x, x, x, *consts)


SCAN_PARTS = 1
SCAN_CHUNKS_PER_STEP = 4


def _scan_kernel(at0, rt0, bk0, v0, gc0, at1, rt1, bk1, v1, gc1, s0_ref,
                 hm_ref, ms_ref, mi_ref, idw_ref, bdm_ref, eye_ref,
                 of_ref, or_ref, st_ref):
    @pl.when(pl.program_id(0) == 0)
    def _():
        st_ref[...] = s0_ref[...]

    npart = SCAN_PARTS
    pair = GROUP // 2
    bdm_b = bdm_ref[...]
    bd2 = bdm_b[0:pair, 0:pair].astype(F32)
    eye2 = eye_ref[0:pair, 0:pair]

    def bd(x):
        return [jnp.concatenate([part * hm_ref[h] for h in range(HEADS_PER_GROUP)], axis=0)
                for part in _split(x, npart)]

    def cat_parts(plists):
        return [jnp.concatenate(ps, axis=-1) for ps in zip(*plists)]

    steps = int(np.log2(CHUNK)) - 1

    def load(c):
        at_r, rt_r, bk_r, v_r, gc_r = c["refs"]
        bi, g, sl, q, rows = c["bi"], c["g"], c["sl"], c["q"], c["rows"]
        c.update(at=at_r[bi, rows, sl], rt=rt_r[bi, rows, sl], v=v_r[bi, rows, sl], gc=gc_r[bi, q, 0:1, sl],
                 bk=bk_r[bi, q, g * GROUP:(g + 1) * GROUP, :])

    def interactions(c):
        ms, mi = ms_ref[c["d"]], mi_ref[c["d"]]
        bk = c["bk"]
        wts = jnp.concatenate([jnp.concatenate([bk[:, 0:pair]] * 2, axis=1) * bdm_b,
                               jnp.concatenate([bk[:, pair:]] * 2, axis=1) * bdm_b], axis=1)
        gmat = _dot(jnp.concatenate([c["at"], c["rt"]], axis=0), wts, False)
        lab = gmat[0:CHUNK, 0:GROUP] * ms
        c.update(lak=gmat[0:CHUNK, GROUP:] * ms, mrb=gmat[CHUNK:, 0:GROUP] * mi,
                 mrk=gmat[CHUNK:, GROUP:] * mi, pw=lab, tinv=idw_ref[...] + lab)

    def invert(s):
        def stage(c):
            pw, tinv = c["pw"], c["tinv"]
            wts = bd(pw)
            if s == 0:
                pw = _mmp(_split(pw, npart), wts)
            elif s < steps:
                res = _mmp(_split(jnp.concatenate([pw, tinv], axis=0), npart), wts)
                pw, tinv = res[0:CHUNK], tinv + res[CHUNK:]
            else:
                tinv = tinv + _mmp(_split(tinv, npart), wts)
            c.update(pw=pw, tinv=tinv)
        return stage

    def values(c):
        lv_mv = _mmp(_split(jnp.concatenate([c["lak"], c["mrk"]], axis=0), npart), bd(c["v"]))
        c.update(lv=lv_mv[0:CHUNK], mv=lv_mv[CHUNK:])

    def solve(c):
        aw = _mmp(_split(c["tinv"], npart), cat_parts([bd(c["at"]), bd(c["lv"])]))
        c.update(ahat=aw[:, 0:GROUP], wv=aw[:, GROUP:])

    def outputs(c):
        ro = _mmp(_split(c["mrb"], npart), cat_parts([bd(c["ahat"]), bd(c["wv"])]))
        c.update(rhat=(c["rt"].astype(F32) + ro[:, 0:GROUP]) * c["gc"], ointra=ro[:, GROUP:] + c["mv"])

    def transition(c):
        zero_rows = jnp.zeros((CHUNK, GROUP), BF16)
        zero_half = jnp.zeros((CHUNK, pair), BF16)
        zero_blk = jnp.zeros((pair, pair), F32)
        pms, qms = [], []
        for p in range(2):
            ps = slice(p * pair, (p + 1) * pair)
            top = jnp.concatenate([c["ahat"][:, ps], c["wv"][:, ps]], axis=1).astype(BF16)
            mid = jnp.concatenate([zero_half, c["v"][:, ps]], axis=1)
            rhs = jnp.concatenate([top, zero_rows, mid, zero_rows], axis=0)
            pq = _dot(c["bk"][ps, :], rhs, False)
            pblk = (pq[:, 0:pair] * bd2 + eye2) * c["gc"][:, ps]
            qblk = pq[:, pair:] * bd2
            pms.append(jnp.concatenate([pblk, zero_blk] if p == 0 else [zero_blk, pblk], axis=1))
            qms.append(jnp.concatenate([qblk, zero_blk] if p == 0 else [zero_blk, qblk], axis=1))
        c.update(pm=jnp.concatenate(pms, axis=0), qm=jnp.concatenate(qms, axis=0))

    def advance(c, st):
        rs = _dot(jnp.concatenate([c["rhat"], c["pm"]], axis=0).astype(BF16), st.astype(BF16), False)
        c["out"][c["bi"], c["rows"], c["sl"]] = (rs[0:CHUNK] + c["ointra"]).astype(BF16)
        return rs[CHUNK:] + c["qm"]

    nq = at0.shape[1] // CHUNK
    dirs = ((at0, rt0, bk0, v0, gc0, of_ref), (at1, rt1, bk1, v1, gc1, or_ref))
    groups = {}
    for bi in range(at0.shape[0]):
        for d, refs in enumerate(dirs):
            for g in range(at0.shape[-1] // GROUP):
                order = range(nq) if d == 0 else reversed(range(nq))
                groups[(bi, d, g)] = [
                    dict(q=q, rows=slice(q * CHUNK, (q + 1) * CHUNK), bi=bi, d=d, g=g,
                         sl=slice(g * GROUP, (g + 1) * GROUP), refs=refs[:5], out=refs[5]) for q in order]
    chains = [c for k in range(nq) for cs in groups.values() for c in [cs[k]]]
    stages = ([load, interactions] + [invert(s) for s in range(steps + 1)]
              + [values, solve, outputs, transition])
    for stage in stages:
        for c in chains:
            stage(c)
    states = {key: st_ref[key[0], key[1], key[2]] for key in groups}
    for k in range(nq):
        for key, cs in groups.items():
            states[key] = advance(cs[k], states[key])
    for key in groups:
        st_ref[key[0], key[1], key[2]] = states[key]


def _scan_constants():
    t = np.arange(CHUNK)[:, None]
    lane = np.arange(GROUP)[None, :]
    s = lane % CHUNK
    strict = np.stack([s < t, s > t]).astype(np.float32)
    incl = np.stack([s <= t, s >= t]).astype(np.float32)
    idw = (s == t).astype(np.float32)
    hm = np.stack([np.broadcast_to((lane // RWKV_HEAD) == h, (CHUNK, GROUP)) for h in range(HEADS_PER_GROUP)])
    bdm = np.kron(np.eye(HEADS_PER_GROUP), np.ones((RWKV_HEAD, RWKV_HEAD)))
    return (jnp.asarray(hm.astype(np.float32), BF16), jnp.asarray(strict), jnp.asarray(incl), jnp.asarray(idw),
            jnp.asarray(bdm, BF16), jnp.eye(GROUP, dtype=F32))


def _rwkv_scan(at, rt, bk, v, gc, s0):
    b, _, t, db = at.shape
    nc = t // CHUNK
    consts = _scan_constants()
    full = lambda a: pl.BlockSpec(a.shape, lambda i: (0,) * a.ndim)
    nq = min(SCAN_CHUNKS_PER_STEP, nc)
    ns = nc // nq
    assert nc % nq == 0
    fwd, rev = (lambda i: i), (lambda i: ns - 1 - i)
    dspec = lambda d, ci: pl.BlockSpec((b, None, nq * CHUNK, db), lambda i: (0, d, ci(i), 0))
    kspec = lambda d, ci: pl.BlockSpec((b, None, nq, db, 4 * CHUNK), lambda i: (0, d, ci(i), 0, 0))
    vspec = lambda ci: pl.BlockSpec((b, nq * CHUNK, db), lambda i: (0, ci(i), 0))
    gspec = lambda d, ci: pl.BlockSpec((b, None, nq, SUBLANES, db), lambda i: (0, d, ci(i), 0, 0))
    in_specs = ([dspec(0, fwd), dspec(0, fwd), kspec(0, fwd), vspec(fwd), gspec(0, fwd),
                 dspec(1, rev), dspec(1, rev), kspec(1, rev), vspec(rev), gspec(1, rev), full(s0)]
                + [full(a) for a in consts])
    return pl.pallas_call(
        _scan_kernel,
        grid=(ns,),
        in_specs=in_specs,
        out_specs=[vspec(fwd), vspec(rev), full(s0)],
        out_shape=[jax.ShapeDtypeStruct((b, t, db), BF16), jax.ShapeDtypeStruct((b, t, db), BF16),
                   jax.ShapeDtypeStruct(s0.shape, F32)],
        compiler_params=pltpu.CompilerParams(dimension_semantics=("arbitrary",),
                                             vmem_limit_bytes=VMEM_LIMIT),
        name="rwkv_scan",
    )(at, rt, bk, v, gc, at, rt, bk, v, gc, s0, *consts)


def _conv_kernel(glu_ref, w_ref, b_ref, o_ref, h_ref, *, stride):
    t, c = glu_ref.shape
    halo = (CONF_CONV_W // 2) * stride
    nsteps = t // CONV_ROWS
    zeros = jnp.zeros((halo, c), F32)
    h_ref[0:halo, :] = zeros
    h_ref[halo + t:halo + t + halo, :] = zeros

    def widen(j, carry):
        r0 = pl.multiple_of(j * CONV_ROWS, CONV_ROWS)
        h_ref[pl.ds(halo + r0, CONV_ROWS), :] = glu_ref[pl.ds(r0, CONV_ROWS), :].astype(F32)
        return carry

    lax.fori_loop(0, nsteps, widen, 0)
    w = w_ref[...]
    bias = b_ref[...]

    def conv(j, carry):
        r0 = pl.multiple_of(j * CONV_ROWS, CONV_ROWS)
        acc = jnp.broadcast_to(bias, (CONV_ROWS, c))
        for tap in range(CONF_CONV_W):
            acc = acc + h_ref[pl.ds(r0 + tap * stride, CONV_ROWS), :] * w[tap:tap + 1, :]
        o_ref[pl.ds(r0, CONV_ROWS), :] = acc.astype(BF16)
        return carry

    lax.fori_loop(0, nsteps, conv, 0)


def _conformer_conv(proj, dw_w, dw_b, *, stride):
    b, t, n = proj.shape
    db = dw_w.shape[-1]
    lanes = LANES
    assert t % CONV_ROWS == 0 and stride % SUBLANES == 0
    halo = (CONF_CONV_W // 2) * stride
    c1 = db // lanes
    return pl.pallas_call(
        functools.partial(_conv_kernel, stride=stride),
        grid=(b, db // lanes),
        in_specs=[pl.BlockSpec((None, t, lanes), lambda bi, j: (bi, 0, c1 + j)),
                  pl.BlockSpec((_round_up(CONF_CONV_W, SUBLANES), lanes), lambda bi, j: (0, j)),
                  pl.BlockSpec((1, lanes), lambda bi, j: (0, j))],
        out_specs=pl.BlockSpec((None, t, lanes), lambda bi, j: (bi, 0, j)),
        out_shape=jax.ShapeDtypeStruct((b, t, db), BF16),
        scratch_shapes=[pltpu.VMEM((t + 2 * halo, lanes), F32)],
        compiler_params=pltpu.CompilerParams(dimension_semantics=("parallel", "parallel"),
                                             vmem_limit_bytes=VMEM_LIMIT),
        name="conformer_conv",
    )(proj, _pad_to(dw_w, 0, _round_up(CONF_CONV_W, SUBLANES)), dw_b.reshape(1, db))


def _round_up(x, m):
    return (x + m - 1) // m * m


def _merge_kernel(x_ref, of_ref, or_ref, g_ref, bonus_ref, conv_ref, zc_ref, zd_ref, mod_ref,
                  lnxg_ref, lnxb_ref, clng_ref, clnb_ref, wout_ref, fg_ref, seg_ref, o_ref):
    tm, d = x_ref.shape
    mrow = _mod_row(mod_ref, False)
    seg = seg_ref[...]

    def centre(c):
        o = of_ref[c["rows"], :].astype(F32) + or_ref[c["rows"], :].astype(F32)
        c.update(dev=o - _mm(o, seg))

    def normalise(c):
        rows, dev = c["rows"], c["dev"]
        var = _mm(dev * dev, seg)
        on = dev * lax.rsqrt(var + LNX_EPS) * lnxg_ref[...] + lnxb_ref[...]
        y_c = ((on + bonus_ref[rows, :].astype(F32)) * g_ref[rows, :].astype(F32)
               * _silu(zc_ref[rows, :].astype(F32)))
        cv = conv_ref[rows, :].astype(F32)
        cdev = cv - jnp.mean(cv, axis=-1, keepdims=True)
        cvar = jnp.mean(cdev * cdev, axis=-1, keepdims=True)
        ln = cdev * lax.rsqrt(cvar + CONF_LN_EPS) * clng_ref[...] + clnb_ref[...]
        y_d = _silu(ln) * _silu(zd_ref[rows, :].astype(F32))
        c.update(y=jnp.concatenate([y_c, y_d], axis=-1))

    def emit(c):
        y = _mm(c["y"], wout_ref[...])
        o_ref[c["rows"], :] = _rmsnorm(x_ref[c["rows"], :] + mrow[:, 2 * d:3 * d] * y, fg_ref[...])

    ts = tm // MERGE_SUBTILES
    subs = [dict(rows=slice(s * ts, (s + 1) * ts)) for s in range(MERGE_SUBTILES)]
    for stage in (centre, normalise, emit):
        for c in subs:
            stage(c)


def _merge(x, o_f, o_r, g, bonus, conv, proj, mod, p, final_g):
    b, t, d = x.shape
    db = d // 2
    tm = min(4 * TILE_TOKENS, t)
    heads = db // RWKV_HEAD
    seg = jnp.asarray(np.kron(np.eye(heads), np.full((RWKV_HEAD, RWKV_HEAD), 1.0 / RWKV_HEAD)), BF16)
    consts = (mod, p["lnx_g"].reshape(1, db), p["lnx_b"].reshape(1, db), p["conf_ln_g"].reshape(1, db),
              p["conf_ln_b"].reshape(1, db), p["out_o"].astype(BF16), final_g.reshape(1, d), seg)
    full = lambda a: pl.BlockSpec(a.shape, lambda bi, i: (0,) * a.ndim)
    tok = pl.BlockSpec((None, tm, db), lambda bi, i: (bi, i, 0))
    col = lambda cblk: pl.BlockSpec((None, tm, db), lambda bi, i: (bi, i, cblk))
    return pl.pallas_call(
        _merge_kernel,
        grid=(b, t // tm),
        in_specs=[pl.BlockSpec((None, tm, d), lambda bi, i: (bi, i, 0)), tok, tok, tok, tok, tok,
                  col(0), col(2)] + [full(a) for a in consts],
        out_specs=pl.BlockSpec((None, tm, d), lambda bi, i: (bi, i, 0)),
        out_shape=jax.ShapeDtypeStruct((b, t, d), F32),
        compiler_params=pltpu.CompilerParams(dimension_semantics=("parallel", "parallel"),
                                             vmem_limit_bytes=VMEM_LIMIT),
        name="merge",
    )(x, o_f, o_r, g, bonus, conv, proj, proj, *consts)


def kernel(x, c, ctx, c_ctx, ada_w_e, ada_b_e, norm_e, in_e, out_e, pool_w, pool_scale, sconv_w,
           ada_w_o, ada_b_o, norm_o, in_o, out_o, rwkv_mu, w0, w1, w2, a0, a1, a2, g1, g2,
           k_k, k_a, r_k, lnx_g, lnx_b, conf_dw_w, conf_dw_b, conf_ln_g, conf_ln_b, final_g):
    b, t, d = x.shape
    db = d // 2
    assert b == 2
    cvec = _pad_to(jnp.concatenate([c, c_ctx[None, :]], axis=0), 0, SUBLANES)
    mod_e = _adaln(cvec, ada_w_e[0], ada_b_e[0])
    mod_o = _adaln(cvec, ada_w_o[0], ada_b_o[0])

    even = dict(norm_g=norm_e[0], w_in=in_e[0], w_out=out_e[0], pool_w=pool_w[0],
                pool_scale=pool_scale[0], sconv_w=sconv_w[0])
    x1 = _even_layer(x, mod_e, ctx=False, line=GRID_W, **even)
    xc1 = _even_layer(ctx, mod_e, ctx=True, line=ctx.shape[1], **even)

    p = dict(rwkv_mu=rwkv_mu[0], w0=w0[0], w1=w1[0], w2=w2[0], a0=a0[0], a1=a1[0], a2=a2[0],
             g1=g1[0], g2=g2[0], k_k=k_k[0], k_a=k_a[0], r_k=r_k[0], lnx_g=lnx_g[0], lnx_b=lnx_b[0],
             conf_ln_g=conf_ln_g[0], conf_ln_b=conf_ln_b[0], out_o=out_o[0])

    _, v_c, _, _, at_c, rt_c, bk_c, gc_c = _odd_front(xc1, mod_o, norm_o[0], in_o[0], p, ctx=True)
    zero_state = jnp.zeros((b, 2, db // GROUP, GROUP, GROUP), F32)
    _, _, ctx_state = _rwkv_scan(at_c, rt_c, bk_c, v_c, gc_c, zero_state)

    proj, v_l, g_l, bonus, at_l, rt_l, bk_l, gc_l = _odd_front(x1, mod_o, norm_o[0], in_o[0], p, ctx=False)
    o_f, o_r, _ = _rwkv_scan(at_l, rt_l, bk_l, v_l, gc_l, ctx_state)
    conv = _conformer_conv(proj, conf_dw_w[0], conf_dw_b[0], stride=GRID_W)
    return _merge(x1, o_f, o_r, g_l, bonus, conv, proj, mod_o, p, final_g)
```

```python
import functools

import numpy as np
import jax
import jax.numpy as jnp
from jax import lax
from jax.experimental import pallas as pl
from jax.experimental.pallas import tpu as pltpu

F32 = jnp.float32
BF16 = jnp.bfloat16

GRID_W = 64
POOL_WINDOWS = (2, 4, 8, 16)
SHORT_CONV_W = 3
CONF_CONV_W = 31
RWKV_HEAD = 64
NORM_EPS = 1e-6
LNX_EPS = 64e-5
CONF_LN_EPS = 1e-5
DECAY_SCALE = float(np.exp(-0.5))

CHUNK = 64
HEADS_PER_GROUP = 4
GROUP = HEADS_PER_GROUP * RWKV_HEAD
LANES = 128
SUBLANES = 8
LORA_PAD = LANES
VMEM_LIMIT = 52 * 1024 * 1024

TILE_TOKENS = 256
PREP_TOKENS = 128
ADALN_COLS = 768
CONV_ROWS = 256
MERGE_SUBTILES = 4


def _sigmoid(x):
    return 0.5 * jnp.tanh(0.5 * x) + 0.5


def _silu(x):
    return x * _sigmoid(x)


def _split(x, n):
    if x.dtype == BF16:
        return [x]
    parts, r = [], x
    for i in range(n):
        p = r.astype(BF16)
        parts.append(p)
        if i + 1 < n:
            r = r - p.astype(F32)
    return parts


def _dot(a, b, nt):
    dims = (((1,), (1,)), ((), ())) if nt else (((1,), (0,)), ((), ()))
    return lax.dot_general(a, b, dims, preferred_element_type=F32)


def _mmp(ap, bp, nt=False):
    order = max(len(ap), len(bp))
    acc = None
    for i in reversed(range(len(ap))):
        for j in reversed(range(len(bp))):
            if i + j < order:
                t = _dot(ap[i], bp[j], nt)
                acc = t if acc is None else acc + t
    return acc


def _mm(a, b, na=1, nb=1, nt=False):
    return _mmp(_split(a, na), _split(b, nb), nt)


def _rmsnorm(x, g):
    ms = jnp.mean(x * x, axis=-1, keepdims=True)
    return x * lax.rsqrt(ms + NORM_EPS) * g


def _modulated(x, g, mod_row, d):
    shift, scale = mod_row[:, 0:d], mod_row[:, d:2 * d]
    return _rmsnorm(x, g) * (1.0 + scale) + shift


def _mod_row(mod_ref, ctx):
    row = 2 if ctx else pl.program_id(0)
    return mod_ref[pl.ds(row, 1), :]


def _adaln_kernel(c_ref, w_ref, b_ref, o_ref):
    o_ref[...] = _mm(_silu(c_ref[...]), w_ref[...], 2, 2) + b_ref[...]


def _adaln(cvec, w, b):
    d, n = w.shape
    tn = ADALN_COLS
    assert n % tn == 0
    return pl.pallas_call(
        _adaln_kernel,
        grid=(n // tn,),
        in_specs=[pl.BlockSpec((SUBLANES, d), lambda j: (0, 0)),
                  pl.BlockSpec((d, tn), lambda j: (0, j)),
                  pl.BlockSpec((1, tn), lambda j: (0, j))],
        out_specs=pl.BlockSpec((SUBLANES, tn), lambda j: (0, j)),
        out_shape=jax.ShapeDtypeStruct((SUBLANES, n), F32),
        compiler_params=pltpu.CompilerParams(dimension_semantics=("parallel",),
                                             vmem_limit_bytes=VMEM_LIMIT),
        name="adaln",
    )(cvec, w, b.reshape(1, n))


def _even_kernel(x_ref, mod_ref, g_ref, win_ref, wout_ref, poolw_ref, pscale_ref, sconv_ref,
                 band_ref, invc_ref, o_ref, *, ctx, line):
    tm, d = x_ref.shape
    ts = band_ref.shape[-1]
    db = d // 2
    grp = db // len(POOL_WINDOWS)
    mrow = _mod_row(mod_ref, ctx)

    def project(c):
        x = x_ref[c["rows"], :]
        c.update(x=x, proj=_mm(_modulated(x, g_ref[...], mrow, d), win_ref[...]))

    def pool(c):
        u_a = c["proj"][:, 0:db]
        pooled = []
        for i in range(len(POOL_WINDOWS)):
            ug = u_a[:, i * grp:(i + 1) * grp]
            pooled.append(_mm(band_ref[i], ug) * invc_ref[i] - ug)
        c.update(pooled=pooled)

    def mix(c):
        proj = c["proj"]
        z_a = proj[:, db:2 * db]
        v_b, g_b = proj[:, 2 * db:3 * db], proj[:, 3 * db:4 * db]
        g_c, z_b = proj[:, 4 * db:5 * db], proj[:, 5 * db:6 * db]
        ya = [_mm(c["pooled"][i], poolw_ref[i]) for i in range(len(POOL_WINDOWS))]
        y_a = jnp.concatenate(ya, axis=-1) * pscale_ref[...] * _silu(z_a)
        q = g_c * v_b
        pos = lax.broadcasted_iota(jnp.int32, q.shape, 0) % line
        q_prev = jnp.where(pos >= 1, pltpu.roll(q, 1, 0), 0.0)
        q_next = jnp.where(pos <= line - 2, pltpu.roll(q, ts - 1, 0), 0.0)
        w = sconv_ref[...]
        conv = q_prev * w[0:1, :] + q * w[1:2, :] + q_next * w[2:3, :]
        y_b = g_b * conv * _silu(z_b)
        c.update(y=jnp.concatenate([y_a, y_b], axis=-1))

    def emit(c):
        o_ref[c["rows"], :] = c["x"] + mrow[:, 2 * d:3 * d] * _mm(c["y"], wout_ref[...])

    subs = [dict(rows=slice(s * ts, (s + 1) * ts)) for s in range(tm // ts)]
    for stage in (project, pool, mix, emit):
        for c in subs:
            stage(c)


def _pool_constants(tm, line):
    pos = np.arange(tm) % line
    lid = np.arange(tm) // line
    bands, invs = [], []
    for w in POOL_WINDOWS:
        lo = np.clip(pos - w // 2, 0, line)
        hi = np.clip(pos - w // 2 + w, 0, line)
        same = lid[:, None] == lid[None, :]
        band = same & (pos[None, :] >= lo[:, None]) & (pos[None, :] < hi[:, None])
        bands.append(band.astype(np.float32))
        invs.append(np.broadcast_to((1.0 / (hi - lo))[:, None], (tm, LANES)).astype(np.float32))
    return jnp.asarray(np.stack(bands), BF16), jnp.asarray(np.stack(invs), F32)


def _even_layer(x, mod, norm_g, w_in, w_out, pool_w, pool_scale, sconv_w, *, ctx, line):
    b, t, d = x.shape
    tm = min(4 * TILE_TOKENS, t)
    assert t % tm == 0 and tm % line == 0
    db = d // 2
    ts = max(PREP_TOKENS, line)
    assert tm % ts == 0 and ts % line == 0
    band, invc = _pool_constants(ts, line)
    full = lambda a: pl.BlockSpec(a.shape, lambda bi, i: (0,) * a.ndim, pipeline_mode=pl.Buffered(1))
    args = (mod, norm_g.reshape(1, d), w_in.astype(BF16), w_out.astype(BF16), pool_w.astype(BF16),
            pool_scale.reshape(1, db), sconv_w, band, invc)
    return pl.pallas_call(
        functools.partial(_even_kernel, ctx=ctx, line=line),
        grid=(b, t // tm),
        in_specs=[pl.BlockSpec((None, tm, d), lambda bi, i: (bi, i, 0))] + [full(a) for a in args],
        out_specs=pl.BlockSpec((None, tm, d), lambda bi, i: (bi, i, 0)),
        out_shape=jax.ShapeDtypeStruct((b, t, d), F32),
        compiler_params=pltpu.CompilerParams(dimension_semantics=("parallel", "parallel"),
                                             vmem_limit_bytes=VMEM_LIMIT),
        name="even_ctx" if ctx else "even_latent",
    )(x, *args)


def _front_kernel(x_ref, xp_ref, xn_ref, mod_ref, ng_ref, win_ref,
                  mu_ref, w0_ref, a0_ref, w1_ref, w2_ref, a1_ref, a2_ref,
                  g1_ref, g2_ref, kk_ref, ka_ref, rk_ref, seg_ref, later_ref, whole_ref,
                  rest_ref, v_ref, g_ref, bonus_ref, at_ref, rt_ref, bk_ref, gc_ref, *, ctx):
    tm, d = x_ref.shape
    db = v_ref.shape[-1]
    ts = PREP_TOKENS
    te = ts + 2 * SUBLANES
    i, n = pl.program_id(1), pl.num_programs(1)
    mrow = _mod_row(mod_ref, ctx)
    h_prev = _modulated(xp_ref[...], ng_ref[...], mrow, d) * jnp.where(i > 0, 1.0, 0.0)
    h_next = _modulated(xn_ref[...], ng_ref[...], mrow, d) * jnp.where(i < n - 1, 1.0, 0.0)
    h = _modulated(x_ref[...], ng_ref[...], mrow, d)
    h_b = h.astype(BF16)
    h_ext = jnp.concatenate([h_prev, h, h_next], axis=0).astype(BF16)
    mu = mu_ref[...]
    first_chunk = lax.broadcasted_iota(jnp.int32, (db, ts), 1) < CHUNK

    def rest(c, j):
        rows = c["rows"]
        if j == 1:
            p12 = _dot(h_b[rows], win_ref[:, 5 * db:7 * db], False)
            rest_ref[rows, db:2 * db] = (p12[:, 0:db] * _sigmoid(p12[:, db:])).astype(BF16)
        else:
            col = 4 if j == 0 else 7
            rest_ref[rows, j * db:(j + 1) * db] = _dot(h_b[rows], win_ref[:, col * db:(col + 1) * db],
                                                        False).astype(BF16)

    def project(c):
        c.update(z_ext=_dot(h_ext[c["s"] * ts:c["s"] * ts + te], win_ref[:, 0:4 * db], False))

    def shift(c):
        z_ext = c["z_ext"]
        nbr = (pltpu.roll(z_ext, 1, 0) + pltpu.roll(z_ext, te - 1, 0))[SUBLANES:ts + SUBLANES]
        z = z_ext[SUBLANES:ts + SUBLANES]
        dz = 0.5 * nbr - z
        u, du = z[:, 0:db], dz[:, 0:db]
        v = z[:, 3 * db:4 * db] + dz[:, 3 * db:4 * db] * mu[2:3, :]
        c.update(r=z[:, db:2 * db] + dz[:, db:2 * db] * mu[0:1, :],
                 k=z[:, 2 * db:3 * db] + dz[:, 2 * db:3 * db] * mu[1:2, :], v=v,
                 uw=u + du * mu[3:4, :], ua=u + du * mu[4:5, :], ug=u + du * mu[5:6, :])
        v_ref[c["rows"], :] = v.astype(BF16)
        rest(c, 0)

    def loras(c):
        g_ref[c["rows"], :] = _mm(_sigmoid(_mm(c["ug"], g1_ref[...])), g2_ref[...]).astype(BF16)
        kk = c["k"] * kk_ref[...]
        inv_norm = lax.rsqrt(jnp.maximum(_mm(kk * kk, seg_ref[...]), 1e-24))
        c.update(lw_pre=w0_ref[...] + _mm(jnp.tanh(_mm(c["uw"], w1_ref[...])), w2_ref[...]),
                 a_pre=a0_ref[...] + _mm(_mm(c["ua"], a1_ref[...]), a2_ref[...]),
                 kk=kk * inv_norm, ksum=None)
        rest(c, 1)

    def direction(dr):
        def stage(c):
            rows, k, kk = c["rows"], c["k"], c["kk"]
            lw = -DECAY_SCALE * _sigmoid(c["lw_pre"][:, dr * db:(dr + 1) * db])
            a = _sigmoid(c["a_pre"][:, dr * db:(dr + 1) * db])
            k_d = k * (1.0 + (a - 1.0) * ka_ref[...])
            c["ksum"] = k_d if c["ksum"] is None else c["ksum"] + k_d
            lw_parts = _split(lw, 1)
            rel = -_mmp([later_ref[dr]], lw_parts)
            at_ref[dr, rows, :] = (-kk * jnp.exp(rel - lw)).astype(BF16)
            rt_ref[dr, rows, :] = (c["r"] * jnp.exp(rel)).astype(BF16)
            gout = jnp.exp(-rel)
            bh_t = (kk * a * gout).T
            kh_t = (k_d * gout).T
            bh_r, kh_r = pltpu.roll(bh_t, CHUNK, 1), pltpu.roll(kh_t, CHUNK, 1)
            gam = jnp.exp(_mmp([whole_ref[...]], lw_parts))
            for q in range(ts // CHUNK):
                own = first_chunk if q == 0 else jnp.logical_not(first_chunk)
                chunk = c["s"] * (ts // CHUNK) + q
                bk_ref[dr, chunk] = jnp.concatenate([jnp.where(own, bh_t, bh_r), jnp.where(own, kh_t, kh_r)],
                                                    axis=1).astype(BF16)
                gc_ref[dr, chunk] = gam[q * SUBLANES:(q + 1) * SUBLANES, :]
            if dr == 0:
                rest(c, 2)
        return stage

    def bonus(c):
        bonus_ref[c["rows"], :] = (_mm(c["r"] * c["ksum"] * rk_ref[...], seg_ref[...]) * c["v"]).astype(BF16)

    subs = [dict(s=s, rows=slice(s * ts, (s + 1) * ts)) for s in range(tm // ts)]
    for stage in (project, shift, loras, direction(0), direction(1), bonus):
        for c in subs:
            stage(c)


def _prep_constants(tm):
    t = np.arange(tm)
    same = (t[:, None] // CHUNK) == (t[None, :] // CHUNK)
    later = np.stack([same & (t[None, :] > t[:, None]), same & (t[None, :] < t[:, None])]).astype(np.float32)
    whole = (np.arange(tm // CHUNK * SUBLANES)[:, None] // SUBLANES) == (t[None, :] // CHUNK)
    return jnp.asarray(later, BF16), jnp.asarray(whole.astype(np.float32), BF16)


def _pad_to(a, axis, size):
    pad = [(0, 0)] * a.ndim
    pad[axis] = (0, size - a.shape[axis])
    return jnp.pad(a, pad)


def _odd_front(x, mod, norm_g, w_in, p, *, ctx):
    b, t, d = x.shape
    db = p["k_k"].shape[-1]
    tm = min(2 * TILE_TOKENS, t)
    assert t % tm == 0 and tm % PREP_TOKENS == 0 and PREP_TOKENS == 2 * CHUNK
    nc = t // CHUNK
    nb = t // SUBLANES
    per = tm // SUBLANES
    later, whole = _prep_constants(PREP_TOKENS)
    heads = db // RWKV_HEAD
    seg = jnp.asarray(np.kron(np.eye(heads), np.ones((RWKV_HEAD, RWKV_HEAD))), BF16)

    def lora_pair(w_a, w_b):
        rank = w_a.shape[-1]
        down = _pad_to(jnp.concatenate([w_a[0], w_a[1]], axis=-1), 1, LORA_PAD)
        up = jnp.zeros((LORA_PAD, 2 * db), F32)
        up = up.at[0:rank, 0:db].set(w_b[0]).at[rank:2 * rank, db:2 * db].set(w_b[1])
        return down.astype(BF16), up.astype(BF16)

    w1c, w2c = lora_pair(p["w1"], p["w2"])
    a1c, a2c = lora_pair(p["a1"], p["a2"])
    consts = (mod, norm_g.reshape(1, d), w_in.astype(BF16),
              _pad_to(p["rwkv_mu"], 0, SUBLANES), p["w0"].reshape(1, 2 * db), p["a0"].reshape(1, 2 * db),
              w1c, w2c, a1c, a2c,
              _pad_to(p["g1"], 1, LORA_PAD).astype(BF16), _pad_to(p["g2"], 0, LORA_PAD).astype(BF16),
              p["k_k"].reshape(1, db), p["k_a"].reshape(1, db), p["r_k"].reshape(1, db), seg, later, whole)
    full = lambda a: pl.BlockSpec(a.shape, lambda bi, i: (0,) * a.ndim, pipeline_mode=pl.Buffered(1))
    tok = pl.BlockSpec((None, tm, db), lambda bi, i: (bi, i, 0))
    dirtok = pl.BlockSpec((None, 2, tm, db), lambda bi, i: (bi, 0, i, 0))
    dir_shape = jax.ShapeDtypeStruct((b, 2, t, db), BF16)
    nrest = 3 * db
    assert w_in.shape[1] == 8 * db
    tok_b = jax.ShapeDtypeStruct((b, t, db), BF16)
    return pl.pallas_call(
        functools.partial(_front_kernel, ctx=ctx),
        grid=(b, t // tm),
        in_specs=[pl.BlockSpec((None, tm, d), lambda bi, i: (bi, i, 0)),
                  pl.BlockSpec((None, SUBLANES, d), lambda bi, i: (bi, jnp.maximum(i * per - 1, 0), 0)),
                  pl.BlockSpec((None, SUBLANES, d), lambda bi, i: (bi, jnp.minimum((i + 1) * per, nb - 1), 0))]
        + [full(a) for a in consts],
        out_specs=[pl.BlockSpec((None, tm, nrest), lambda bi, i: (bi, i, 0)), tok, tok, tok, dirtok, dirtok,
                   pl.BlockSpec((None, 2, tm // CHUNK, db, 4 * CHUNK), lambda bi, i: (bi, 0, i, 0, 0)),
                   pl.BlockSpec((None, 2, tm // CHUNK, SUBLANES, db), lambda bi, i: (bi, 0, i, 0, 0))],
        out_shape=[jax.ShapeDtypeStruct((b, t, nrest), BF16), tok_b, tok_b, tok_b, dir_shape, dir_shape,
                   jax.ShapeDtypeStruct((b, 2, nc, db, 4 * CHUNK), BF16),
                   jax.ShapeDtypeStruct((b, 2, nc, SUBLANES, db), F32)],
        compiler_params=pltpu.CompilerParams(dimension_semantics=("parallel", "parallel"),
                                             vmem_limit_bytes=VMEM_LIMIT),
        name="front_ctx" if ctx else "front_latent",
    )(x, x, x, *consts)


SCAN_PARTS = 1
SCAN_CHUNKS_PER_STEP = 4


def _scan_kernel(at0, rt0, bk0, v0, gc0, at1, rt1, bk1, v1, gc1, s0_ref,
                 hm_ref, ms_ref, mi_ref, idw_ref, bdm_ref, eye_ref,
                 of_ref, or_ref, st_ref):
    @pl.when(pl.program_id(0) == 0)
    def _():
        st_ref[...] = s0_ref[...]

    npart = SCAN_PARTS
    pair = GROUP // 2
    bdm_b = bdm_ref[...]
    bd2 = bdm_b[0:pair, 0:pair].astype(F32)
    eye2 = eye_ref[0:pair, 0:pair]

    def bd(x):
        return [jnp.concatenate([part * hm_ref[h] for h in range(HEADS_PER_GROUP)], axis=0)
                for part in _split(x, npart)]

    def cat_parts(plists):
        return [jnp.concatenate(ps, axis=-1) for ps in zip(*plists)]

    steps = int(np.log2(CHUNK)) - 1

    def load(c):
        at_r, rt_r, bk_r, v_r, gc_r = c["refs"]
        bi, g, sl, q, rows = c["bi"], c["g"], c["sl"], c["q"], c["rows"]
        c.update(at=at_r[bi, rows, sl], rt=rt_r[bi, rows, sl], v=v_r[bi, rows, sl], gc=gc_r[bi, q, 0:1, sl],
                 bk=bk_r[bi, q, g * GROUP:(g + 1) * GROUP, :])

    def interactions(c):
        ms, mi = ms_ref[c["d"]], mi_ref[c["d"]]
        bk = c["bk"]
        wts = jnp.concatenate([jnp.concatenate([bk[:, 0:pair]] * 2, axis=1) * bdm_b,
                               jnp.concatenate([bk[:, pair:]] * 2, axis=1) * bdm_b], axis=1)
        gmat = _dot(jnp.concatenate([c["at"], c["rt"]], axis=0), wts, False)
        lab = gmat[0:CHUNK, 0:GROUP] * ms
        c.update(lak=gmat[0:CHUNK, GROUP:] * ms, mrb=gmat[CHUNK:, 0:GROUP] * mi,
                 mrk=gmat[CHUNK:, GROUP:] * mi, pw=lab, tinv=idw_ref[...] + lab)

    def invert(s):
        def stage(c):
            pw, tinv = c["pw"], c["tinv"]
            wts = bd(pw)
            if s == 0:
                pw = _mmp(_split(pw, npart), wts)
            elif s < steps:
                res = _mmp(_split(jnp.concatenate([pw, tinv], axis=0), npart), wts)
                pw, tinv = res[0:CHUNK], tinv + res[CHUNK:]
            else:
                tinv = tinv + _mmp(_split(tinv, npart), wts)
            c.update(pw=pw, tinv=tinv)
        return stage

    def values(c):
        lv_mv = _mmp(_split(jnp.concatenate([c["lak"], c["mrk"]], axis=0), npart), bd(c["v"]))
        c.update(lv=lv_mv[0:CHUNK], mv=lv_mv[CHUNK:])

    def solve(c):
        aw = _mmp(_split(c["tinv"], npart), cat_parts([bd(c["at"]), bd(c["lv"])]))
        c.update(ahat=aw[:, 0:GROUP], wv=aw[:, GROUP:])

    def outputs(c):
        ro = _mmp(_split(c["mrb"], npart), cat_parts([bd(c["ahat"]), bd(c["wv"])]))
        c.update(rhat=(c["rt"].astype(F32) + ro[:, 0:GROUP]) * c["gc"], ointra=ro[:, GROUP:] + c["mv"])

    def transition(c):
        zero_rows = jnp.zeros((CHUNK, GROUP), BF16)
        zero_half = jnp.zeros((CHUNK, pair), BF16)
        zero_blk = jnp.zeros((pair, pair), F32)
        pms, qms = [], []
        for p in range(2):
            ps = slice(p * pair, (p + 1) * pair)
            top = jnp.concatenate([c["ahat"][:, ps], c["wv"][:, ps]], axis=1).astype(BF16)
            mid = jnp.concatenate([zero_half, c["v"][:, ps]], axis=1)
            rhs = jnp.concatenate([top, zero_rows, mid, zero_rows], axis=0)
            pq = _dot(c["bk"][ps, :], rhs, False)
            pblk = (pq[:, 0:pair] * bd2 + eye2) * c["gc"][:, ps]
            qblk = pq[:, pair:] * bd2
            pms.append(jnp.concatenate([pblk, zero_blk] if p == 0 else [zero_blk, pblk], axis=1))
            qms.append(jnp.concatenate([qblk, zero_blk] if p == 0 else [zero_blk, qblk], axis=1))
        c.update(pm=jnp.concatenate(pms, axis=0), qm=jnp.concatenate(qms, axis=0))

    def advance(c, st):
        rs = _dot(jnp.concatenate([c["rhat"], c["pm"]], axis=0).astype(BF16), st.astype(BF16), False)
        c["out"][c["bi"], c["rows"], c["sl"]] = (rs[0:CHUNK] + c["ointra"]).astype(BF16)
        return rs[CHUNK:] + c["qm"]

    nq = at0.shape[1] // CHUNK
    dirs = ((at0, rt0, bk0, v0, gc0, of_ref), (at1, rt1, bk1, v1, gc1, or_ref))
    groups = {}
    for bi in range(at0.shape[0]):
        for d, refs in enumerate(dirs):
            for g in range(at0.shape[-1] // GROUP):
                order = range(nq) if d == 0 else reversed(range(nq))
                groups[(bi, d, g)] = [
                    dict(q=q, rows=slice(q * CHUNK, (q + 1) * CHUNK), bi=bi, d=d, g=g,
                         sl=slice(g * GROUP, (g + 1) * GROUP), refs=refs[:5], out=refs[5]) for q in order]
    chains = [c for k in range(nq) for cs in groups.values() for c in [cs[k]]]
    stages = ([load, interactions] + [invert(s) for s in range(steps + 1)]
              + [values, solve, outputs, transition])
    for stage in stages:
        for c in chains:
            stage(c)
    states = {key: st_ref[key[0], key[1], key[2]] for key in groups}
    for k in range(nq):
        for key, cs in groups.items():
            states[key] = advance(cs[k], states[key])
    for key in groups:
        st_ref[key[0], key[1], key[2]] = states[key]


def _scan_constants():
    t = np.arange(CHUNK)[:, None]
    lane = np.arange(GROUP)[None, :]
    s = lane % CHUNK
    strict = np.stack([s < t, s > t]).astype(np.float32)
    incl = np.stack([s <= t, s >= t]).astype(np.float32)
    idw = (s == t).astype(np.float32)
    hm = np.stack([np.broadcast_to((lane // RWKV_HEAD) == h, (CHUNK, GROUP)) for h in range(HEADS_PER_GROUP)])
    bdm = np.kron(np.eye(HEADS_PER_GROUP), np.ones((RWKV_HEAD, RWKV_HEAD)))
    return (jnp.asarray(hm.astype(np.float32), BF16), jnp.asarray(strict), jnp.asarray(incl), jnp.asarray(idw),
            jnp.asarray(bdm, BF16), jnp.eye(GROUP, dtype=F32))


def _rwkv_scan(at, rt, bk, v, gc, s0):
    b, _, t, db = at.shape
    nc = t // CHUNK
    consts = _scan_constants()
    full = lambda a: pl.BlockSpec(a.shape, lambda i: (0,) * a.ndim)
    nq = min(SCAN_CHUNKS_PER_STEP, nc)
    ns = nc // nq
    assert nc % nq == 0
    fwd, rev = (lambda i: i), (lambda i: ns - 1 - i)
    dspec = lambda d, ci: pl.BlockSpec((b, None, nq * CHUNK, db), lambda i: (0, d, ci(i), 0))
    kspec = lambda d, ci: pl.BlockSpec((b, None, nq, db, 4 * CHUNK), lambda i: (0, d, ci(i), 0, 0))
    vspec = lambda ci: pl.BlockSpec((b, nq * CHUNK, db), lambda i: (0, ci(i), 0))
    gspec = lambda d, ci: pl.BlockSpec((b, None, nq, SUBLANES, db), lambda i: (0, d, ci(i), 0, 0))
    in_specs = ([dspec(0, fwd), dspec(0, fwd), kspec(0, fwd), vspec(fwd), gspec(0, fwd),
                 dspec(1, rev), dspec(1, rev), kspec(1, rev), vspec(rev), gspec(1, rev), full(s0)]
                + [full(a) for a in consts])
    return pl.pallas_call(
        _scan_kernel,
        grid=(ns,),
        in_specs=in_specs,
        out_specs=[vspec(fwd), vspec(rev), full(s0)],
        out_shape=[jax.ShapeDtypeStruct((b, t, db), BF16), jax.ShapeDtypeStruct((b, t, db), BF16),
                   jax.ShapeDtypeStruct(s0.shape, F32)],
        compiler_params=pltpu.CompilerParams(dimension_semantics=("arbitrary",),
                                             vmem_limit_bytes=VMEM_LIMIT),
        name="rwkv_scan",
    )(at, rt, bk, v, gc, at, rt, bk, v, gc, s0, *consts)


def _conv_kernel(glu_ref, w_ref, b_ref, o_ref, h_ref, *, stride):
    t, c = glu_ref.shape
    halo = (CONF_CONV_W // 2) * stride
    nsteps = t // CONV_ROWS
    zeros = jnp.zeros((halo, c), F32)
    h_ref[0:halo, :] = zeros
    h_ref[halo + t:halo + t + halo, :] = zeros

    def widen(j, carry):
        r0 = pl.multiple_of(j * CONV_ROWS, CONV_ROWS)
        h_ref[pl.ds(halo + r0, CONV_ROWS), :] = glu_ref[pl.ds(r0, CONV_ROWS), :].astype(F32)
        return carry

    lax.fori_loop(0, nsteps, widen, 0)
    w = w_ref[...]
    bias = b_ref[...]

    def conv(j, carry):
        r0 = pl.multiple_of(j * CONV_ROWS, CONV_ROWS)
        acc = jnp.broadcast_to(bias, (CONV_ROWS, c))
        for tap in range(CONF_CONV_W):
            acc = acc + h_ref[pl.ds(r0 + tap * stride, CONV_ROWS), :] * w[tap:tap + 1, :]
        o_ref[pl.ds(r0, CONV_ROWS), :] = acc.astype(BF16)
        return carry

    lax.fori_loop(0, nsteps, conv, 0)


def _conformer_conv(proj, dw_w, dw_b, *, stride):
    b, t, n = proj.shape
    db = dw_w.shape[-1]
    lanes = LANES
    assert t % CONV_ROWS == 0 and stride % SUBLANES == 0
    halo = (CONF_CONV_W // 2) * stride
    c1 = db // lanes
    return pl.pallas_call(
        functools.partial(_conv_kernel, stride=stride),
        grid=(b, db // lanes),
        in_specs=[pl.BlockSpec((None, t, lanes), lambda bi, j: (bi, 0, c1 + j)),
                  pl.BlockSpec((_round_up(CONF_CONV_W, SUBLANES), lanes), lambda bi, j: (0, j)),
                  pl.BlockSpec((1, lanes), lambda bi, j: (0, j))],
        out_specs=pl.BlockSpec((None, t, lanes), lambda bi, j: (bi, 0, j)),
        out_shape=jax.ShapeDtypeStruct((b, t, db), BF16),
        scratch_shapes=[pltpu.VMEM((t + 2 * halo, lanes), F32)],
        compiler_params=pltpu.CompilerParams(dimension_semantics=("parallel", "parallel"),
                                             vmem_limit_bytes=VMEM_LIMIT),
        name="conformer_conv",
    )(proj, _pad_to(dw_w, 0, _round_up(CONF_CONV_W, SUBLANES)), dw_b.reshape(1, db))


def _round_up(x, m):
    return (x + m - 1) // m * m


def _merge_kernel(x_ref, of_ref, or_ref, g_ref, bonus_ref, conv_ref, zc_ref, zd_ref, mod_ref,
                  lnxg_ref, lnxb_ref, clng_ref, clnb_ref, wout_ref, fg_ref, seg_ref, o_ref):
    tm, d = x_ref.shape
    mrow = _mod_row(mod_ref, False)
    seg = seg_ref[...]

    def centre(c):
        o = of_ref[c["rows"], :].astype(F32) + or_ref[c["rows"], :].astype(F32)
        c.update(dev=o - _mm(o, seg))

    def normalise(c):
        rows, dev = c["rows"], c["dev"]
        var = _mm(dev * dev, seg)
        on = dev * lax.rsqrt(var + LNX_EPS) * lnxg_ref[...] + lnxb_ref[...]
        y_c = ((on + bonus_ref[rows, :].astype(F32)) * g_ref[rows, :].astype(F32)
               * _silu(zc_ref[rows, :].astype(F32)))
        cv = conv_ref[rows, :].astype(F32)
        cdev = cv - jnp.mean(cv, axis=-1, keepdims=True)
        cvar = jnp.mean(cdev * cdev, axis=-1, keepdims=True)
        ln = cdev * lax.rsqrt(cvar + CONF_LN_EPS) * clng_ref[...] + clnb_ref[...]
        y_d = _silu(ln) * _silu(zd_ref[rows, :].astype(F32))
        c.update(y=jnp.concatenate([y_c, y_d], axis=-1))

    def emit(c):
        y = _mm(c["y"], wout_ref[...])
        o_ref[c["rows"], :] = _rmsnorm(x_ref[c["rows"], :] + mrow[:, 2 * d:3 * d] * y, fg_ref[...])

    ts = tm // MERGE_SUBTILES
    subs = [dict(rows=slice(s * ts, (s + 1) * ts)) for s in range(MERGE_SUBTILES)]
    for stage in (centre, normalise, emit):
        for c in subs:
            stage(c)


def _merge(x, o_f, o_r, g, bonus, conv, proj, mod, p, final_g):
    b, t, d = x.shape
    db = d // 2
    tm = min(4 * TILE_TOKENS, t)
    heads = db // RWKV_HEAD
    seg = jnp.asarray(np.kron(np.eye(heads), np.full((RWKV_HEAD, RWKV_HEAD), 1.0 / RWKV_HEAD)), BF16)
    consts = (mod, p["lnx_g"].reshape(1, db), p["lnx_b"].reshape(1, db), p["conf_ln_g"].reshape(1, db),
              p["conf_ln_b"].reshape(1, db), p["out_o"].astype(BF16), final_g.reshape(1, d), seg)
    full = lambda a: pl.BlockSpec(a.shape, lambda bi, i: (0,) * a.ndim, pipeline_mode=pl.Buffered(1))
    tok = pl.BlockSpec((None, tm, db), lambda bi, i: (bi, i, 0))
    col = lambda cblk: pl.BlockSpec((None, tm, db), lambda bi, i: (bi, i, cblk))
    return pl.pallas_call(
        _merge_kernel,
        grid=(b, t // tm),
        in_specs=[pl.BlockSpec((None, tm, d), lambda bi, i: (bi, i, 0)), tok, tok, tok, tok, tok,
                  col(0), col(2)] + [full(a) for a in consts],
        out_specs=pl.BlockSpec((None, tm, d), lambda bi, i: (bi, i, 0)),
        out_shape=jax.ShapeDtypeStruct((b, t, d), F32),
        compiler_params=pltpu.CompilerParams(dimension_semantics=("parallel", "parallel"),
                                             vmem_limit_bytes=VMEM_LIMIT),
        name="merge",
    )(x, o_f, o_r, g, bonus, conv, proj, proj, *consts)


def kernel(x, c, ctx, c_ctx, ada_w_e, ada_b_e, norm_e, in_e, out_e, pool_w, pool_scale, sconv_w,
           ada_w_o, ada_b_o, norm_o, in_o, out_o, rwkv_mu, w0, w1, w2, a0, a1, a2, g1, g2,
           k_k, k_a, r_k, lnx_g, lnx_b, conf_dw_w, conf_dw_b, conf_ln_g, conf_ln_b, final_g):
    b, t, d = x.shape
    db = d // 2
    assert b == 2
    cvec = _pad_to(jnp.concatenate([c, c_ctx[None, :]], axis=0), 0, SUBLANES)
    mod_e = _adaln(cvec, ada_w_e[0], ada_b_e[0])
    mod_o = _adaln(cvec, ada_w_o[0], ada_b_o[0])

    even = dict(norm_g=norm_e[0], w_in=in_e[0], w_out=out_e[0], pool_w=pool_w[0],
                pool_scale=pool_scale[0], sconv_w=sconv_w[0])
    x1 = _even_layer(x, mod_e, ctx=False, line=GRID_W, **even)
    xc1 = _even_layer(ctx, mod_e, ctx=True, line=ctx.shape[1], **even)

    p = dict(rwkv_mu=rwkv_mu[0], w0=w0[0], w1=w1[0], w2=w2[0], a0=a0[0], a1=a1[0], a2=a2[0],
             g1=g1[0], g2=g2[0], k_k=k_k[0], k_a=k_a[0], r_k=r_k[0], lnx_g=lnx_g[0], lnx_b=lnx_b[0],
             conf_ln_g=conf_ln_g[0], conf_ln_b=conf_ln_b[0], out_o=out_o[0])

    _, v_c, _, _, at_c, rt_c, bk_c, gc_c = _odd_front(xc1, mod_o, norm_o[0], in_o[0], p, ctx=True)
    zero_state = jnp.zeros((b, 2, db // GROUP, GROUP, GROUP), F32)
    _, _, ctx_state = _rwkv_scan(at_c, rt_c, bk_c, v_c, gc_c, zero_state)

    proj, v_l, g_l, bonus, at_l, rt_l, bk_l, gc_l = _odd_front(x1, mod_o, norm_o[0], in_o[0], p, ctx=False)
    o_f, o_r, _ = _rwkv_scan(at_l, rt_l, bk_l, v_l, gc_l, ctx_state)
    conv = _conformer_conv(proj, conf_dw_w[0], conf_dw_b[0], stride=GRID_W)
    return _merge(x1, o_f, o_r, g_l, bonus, conv, proj, mod_o, p, final_g)
```
